```python
import math
import jax, jax.numpy as jnp
from jax import lax
import numpy as np

D_MODEL = 2048
BATCH = 4
SEQ = 2048
DEPTH = 1
DEC_BATCH = 128
DEC_SEQ = 1
PAST_LEN = 16384
PAGE_SIZE = 128

D_RWKV = D_MODEL // 2
RWKV_HEAD = 64
N_RWKV_HEADS = D_RWKV // RWKV_HEAD
D_CONV = D_MODEL - D_RWKV
D_MIX = D_RWKV + D_CONV
D_IN = 3 * D_RWKV + 2 * D_CONV
DECAY_LORA = 64
AAA_LORA = 64
GATE_LORA = 160
CONV_WIDTH = 31
N_EXPERTS = 32
TOP_K = 4
D_FF = D_MODEL
SWIGLU_LIMIT = 7.0
SWIGLU_ALPHA = 1.702
MOE_BLOCK = 128
LN_EPS = 1e-5
GN_EPS = 64e-5
DEEPNORM_ALPHA = (2.0 * DEPTH) ** 0.25
DEEPNORM_BETA = (8.0 * DEPTH) ** -0.25

kernel_name = "rwkv7_conformer_hymba_moe_step"


def layer_norm(x, g, b, eps=LN_EPS):
    xf = x.astype(jnp.float32)
    mu = xf.mean(-1, keepdims=True)
    var = jnp.square(xf - mu).mean(-1, keepdims=True)
    return ((xf - mu) * lax.rsqrt(var + eps) * g + b).astype(x.dtype)


def token_shift(seq, prev):
    return jnp.concatenate([prev[:, None].astype(seq.dtype), seq[:, :-1]], axis=1)


def wkv_step(S, inp):
    r, d, k, v, av, bv = inp
    sa = jnp.einsum('bhvk,bhk->bhv', S, av)
    S = S * d[:, :, None, :] + sa[..., None] * bv[:, :, None, :] + v[..., None] * k[:, :, None, :]
    y = jnp.einsum('bhvk,bhk->bhv', S, r)
    return S, y


def rwkv7_time_mix(x, u_rkv, shift_x, shift_rkv, S0, mu_x, mu_rkv, w0, w_A, w_B, a0, a_A, a_B,
                   g_A, g_B, k_k, k_a, r_k, gn_g, gn_b):
    B, T, _ = x.shape
    f32 = jnp.float32
    H, N = N_RWKV_HEADS, RWKV_HEAD
    xx = token_shift(x, shift_x) - x
    xw = x + xx * mu_x[0]
    xa = x + xx * mu_x[1]
    xg = x + xx * mu_x[2]
    rkv = u_rkv + (token_shift(u_rkv, shift_rkv) - u_rkv) * mu_rkv
    r, k, v = [t.astype(f32).reshape(B, T, H, N) for t in jnp.split(rkv, 3, axis=-1)]
    w_log = -jax.nn.softplus(-(w0 + jnp.tanh(xw @ w_A) @ w_B).astype(f32)) - 0.5
    decay = jnp.exp(-jnp.exp(w_log)).reshape(B, T, H, N)
    a = jax.nn.sigmoid((a0 + (xa @ a_A) @ a_B).astype(f32)).reshape(B, T, H, N)
    g = (jax.nn.sigmoid(xg @ g_A) @ g_B).astype(f32)
    kk = k * k_k.reshape(H, N).astype(f32)
    kk = kk * lax.rsqrt(jnp.maximum(jnp.sum(kk * kk, -1, keepdims=True), 1e-24))
    k = k * (1.0 + (a - 1.0) * k_a.reshape(H, N).astype(f32))
    xs = tuple(jnp.moveaxis(t, 1, 0) for t in (r, decay, k, v, -kk, kk * a))
    S_T, ys = lax.scan(wkv_step, S0.astype(f32), xs)
    y = jnp.moveaxis(ys, 0, 1)
    mu = y.mean(-1, keepdims=True)
    var = jnp.square(y - mu).mean(-1, keepdims=True)
    y = ((y - mu) * lax.rsqrt(var + GN_EPS)).reshape(B, T, D_RWKV) * gn_g + gn_b
    bonus = jnp.sum(r * k * r_k.reshape(H, N).astype(f32), -1, keepdims=True) * v
    y = (y + bonus.reshape(B, T, D_RWKV)) * g
    return y.astype(x.dtype), x[:, -1], u_rkv[:, -1], S_T.astype(S0.dtype)


def conformer_conv(u_glu, conv_buf, conv_w, conv_b, ln_g, ln_b):
    val, gate = jnp.split(u_glu, 2, axis=-1)
    h = val * jax.nn.sigmoid(gate)
    hp = jnp.concatenate([conv_buf.astype(h.dtype), h], axis=1)
    out = lax.conv_general_dilated(hp, conv_w[:, None, :].astype(h.dtype), window_strides=(1,), padding='VALID',
                                   dimension_numbers=('NWC', 'WIO', 'NWC'), feature_group_count=D_CONV) + conv_b
    out = jax.nn.silu(layer_norm(out, ln_g, ln_b))
    return out, hp[:, -(CONV_WIDTH - 1):]


def routed_experts(h, w_router, b_router, w_gu, b_gu, w_down, b_down):
    B, T, D = h.shape
    n_tok = B * T
    xt = h.reshape(n_tok, D)
    logits = (xt @ w_router).astype(jnp.float32) + b_router.astype(jnp.float32)
    top_val, top_idx = lax.top_k(logits, TOP_K)
    gates = jax.nn.softmax(top_val, axis=-1)
    n_assign = n_tok * TOP_K
    e_flat = top_idx.reshape(n_assign).astype(jnp.int32)
    tok_flat = jnp.repeat(jnp.arange(n_tok, dtype=jnp.int32), TOP_K)
    g_flat = gates.reshape(n_assign)
    onehot = (e_flat[:, None] == jnp.arange(N_EXPERTS, dtype=jnp.int32)[None, :]).astype(jnp.int32)
    counts = onehot.sum(0)
    rank = jnp.take_along_axis(jnp.cumsum(onehot, axis=0), e_flat[:, None], axis=1)[:, 0] - 1
    padded = (counts + MOE_BLOCK - 1) // MOE_BLOCK * MOE_BLOCK
    pad_end = jnp.cumsum(padded)
    pad_start = pad_end - padded
    row = pad_start[e_flat] + rank
    n_blocks = -(-n_assign // MOE_BLOCK) + N_EXPERTS
    n_rows = n_blocks * MOE_BLOCK
    row_tok = jnp.full((n_rows,), n_tok, jnp.int32).at[row].set(tok_flat)
    row_gate = jnp.zeros((n_rows,), jnp.float32).at[row].set(g_flat)
    block_expert = jnp.minimum(
        jnp.searchsorted(pad_end, jnp.arange(n_blocks, dtype=jnp.int32) * MOE_BLOCK, side='right'),
        N_EXPERTS - 1).astype(jnp.int32)
    x_rows = jnp.concatenate([xt, jnp.zeros((1, D), xt.dtype)], 0)[row_tok].reshape(n_blocks, MOE_BLOCK, D)

    def expert_block(args):
        xb, e = args
        gu = xb @ w_gu[e] + b_gu[e]
        x_glu, x_lin = jnp.split(gu, 2, axis=-1)
        x_glu = jnp.minimum(x_glu, SWIGLU_LIMIT)
        x_lin = jnp.clip(x_lin, -SWIGLU_LIMIT, SWIGLU_LIMIT)
        act = x_glu * jax.nn.sigmoid(SWIGLU_ALPHA * x_glu) * (x_lin + 1.0)
        return act @ w_down[e] + b_down[e]

    y_rows = lax.map(expert_block, (x_rows, block_expert)).reshape(n_rows, D)
    y_rows = y_rows * row_gate[:, None].astype(y_rows.dtype)
    y = jax.ops.segment_sum(y_rows, row_tok, num_segments=n_tok + 1)[:n_tok]
    return y.reshape(B, T, D)


def hybrid_layer(x, shift_x, shift_rkv, wkv, conv_buf, w):
    (w_in, mu_x, mu_rkv, w0, w_A, w_B, a0, a_A, a_B, g_A, g_B, k_k, k_a, r_k, gn_g, gn_b,
     conv_w, conv_b, conv_ln_g, conv_ln_b, w_out, ln1_g, ln1_b,
     w_router, b_router, w_gu, b_gu, w_down, b_down, ln2_g, ln2_b) = w
    u = x @ w_in
    u_rkv, u_glu = u[..., :3 * D_RWKV], u[..., 3 * D_RWKV:]
    y_rwkv, new_sx, new_srkv, new_wkv = rwkv7_time_mix(
        x, u_rkv, shift_x, shift_rkv, wkv, mu_x, mu_rkv, w0, w_A, w_B, a0, a_A, a_B,
        g_A, g_B, k_k, k_a, r_k, gn_g, gn_b)
    y_conv, new_buf = conformer_conv(u_glu, conv_buf, conv_w, conv_b, conv_ln_g, conv_ln_b)
    mix = jnp.concatenate([y_rwkv, y_conv], axis=-1) @ w_out
    h = layer_norm(DEEPNORM_ALPHA * x + mix, ln1_g, ln1_b)
    ffn = routed_experts(h, w_router, b_router, w_gu, b_gu, w_down, b_down)
    y = layer_norm(DEEPNORM_ALPHA * h + ffn, ln2_g, ln2_b)
    return y, (new_sx, new_srkv, new_wkv, new_buf)


def run_group(x, sx, srkv, swkv, sconv, layers):
    new_sx, new_srkv, new_wkv, new_conv = [], [], [], []
    for l in range(DEPTH):
        x, (a, b, c, d) = hybrid_layer(x, sx[l], srkv[l], swkv[l], sconv[l], layers[l])
        new_sx.append(a)
        new_srkv.append(b)
        new_wkv.append(c)
        new_conv.append(d)
    return x, jnp.stack(new_sx), jnp.stack(new_srkv), jnp.stack(new_wkv), jnp.stack(new_conv)


def setup_inputs(seed: int = 0) -> dict:
    key = jax.random.key(seed)
    ks = iter(jax.random.split(key, 48))
    f32 = jnp.float32
    L, D, H, N = DEPTH, D_MODEL, N_RWKV_HEADS, RWKV_HEAD

    def nrm(shape, scale):
        return jax.random.normal(next(ks), shape, f32) * scale

    def unif(shape, lo, hi):
        return jax.random.uniform(next(ks), shape, f32, lo, hi)

    col_scale = jnp.concatenate([jnp.ones((2 * D_RWKV,), f32), jnp.full((D_RWKV,), DEEPNORM_BETA, f32),
                                 jnp.ones((2 * D_CONV,), f32)])
    return {
        "x_prompt": nrm((BATCH, SEQ, D), 1.0),
        "x_sample": nrm((DEC_BATCH, DEC_SEQ, D), 1.0),
        "state_shift_x": nrm((L, DEC_BATCH, D), 1.0),
        "state_shift_rkv": nrm((L, DEC_BATCH, 3 * D_RWKV), 1.0),
        "state_wkv": nrm((L, DEC_BATCH, H, N, N), 0.1),
        "state_conv": nrm((L, DEC_BATCH, CONV_WIDTH - 1, D_CONV), 0.5),
        "w_in": nrm((L, D, D_IN), D ** -0.5) * col_scale,
        "mu_x": unif((L, 3, D), 0.0, 1.0),
        "mu_rkv": unif((L, 3 * D_RWKV), 0.0, 1.0),
        "w0": unif((L, D_RWKV), -5.0, -1.0),
        "w_A": nrm((L, D, DECAY_LORA), D ** -0.5),
        "w_B": nrm((L, DECAY_LORA, D_RWKV), 0.5 * DECAY_LORA ** -0.5),
        "a0": nrm((L, D_RWKV), 0.5),
        "a_A": nrm((L, D, AAA_LORA), D ** -0.5),
        "a_B": nrm((L, AAA_LORA, D_RWKV), 0.5 * AAA_LORA ** -0.5),
        "g_A": nrm((L, D, GATE_LORA), D ** -0.5),
        "g_B": nrm((L, GATE_LORA, D_RWKV), GATE_LORA ** -0.5),
        "k_k": 0.85 + nrm((L, D_RWKV), 0.05),
        "k_a": 1.0 + nrm((L, D_RWKV), 0.05),
        "r_k": nrm((L, D_RWKV), 0.1),
        "gn_g": 1.0 + nrm((L, D_RWKV), 0.05),
        "gn_b": nrm((L, D_RWKV), 0.01),
        "conv_w": nrm((L, CONV_WIDTH, D_CONV), CONV_WIDTH ** -0.5),
        "conv_b": nrm((L, D_CONV), 0.01),
        "conv_ln_g": 1.0 + nrm((L, D_CONV), 0.05),
        "conv_ln_b": nrm((L, D_CONV), 0.01),
        "w_out": nrm((L, D_MIX, D), DEEPNORM_BETA * D_MIX ** -0.5),
        "ln1_g": 1.0 + nrm((L, D), 0.05),
        "ln1_b": nrm((L, D), 0.01),
        "w_router": nrm((L, D, N_EXPERTS), D ** -0.5),
        "b_router": nrm((L, N_EXPERTS), 0.01),
        "w_gu": nrm((L, N_EXPERTS, D, 2 * D_FF), DEEPNORM_BETA * D ** -0.5),
        "b_gu": nrm((L, N_EXPERTS, 2 * D_FF), 0.01),
        "w_down": nrm((L, N_EXPERTS, D_FF, D), DEEPNORM_BETA * D_FF ** -0.5),
        "b_down": nrm((L, N_EXPERTS, D), 0.01),
        "ln2_g": 1.0 + nrm((L, D), 0.05),
        "ln2_b": nrm((L, D), 0.01),
    }


def reference(x_prompt, x_sample, state_shift_x, state_shift_rkv, state_wkv, state_conv,
              w_in, mu_x, mu_rkv, w0, w_A, w_B, a0, a_A, a_B, g_A, g_B, k_k, k_a, r_k, gn_g, gn_b,
              conv_w, conv_b, conv_ln_g, conv_ln_b, w_out, ln1_g, ln1_b,
              w_router, b_router, w_gu, b_gu, w_down, b_down, ln2_g, ln2_b):
    params = (w_in, mu_x, mu_rkv, w0, w_A, w_B, a0, a_A, a_B, g_A, g_B, k_k, k_a, r_k, gn_g, gn_b,
              conv_w, conv_b, conv_ln_g, conv_ln_b, w_out, ln1_g, ln1_b,
              w_router, b_router, w_gu, b_gu, w_down, b_down, ln2_g, ln2_b)
    layers = [tuple(p[l] for p in params) for l in range(DEPTH)]
    dt = x_prompt.dtype
    z_sx = jnp.zeros((DEPTH, BATCH, D_MODEL), dt)
    z_srkv = jnp.zeros((DEPTH, BATCH, 3 * D_RWKV), dt)
    z_wkv = jnp.zeros((DEPTH, BATCH, N_RWKV_HEADS, RWKV_HEAD, RWKV_HEAD), state_wkv.dtype)
    z_conv = jnp.zeros((DEPTH, BATCH, CONV_WIDTH - 1, D_CONV), dt)
    y_prompt, p_sx, p_srkv, p_wkv, p_conv = run_group(x_prompt, z_sx, z_srkv, z_wkv, z_conv, layers)
    y_sample, s_sx, s_srkv, s_wkv, s_conv = run_group(x_sample, state_shift_x, state_shift_rkv,
                                                      state_wkv, state_conv, layers)
    return (y_prompt, y_sample, p_sx, p_srkv, p_wkv, p_conv, s_sx, s_srkv, s_wkv, s_conv)
```

```python
import functools
import math

import jax
import jax.numpy as jnp
from jax import lax
from jax.experimental import pallas as pl
from jax.experimental.pallas import tpu as pltpu

F32 = jnp.float32
BF16 = jnp.bfloat16

RWKV_HEAD = 64
CONV_WIDTH = 31
TOP_K = 4
SWIGLU_LIMIT = 7.0
SWIGLU_ALPHA = 1.702
LN_EPS = 1e-5
GN_EPS = 64e-5

LANES = 128
SUBLANES = 8
MXU_DIM = 256
VMEM_LIMIT_BYTES = 56 * 1024 * 1024

SCAN_CHUNK = 64
HEADS_PER_GROUP = MXU_DIM // RWKV_HEAD
CONV_HALO = 32
MOE_SUB = 256


def _row_tile(n, target):
    best = None
    for t in range(SUBLANES, min(n, target) + 1, SUBLANES):
        if n % t == 0:
            best = t
    assert best is not None, (n, target)
    return best


def _params(*sem):
    return pltpu.CompilerParams(dimension_semantics=sem, vmem_limit_bytes=VMEM_LIMIT_BYTES)


def _const_spec(shape):
    nd = len(shape)
    return pl.BlockSpec(shape, lambda *_: (0,) * nd, pipeline_mode=pl.Buffered(1))


def _dot(a, b):
    return jnp.dot(a, b, preferred_element_type=F32)


def _split2(x):
    hi = x.astype(BF16)
    lo = (x - hi.astype(F32)).astype(BF16)
    return hi, lo


def _layer_norm(z, g, b):
    mu = jnp.mean(z, axis=-1, keepdims=True)
    zc = z - mu
    var = jnp.mean(zc * zc, axis=-1, keepdims=True)
    return zc * lax.rsqrt(var + LN_EPS) * g + b


def _shift_rows(x, first_row):
    xs = pltpu.roll(x, 1, axis=0)
    row = lax.broadcasted_iota(jnp.int32, x.shape, 0)
    return jnp.where(row == 0, first_row, xs)


def _inproj_kernel(x_ref, xprev_ref, w_in_ref, mu_ref, wA_ref, aA_ref, gA_ref, wB_ref, aB_ref, gB_ref,
                   w0_ref, a0_ref, urkv_ref, hglu_ref, dlog_ref, asig_ref, g_ref, *, seq_tiles, halo):
    x = x_ref[...]
    if halo:
        first = (pl.program_id(0) % seq_tiles) == 0
        prev_row = jnp.where(first, 0.0, xprev_ref[SUBLANES - 1:SUBLANES, :])
        xprev = _shift_rows(x, prev_row)
    else:
        xprev = xprev_ref[...]
    xx = xprev - x
    xb = x.astype(BF16)
    d3 = urkv_ref.shape[1]
    dc = hglu_ref.shape[1]
    urkv_ref[...] = _dot(xb, w_in_ref[:, :d3])
    val = _dot(xb, w_in_ref[:, d3:d3 + dc])
    gate = _dot(xb, w_in_ref[:, d3 + dc:])
    hglu_ref[...] = val * jax.nn.sigmoid(gate)

    xw = (x + xx * mu_ref[0:1, :]).astype(BF16)
    tw = jnp.tanh(_dot(xw, wA_ref[...]))
    wlin = w0_ref[...] + _dot(tw.astype(BF16), wB_ref[...])
    wlog = -jax.nn.softplus(-wlin) - 0.5
    dlog_ref[...] = -jnp.exp(wlog)

    xa = (x + xx * mu_ref[1:2, :]).astype(BF16)
    ta = _dot(xa, aA_ref[...])
    asig_ref[...] = jax.nn.sigmoid(a0_ref[...] + _dot(ta.astype(BF16), aB_ref[...]))

    xg = (x + xx * mu_ref[2:3, :]).astype(BF16)
    tg = jax.nn.sigmoid(_dot(xg, gA_ref[...]))
    g_ref[...] = _dot(tg.astype(BF16), gB_ref[...])


def _inproj(x, xprev, seq_len, lw):
    n, d = x.shape
    d_in = lw["w_in"].shape[1]
    dr = lw["w_B"].shape[1]
    d3 = 3 * dr
    dc = (d_in - d3) // 2
    halo = xprev is None
    tm = _row_tile(seq_len if halo else n, 256)
    seq_tiles = (seq_len // tm) if halo else 1
    if halo:
        prev_spec = pl.BlockSpec((SUBLANES, d), lambda i: (jnp.maximum(i * (tm // SUBLANES) - 1, 0), 0))
        prev_arg = x
    else:
        prev_spec = pl.BlockSpec((tm, d), lambda i: (i, 0))
        prev_arg = xprev
    row = lambda c: pl.BlockSpec((tm, c), lambda i: (i, 0))
    outs = [jax.ShapeDtypeStruct((n, c), F32) for c in (d3, dc, dr, dr, dr)]
    consts = [lw["w_in"], lw["mu_x"], lw["w_A"], lw["a_A"], lw["g_A"], lw["w_B"], lw["a_B"], lw["g_B"],
              lw["w0"], lw["a0"]]
    return pl.pallas_call(
        functools.partial(_inproj_kernel, seq_tiles=seq_tiles, halo=halo),
        grid=(n // tm,),
        in_specs=[row(d), prev_spec] + [_const_spec(c.shape) for c in consts],
        out_specs=[row(c) for c in (d3, dc, dr, dr, dr)],
        out_shape=outs,
        compiler_params=_params("arbitrary"),
        name="inproj",
    )(x, prev_arg, *consts)


def _head_mask(n, dtype):
    r = lax.broadcasted_iota(jnp.int32, (n, n), 0) // RWKV_HEAD
    c = lax.broadcasted_iota(jnp.int32, (n, n), 1) // RWKV_HEAD
    return (r == c).astype(dtype)


def _head_sum(x, ones_bd):
    cols = []
    for c0 in range(0, x.shape[1], MXU_DIM):
        hi, lo = _split2(x[:, c0:c0 + MXU_DIM])
        cols.append(_dot(hi, ones_bd) + _dot(lo, ones_bd))
    return cols[0] if len(cols) == 1 else jnp.concatenate(cols, axis=1)


def _scanprep_kernel(u_ref, uprev_ref, mu_ref, kk_ref, ka_ref, rk_ref, asig_ref,
                     r_ref, k_ref, v_ref, a_ref, b_ref, bonus_ref, *, seq_tiles, halo):
    u = u_ref[...]
    if halo:
        first = (pl.program_id(0) % seq_tiles) == 0
        prev_row = jnp.where(first, 0.0, uprev_ref[SUBLANES - 1:SUBLANES, :])
        uprev = _shift_rows(u, prev_row)
    else:
        uprev = uprev_ref[...]
    rkv = u + (uprev - u) * mu_ref[...]
    dr = r_ref.shape[1]
    r = rkv[:, :dr]
    k = rkv[:, dr:2 * dr]
    v = rkv[:, 2 * dr:]
    asig = asig_ref[...]
    ones_bd = _head_mask(MXU_DIM, BF16)
    kk = k * kk_ref[...]
    kk = kk * lax.rsqrt(jnp.maximum(_head_sum(kk * kk, ones_bd), 1e-24))
    k = k * (1.0 + (asig - 1.0) * ka_ref[...])
    r_ref[...] = r
    k_ref[...] = k
    v_ref[...] = v
    a_ref[...] = -kk
    b_ref[...] = kk * asig
    bonus_ref[...] = _head_sum(r * k * rk_ref[...], ones_bd) * v


def _scanprep(u_rkv, uprev, seq_len, asig, lw):
    n, d3 = u_rkv.shape
    dr = d3 // 3
    halo = uprev is None
    tm = _row_tile(seq_len if halo else n, 256)
    seq_tiles = (seq_len // tm) if halo else 1
    if halo:
        prev_spec = pl.BlockSpec((SUBLANES, d3), lambda i: (jnp.maximum(i * (tm // SUBLANES) - 1, 0), 0))
        prev_arg = u_rkv
    else:
        prev_spec = pl.BlockSpec((tm, d3), lambda i: (i, 0))
        prev_arg = uprev
    row = lambda c: pl.BlockSpec((tm, c), lambda i: (i, 0))
    consts = [lw["mu_rkv"], lw["k_k"], lw["k_a"], lw["r_k"]]
    return pl.pallas_call(
        functools.partial(_scanprep_kernel, seq_tiles=seq_tiles, halo=halo),
        grid=(n // tm,),
        in_specs=[row(d3), prev_spec] + [_const_spec(c.shape) for c in consts] + [row(dr)],
        out_specs=[row(dr)] * 6,
        out_shape=[jax.ShapeDtypeStruct((n, dr), F32)] * 6,
        compiler_params=_params("arbitrary"),
        name="scanprep",
    )(u_rkv, prev_arg, *consts, asig)


def _blockdiag(x, mask):
    reps = mask.shape[0] // x.shape[0]
    return jnp.concatenate([x] * reps, axis=0) * mask


def _dot_nt(a, b):
    return lax.dot_general(a, b, (((1,), (1,)), ((), ())), preferred_element_type=F32)


def _dot_tn(a, b):
    return lax.dot_general(a, b, (((0,), (0,)), ((), ())), preferred_element_type=F32)


def _scan_kernel(r_ref, dl_ref, k_ref, v_ref, a_ref, b_ref, y_ref, s_ref, h_scr):
    c = pl.program_id(1)
    nb = r_ref.shape[0]
    C = r_ref.shape[1]
    W = r_ref.shape[2]

    @pl.when(c == 0)
    def _():
        h_scr[...] = jnp.zeros_like(h_scr)

    ti = lax.broadcasted_iota(jnp.int32, (C, C), 0)
    tj = lax.broadcasted_iota(jnp.int32, (C, C), 1)
    tri = (ti >= tj).astype(BF16)
    t_row = lax.broadcasted_iota(jnp.int32, (C, W), 0)
    j_col = lax.broadcasted_iota(jnp.int32, (C, W), 1) % C
    strict = j_col < t_row
    incl = j_col <= t_row
    eye_cat = (j_col == t_row).astype(F32)
    rb = lax.broadcasted_iota(jnp.int32, (HEADS_PER_GROUP * C, W), 0) // C
    cb = lax.broadcasted_iota(jnp.int32, (HEADS_PER_GROUP * C, W), 1) // RWKV_HEAD
    bmask = (rb == cb).astype(BF16)
    hmask = _head_mask(W, F32)

    for bi in range(nb):
        r = r_ref[bi]
        dl = dl_ref[bi]
        k = k_ref[bi]
        v = v_ref[bi]
        a = a_ref[bi]
        b = b_ref[bi]
        sbd = h_scr[bi]
        sbd_b = sbd.astype(BF16)

        d_hi = dl.astype(BF16)
        d_r1 = dl - d_hi.astype(F32)
        d_mid = d_r1.astype(BF16)
        d_lo = (d_r1 - d_mid.astype(F32)).astype(BF16)
        cum = _dot(tri, d_hi) + (_dot(tri, d_mid) + _dot(tri, d_lo))
        cum_last = cum[C - 1:C, :]
        e_neg = jnp.exp(-cum)
        at = (a * jnp.exp(cum - dl)).astype(BF16)
        rt = (r * jnp.exp(cum)).astype(BF16)
        bt = (b * e_neg).astype(BF16)
        kt = (k * e_neg).astype(BF16)
        e_end = jnp.exp(cum_last - cum)
        bh = (b * e_end).astype(BF16)
        kh = (k * e_end).astype(BF16)
        vb = v.astype(BF16)

        ar = jnp.concatenate([at, rt], axis=0)
        p_b = _dot_nt(ar, _blockdiag(bt, bmask))
        p_k = _dot_nt(ar, _blockdiag(kt, bmask))
        p_ab = jnp.where(strict, p_b[:C], 0.0)
        p_rb = jnp.where(incl, p_b[C:], 0.0)
        p_ak = jnp.where(strict, p_k[:C], 0.0)
        p_rk = jnp.where(incl, p_k[C:], 0.0)

        arh = _dot_nt(ar, sbd_b)
        pv = _dot(jnp.concatenate([p_ak, p_rk], axis=0).astype(BF16), _blockdiag(vb, bmask))
        w = arh[:C] + pv[:C]

        nn = p_ab
        tm = eye_cat + nn
        n_sq = int(math.log2(C))
        for i in range(1, n_sq):
            nbd = _blockdiag(nn.astype(BF16), bmask)
            nn = _dot(nn.astype(BF16), nbd)
            tm = tm + _dot(tm.astype(BF16), _blockdiag(nn.astype(BF16), bmask))
        u = _dot(tm.astype(BF16), _blockdiag(w.astype(BF16), bmask))
        ub = u.astype(BF16)

        y_ref[bi] = arh[C:] + pv[C:] + _dot(p_rb.astype(BF16), _blockdiag(ub, bmask))

        upd = _dot_tn(jnp.concatenate([ub, vb], axis=0), jnp.concatenate([bh, kh], axis=0))
        g_end = jnp.exp(cum_last)
        h_scr[bi] = (sbd * g_end + upd) * hmask

    @pl.when(c == pl.num_programs(1) - 1)
    def _():
        s_ref[...] = h_scr[...].reshape(s_ref.shape)


def _scan(r, dl, k, v, a, b):
    nb, t, dr = r.shape
    ng = dr // MXU_DIM
    C = SCAN_CHUNK
    spec = pl.BlockSpec((nb, C, MXU_DIM), lambda g, c: (0, c, g))
    return pl.pallas_call(
        _scan_kernel,
        grid=(ng, t // C),
        in_specs=[spec] * 6,
        out_specs=[spec, pl.BlockSpec((nb, 1, MXU_DIM, MXU_DIM), lambda g, c: (0, g, 0, 0))],
        out_shape=[jax.ShapeDtypeStruct((nb, t, dr), F32),
                   jax.ShapeDtypeStruct((nb, ng, MXU_DIM, MXU_DIM), F32)],
        scratch_shapes=[pltpu.VMEM((nb, MXU_DIM, MXU_DIM), F32)],
        compiler_params=_params("arbitrary", "arbitrary"),
        name="wkv_scan",
    )(r, dl, k, v, a, b)


def _step_kernel(s_ref, r_ref, dl_ref, k_ref, v_ref, a_ref, b_ref, sout_ref, y_ref):
    S = s_ref[...]
    n = S.shape[-1]
    eye = (lax.broadcasted_iota(jnp.int32, (n, n), 0) == lax.broadcasted_iota(jnp.int32, (n, n), 1)).astype(F32)
    row = lambda ref: ref[...][:, :, None, :]
    sa = jnp.sum(S * row(a_ref), axis=-1, keepdims=True)
    v_col = jnp.sum(eye * row(v_ref), axis=-1, keepdims=True)
    s_new = S * jnp.exp(row(dl_ref)) + sa * row(b_ref) + v_col * row(k_ref)
    sout_ref[...] = s_new
    y_col = jnp.sum(s_new * row(r_ref), axis=-1, keepdims=True)
    y_ref[...] = jnp.sum(eye * y_col, axis=-2)


def _step(S0, r, dl, k, v, a, b):
    nb, nh, n, _ = S0.shape
    bb = _row_tile(nb, 8)
    vec = lambda x: x.reshape(nb, nh, n)
    sspec = pl.BlockSpec((bb, nh, n, n), lambda i: (i, 0, 0, 0))
    vspec = pl.BlockSpec((bb, nh, n), lambda i: (i, 0, 0))
    s_new, y = pl.pallas_call(
        _step_kernel,
        grid=(nb // bb,),
        in_specs=[sspec] + [vspec] * 6,
        out_specs=[sspec, vspec],
        out_shape=[jax.ShapeDtypeStruct(S0.shape, F32), jax.ShapeDtypeStruct((nb, nh, n), F32)],
        compiler_params=_params("arbitrary"),
        name="wkv_step",
    )(S0, *(vec(x) for x in (r, dl, k, v, a, b)))
    return s_new, y.reshape(nb, nh * n)


def _post_kernel(y_ref, bonus_ref, g_ref, gng_ref, gnb_ref, o_ref):
    y = y_ref[...]
    ones_bd = _head_mask(MXU_DIM, BF16)
    inv_n = 1.0 / RWKV_HEAD
    mu = _head_sum(y, ones_bd) * inv_n
    yc = y - mu
    var = _head_sum(yc * yc, ones_bd) * inv_n
    yn = yc * lax.rsqrt(var + GN_EPS) * gng_ref[...] + gnb_ref[...]
    o_ref[...] = (yn + bonus_ref[...]) * g_ref[...]


def _post(y, bonus, g, lw):
    n, dr = y.shape
    tm = _row_tile(n, 512)
    row = pl.BlockSpec((tm, dr), lambda i: (i, 0))
    return pl.pallas_call(
        _post_kernel,
        grid=(n // tm,),
        in_specs=[row, row, row, _const_spec(lw["gn_g"].shape), _const_spec(lw["gn_b"].shape)],
        out_specs=row,
        out_shape=jax.ShapeDtypeStruct((n, dr), F32),
        compiler_params=_params("arbitrary"),
        name="wkv_post",
    )(y, bonus, g, lw["gn_g"], lw["gn_b"])


def _conv_seq_kernel(h_ref, halo_ref, w_ref, b_ref, lng_ref, lnb_ref, o_ref, hp_scr, *, seq_tiles):
    tt = h_ref.shape[0]
    first = (pl.program_id(0) % seq_tiles) == 0
    hp_scr[0:CONV_HALO, :] = jnp.where(first, 0.0, halo_ref[...])
    hp_scr[CONV_HALO:, :] = h_ref[...]
    off = CONV_HALO - (CONV_WIDTH - 1)
    acc = jnp.zeros(o_ref.shape, F32) + b_ref[...]
    for j in range(CONV_WIDTH):
        acc = acc + hp_scr[off + j:off + j + tt, :] * w_ref[j:j + 1, :]
    z = _layer_norm(acc, lng_ref[...], lnb_ref[...])
    o_ref[...] = z * jax.nn.sigmoid(z)


def _conv_seq(h, seq_len, lw):
    n, dc = h.shape
    tt = _row_tile(seq_len, 256)
    assert tt % CONV_HALO == 0
    seq_tiles = seq_len // tt
    row = pl.BlockSpec((tt, dc), lambda i: (i, 0))
    halo = pl.BlockSpec((CONV_HALO, dc), lambda i: (jnp.maximum(i * (tt // CONV_HALO) - 1, 0), 0))
    consts = [lw["conv_w"], lw["conv_b"], lw["conv_ln_g"], lw["conv_ln_b"]]
    return pl.pallas_call(
        functools.partial(_conv_seq_kernel, seq_tiles=seq_tiles),
        grid=(n // tt,),
        in_specs=[row, halo] + [_const_spec(c.shape) for c in consts],
        out_specs=row,
        out_shape=jax.ShapeDtypeStruct((n, dc), F32),
        scratch_shapes=[pltpu.VMEM((CONV_HALO + tt, dc), F32)],
        compiler_params=_params("arbitrary"),
        name="conv_seq",
    )(h, h, *consts)


def _conv_step_kernel(buf_ref, h_ref, w_ref, b_ref, lng_ref, lnb_ref, o_ref, nbuf_ref):
    buf = buf_ref[...]
    h = h_ref[...]
    acc = b_ref[...] + h * w_ref[CONV_WIDTH - 1:CONV_WIDTH, :]
    for j in range(CONV_WIDTH - 1):
        acc = acc + buf[:, j, :] * w_ref[j:j + 1, :]
    z = _layer_norm(acc, lng_ref[...], lnb_ref[...])
    o_ref[...] = z * jax.nn.sigmoid(z)
    nbuf_ref[:, 0:CONV_WIDTH - 2, :] = buf[:, 1:, :]
    nbuf_ref[:, CONV_WIDTH - 2, :] = h


def _conv_step(buf, h, lw):
    nb, wm1, dc = buf.shape
    bb = _row_tile(nb, 32)
    bspec = pl.BlockSpec((bb, wm1, dc), lambda i: (i, 0, 0))
    row = pl.BlockSpec((bb, dc), lambda i: (i, 0))
    consts = [lw["conv_w"], lw["conv_b"], lw["conv_ln_g"], lw["conv_ln_b"]]
    return pl.pallas_call(
        _conv_step_kernel,
        grid=(nb // bb,),
        in_specs=[bspec, row] + [_const_spec(c.shape) for c in consts],
        out_specs=[row, bspec],
        out_shape=[jax.ShapeDtypeStruct((nb, dc), F32), jax.ShapeDtypeStruct(buf.shape, F32)],
        compiler_params=_params("arbitrary"),
        name="conv_step",
    )(buf, h, *consts)


def _outproj_kernel(x_ref, yr_ref, yc_ref, wo_ref, g_ref, b_ref, wr_ref, br_ref, h_ref, logit_ref, *, alpha):
    dr = yr_ref.shape[1]
    mix = _dot(yr_ref[...].astype(BF16), wo_ref[:dr, :]) + _dot(yc_ref[...].astype(BF16), wo_ref[dr:, :])
    h = _layer_norm(alpha * x_ref[...] + mix, g_ref[...], b_ref[...])
    h_ref[...] = h
    logit_ref[...] = jnp.dot(h, wr_ref[...], precision=lax.Precision.HIGHEST,
                             preferred_element_type=F32) + br_ref[...]


def _outproj(x, y_rwkv, y_conv, lw, alpha):
    n, d = x.shape
    dr = y_rwkv.shape[1]
    dc = y_conv.shape[1]
    ne = lw["w_router"].shape[1]
    tm = _row_tile(n, 256)
    row = lambda c: pl.BlockSpec((tm, c), lambda i: (i, 0))
    consts = [lw["w_out"], lw["ln1_g"], lw["ln1_b"], lw["w_router"], lw["b_router"]]
    return pl.pallas_call(
        functools.partial(_outproj_kernel, alpha=alpha),
        grid=(n // tm,),
        in_specs=[row(d), row(dr), row(dc)] + [_const_spec(c.shape) for c in consts],
        out_specs=[row(d), row(ne)],
        out_shape=[jax.ShapeDtypeStruct((n, d), F32), jax.ShapeDtypeStruct((n, ne), F32)],
        compiler_params=_params("arbitrary"),
        name="outproj",
    )(x, y_rwkv, y_conv, *consts)


def _route(logits, moe_tm):
    n_tok, ne = logits.shape
    top_val, top_idx = lax.top_k(logits, TOP_K)
    gates = jax.nn.softmax(top_val, axis=-1)
    n_assign = n_tok * TOP_K
    e_flat = top_idx.reshape(n_assign).astype(jnp.int32)
    tok_flat = jnp.repeat(jnp.arange(n_tok, dtype=jnp.int32), TOP_K)
    onehot = (e_flat[:, None] == jnp.arange(ne, dtype=jnp.int32)[None, :]).astype(jnp.int32)
    counts = onehot.sum(0)
    rank = jnp.take_along_axis(jnp.cumsum(onehot, axis=0), e_flat[:, None], axis=1)[:, 0] - 1
    nblk_e = (counts + moe_tm - 1) // moe_tm
    blk_end = jnp.cumsum(nblk_e)
    blk_start = blk_end - nblk_e
    n_active = blk_end[-1]
    nb_max = n_assign // moe_tm + ne
    pos = blk_start[e_flat] * moe_tm + rank
    row_tok = jnp.zeros((nb_max * moe_tm,), jnp.int32).at[pos].set(tok_flat)
    q = jnp.arange(nb_max, dtype=jnp.int32)
    q_eff = jnp.minimum(q, n_active - 1)
    blk_e = jnp.minimum(jnp.searchsorted(blk_end, q_eff, side="right"), ne - 1).astype(jnp.int32)
    valid = jnp.clip(counts[blk_e] - (q_eff - blk_start[blk_e]) * moe_tm, 0, moe_tm)
    valid = jnp.where(q < n_active, valid, 0).astype(jnp.int32)
    return gates, pos.astype(jnp.int32), row_tok, blk_e, q_eff.astype(jnp.int32), valid


def _gather_kernel(qeff_ref, valid_ref, tok_ref, h_hbm, o_ref, buf, sem):
    q = pl.program_id(0)
    tm = o_ref.shape[0]
    nvalid = valid_ref[q]

    def row_copy(r, tok):
        return pltpu.make_async_copy(h_hbm.at[pl.ds(tok, 1)], buf.at[pl.ds(r, 1)], sem)

    @pl.when(nvalid > 0)
    def _():
        base = q * tm

        def issue(r, c):
            row_copy(r, tok_ref[base + r]).start()
            return c

        lax.fori_loop(0, nvalid, issue, 0)

        def wait(r, c):
            row_copy(r, 0).wait()
            return c

        lax.fori_loop(0, nvalid, wait, 0)
        rows = lax.broadcasted_iota(jnp.int32, buf.shape, 0)
        o_ref[...] = jnp.where(rows < nvalid, buf[...], 0.0).astype(o_ref.dtype)


def _gather_rows(h, row_tok, q_eff, valid, moe_tm):
    n_tok, d = h.shape
    nb_max = q_eff.shape[0]
    grid_spec = pltpu.PrefetchScalarGridSpec(
        num_scalar_prefetch=3,
        grid=(nb_max,),
        in_specs=[pl.BlockSpec(memory_space=pl.ANY)],
        out_specs=pl.BlockSpec((moe_tm, d), lambda q, qe, va, tk: (qe[q], 0)),
        scratch_shapes=[pltpu.VMEM((moe_tm, d), F32), pltpu.SemaphoreType.DMA(())],
    )
    return pl.pallas_call(
        _gather_kernel,
        grid_spec=grid_spec,
        out_shape=jax.ShapeDtypeStruct((nb_max * moe_tm, d), BF16),
        compiler_params=_params("arbitrary"),
        name="moe_gather",
    )(q_eff, valid, row_tok, h)


def _moe_kernel(be_ref, qeff_ref, valid_ref, x_ref, wg_ref, wl_ref, bg_ref, bl_ref, wd_ref, bd_ref, o_ref,
                wg_s, wl_s, wd_s, *, sub):
    q = pl.program_id(0)
    j = pl.program_id(1)
    nvalid = valid_ref[q]
    nsub = (nvalid + sub - 1) // sub
    n_all = o_ref.shape[0] // sub

    @pl.when(nvalid > 0)
    def _():
        wg_s[...] = wg_ref[0].astype(BF16)
        wl_s[...] = wl_ref[0].astype(BF16)
        wd_s[...] = wd_ref[0].astype(BF16)

        def body(s, c):
            r0 = pl.multiple_of(s * sub, sub)
            x = x_ref[pl.ds(r0, sub), :]
            g = jnp.minimum(_dot(x, wg_s[...]) + bg_ref[0], SWIGLU_LIMIT)
            l = jnp.clip(_dot(x, wl_s[...]) + bl_ref[0], -SWIGLU_LIMIT, SWIGLU_LIMIT)
            act = g * jax.nn.sigmoid(SWIGLU_ALPHA * g) * (l + 1.0)
            y = _dot(act.astype(BF16), wd_s[...])

            @pl.when(j == 0)
            def _():
                o_ref[pl.ds(r0, sub), :] = y + bd_ref[0]

            @pl.when(j > 0)
            def _():
                o_ref[pl.ds(r0, sub), :] += y

            return c

        lax.fori_loop(0, nsub, body, 0)

        @pl.when(j == 0)
        def _():
            def zero(s, c):
                r0 = pl.multiple_of(s * sub, sub)
                o_ref[pl.ds(r0, sub), :] = jnp.zeros((sub, o_ref.shape[1]), F32)
                return c

            lax.fori_loop(nsub, n_all, zero, 0)


def _moe_experts(x_sorted, blk_e, q_eff, valid, lw, moe_tm, sub):
    n_rows, d = x_sorted.shape
    ne, _, f2 = lw["w_gu"].shape
    f = f2 // 2
    tf = min(f, MXU_DIM)
    nf = f // tf
    nb_max = q_eff.shape[0]
    b_gu = lw["b_gu"].reshape(ne, 1, f2)
    b_down = lw["b_down"].reshape(ne, 1, d)

    def jf(q, j, va):
        return jnp.where(va[q] > 0, j, nf - 1)

    grid_spec = pltpu.PrefetchScalarGridSpec(
        num_scalar_prefetch=3,
        grid=(nb_max, nf),
        in_specs=[
            pl.BlockSpec((moe_tm, d), lambda q, j, be, qe, va: (qe[q], 0)),
            pl.BlockSpec((1, d, tf), lambda q, j, be, qe, va: (be[q], 0, jf(q, j, va))),
            pl.BlockSpec((1, d, tf), lambda q, j, be, qe, va: (be[q], 0, nf + jf(q, j, va))),
            pl.BlockSpec((1, 1, tf), lambda q, j, be, qe, va: (be[q], 0, jf(q, j, va))),
            pl.BlockSpec((1, 1, tf), lambda q, j, be, qe, va: (be[q], 0, nf + jf(q, j, va))),
            pl.BlockSpec((1, tf, d), lambda q, j, be, qe, va: (be[q], jf(q, j, va), 0)),
            pl.BlockSpec((1, 1, d), lambda q, j, be, qe, va: (be[q], 0, 0)),
        ],
        out_specs=pl.BlockSpec((moe_tm, d), lambda q, j, be, qe, va: (qe[q], 0)),
        scratch_shapes=[pltpu.VMEM((d, tf), BF16), pltpu.VMEM((d, tf), BF16), pltpu.VMEM((tf, d), BF16)],
    )
    return pl.pallas_call(
        functools.partial(_moe_kernel, sub=sub),
        grid_spec=grid_spec,
        out_shape=jax.ShapeDtypeStruct((n_rows, d), F32),
        compiler_params=_params("arbitrary", "arbitrary"),
        name="moe_experts",
    )(blk_e, q_eff, valid, x_sorted, lw["w_gu"], lw["w_gu"], b_gu, b_gu, lw["w_down"], b_down)


def _combine_kernel(pos_ref, h_ref, gates_ref, yrows_hbm, g_ref, b_ref, o_ref, buf, sem, *, alpha):
    i = pl.program_id(0)
    tm = h_ref.shape[0]

    def row_copy(r, k, p):
        return pltpu.make_async_copy(yrows_hbm.at[pl.ds(p, 1)], buf.at[k, pl.ds(r, 1)], sem)

    def issue(r, c):
        for k in range(TOP_K):
            row_copy(r, k, pos_ref[(i * tm + r) * TOP_K + k]).start()
        return c

    lax.fori_loop(0, tm, issue, 0)

    def wait(r, c):
        for k in range(TOP_K):
            row_copy(r, k, 0).wait()
        return c

    lax.fori_loop(0, tm, wait, 0)
    gates = gates_ref[...]
    ffn = buf[0] * gates[:, 0:1]
    for k in range(1, TOP_K):
        ffn = ffn + buf[k] * gates[:, k:k + 1]
    o_ref[...] = _layer_norm(alpha * h_ref[...] + ffn, g_ref[...], b_ref[...])


def _combine(h, gates, pos, y_rows, lw, alpha):
    n_tok, d = h.shape
    tm = _row_tile(n_tok, 128)
    grid_spec = pltpu.PrefetchScalarGridSpec(
        num_scalar_prefetch=1,
        grid=(n_tok // tm,),
        in_specs=[
            pl.BlockSpec((tm, d), lambda i, p: (i, 0)),
            pl.BlockSpec((tm, TOP_K), lambda i, p: (i, 0)),
            pl.BlockSpec(memory_space=pl.ANY),
            pl.BlockSpec((1, d), lambda i, p: (0, 0)),
            pl.BlockSpec((1, d), lambda i, p: (0, 0)),
        ],
        out_specs=pl.BlockSpec((tm, d), lambda i, p: (i, 0)),
        scratch_shapes=[pltpu.VMEM((TOP_K, tm, d), F32), pltpu.SemaphoreType.DMA(())],
    )
    return pl.pallas_call(
        functools.partial(_combine_kernel, alpha=alpha),
        grid_spec=grid_spec,
        out_shape=jax.ShapeDtypeStruct((n_tok, d), F32),
        compiler_params=_params("arbitrary"),
        name="moe_combine",
    )(pos, h, gates, y_rows, lw["ln2_g"], lw["ln2_b"])


def _moe_block_rows(n_assign, ne):
    mean = -(-n_assign // ne)
    tm = -(-(mean * 6 // 5) // MOE_SUB) * MOE_SUB
    return max(MOE_SUB, min(tm, 5 * MOE_SUB))


_VEC_PARAMS = ("mu_rkv", "w0", "a0", "k_k", "k_a", "r_k", "gn_g", "gn_b", "conv_b", "conv_ln_g", "conv_ln_b",
               "ln1_g", "ln1_b", "b_router", "ln2_g", "ln2_b")
_BF16_PARAMS = ("w_in", "w_A", "w_B", "a_A", "a_B", "g_A", "g_B", "w_out")


def _diag_blocks(s):
    nb, ng = s.shape[:2]
    s6 = s.reshape(nb, ng, HEADS_PER_GROUP, RWKV_HEAD, HEADS_PER_GROUP, RWKV_HEAD)
    d = jnp.stack([s6[:, :, h, :, h, :] for h in range(HEADS_PER_GROUP)], axis=2)
    return d.reshape(nb, ng * HEADS_PER_GROUP, RWKV_HEAD, RWKV_HEAD)


def _layer(xp, xs, sx, srkv, swkv, sconv, lw, alpha):
    nbp, t, d = xp.shape
    nbs = xs.shape[0]
    xp2 = xp.reshape(nbp * t, d)
    xs2 = xs.reshape(nbs, d)

    up, hgp, dlp, asp, gp = _inproj(xp2, None, t, lw)
    us, hgs, dls, ass, gs = _inproj(xs2, sx, 1, lw)
    dr = dlp.shape[1]
    rp, kp, vp, ap, bp, bonp = _scanprep(up, None, t, asp, lw)
    rs, ks, vs, as_, bs, bons = _scanprep(us, srkv, 1, ass, lw)

    seq = lambda z: z.reshape(nbp, t, dr)
    yp_raw, s_end = _scan(seq(rp), seq(dlp), seq(kp), seq(vp), seq(ap), seq(bp))
    s_new, ys_raw = _step(swkv, rs, dls, ks, vs, as_, bs)
    yrp = _post(yp_raw.reshape(nbp * t, dr), bonp, gp, lw)
    yrs = _post(ys_raw, bons, gs, lw)

    ycp = _conv_seq(hgp, t, lw)
    ycs, nbuf = _conv_step(sconv, hgs, lw)

    hp, lgp = _outproj(xp2, yrp, ycp, lw, alpha)
    hs, lgs = _outproj(xs2, yrs, ycs, lw, alpha)
    h_all = jnp.concatenate([hp, hs], axis=0)
    logits = jnp.concatenate([lgp, lgs], axis=0)

    ne = logits.shape[1]
    moe_tm = _moe_block_rows(h_all.shape[0] * TOP_K, ne)
    gates, pos, row_tok, blk_e, q_eff, valid = _route(logits, moe_tm)
    x_sorted = _gather_rows(h_all, row_tok, q_eff, valid, moe_tm)
    y_rows = _moe_experts(x_sorted, blk_e, q_eff, valid, lw, moe_tm, MOE_SUB)
    y_all = _combine(h_all, gates, pos, y_rows, lw, alpha)

    yp = y_all[:nbp * t].reshape(nbp, t, d)
    ys = y_all[nbp * t:].reshape(nbs, 1, d)
    p_state = (xp[:, -1], up.reshape(nbp, t, -1)[:, -1], _diag_blocks(s_end),
               hgp.reshape(nbp, t, -1)[:, t - (CONV_WIDTH - 1):])
    s_state = (xs2, us, s_new, nbuf)
    return yp, ys, p_state, s_state


def kernel(x_prompt, x_sample, state_shift_x, state_shift_rkv, state_wkv, state_conv, w_in, mu_x, mu_rkv, w0, w_A,
           w_B, a0, a_A, a_B, g_A, g_B, k_k, k_a, r_k, gn_g, gn_b, conv_w, conv_b, conv_ln_g, conv_ln_b, w_out,
           ln1_g, ln1_b, w_router, b_router, w_gu, b_gu, w_down, b_down, ln2_g, ln2_b):
    params = dict(w_in=w_in, mu_x=mu_x, mu_rkv=mu_rkv, w0=w0, w_A=w_A, w_B=w_B, a0=a0, a_A=a_A, a_B=a_B, g_A=g_A,
                  g_B=g_B, k_k=k_k, k_a=k_a, r_k=r_k, gn_g=gn_g, gn_b=gn_b, conv_w=conv_w, conv_b=conv_b,
                  conv_ln_g=conv_ln_g, conv_ln_b=conv_ln_b, w_out=w_out, ln1_g=ln1_g, ln1_b=ln1_b,
                  w_router=w_router, b_router=b_router, w_gu=w_gu, b_gu=b_gu, w_down=w_down, b_down=b_down,
                  ln2_g=ln2_g, ln2_b=ln2_b)
    depth = w_in.shape[0]
    assert x_sample.shape[1] == 1, "the sample group advances one token per step"
    alpha = (2.0 * depth) ** 0.25
    xp, xs = x_prompt, x_sample
    p_states, s_states = [], []
    for l in range(depth):
        lw = {name: p[l] for name, p in params.items()}
        for name in _VEC_PARAMS:
            lw[name] = lw[name].reshape(1, -1)
        for name in _BF16_PARAMS:
            lw[name] = lw[name].astype(BF16)
        xp, xs, p_st, s_st = _layer(xp, xs, state_shift_x[l], state_shift_rkv[l], state_wkv[l], state_conv[l],
                                    lw, alpha)
        p_states.append(p_st)
        s_states.append(s_st)
    stack = lambda states, i: jnp.stack([st[i] for st in states])
    return (xp, xs,
            stack(p_states, 0), stack(p_states, 1), stack(p_states, 2), stack(p_states, 3),
            stack(s_states, 0), stack(s_states, 1), stack(s_states, 2), stack(s_states, 3))
```

```python
import functools
import math

import jax
import jax.numpy as jnp
from jax import lax
from jax.experimental import pallas as pl
from jax.experimental.pallas import tpu as pltpu

F32 = jnp.float32
BF16 = jnp.bfloat16

RWKV_HEAD = 64
CONV_WIDTH = 31
TOP_K = 4
SWIGLU_LIMIT = 7.0
SWIGLU_ALPHA = 1.702
LN_EPS = 1e-5
GN_EPS = 64e-5

LANES = 128
SUBLANES = 8
MXU_DIM = 256
VMEM_LIMIT_BYTES = 56 * 1024 * 1024

SCAN_CHUNK = 64
HEADS_PER_GROUP = MXU_DIM // RWKV_HEAD
CONV_HALO = 32
MOE_SUB = 256


def _row_tile(n, target):
    best = None
    for t in range(SUBLANES, min(n, target) + 1, SUBLANES):
        if n % t == 0:
            best = t
    assert best is not None, (n, target)
    return best


def _params(*sem):
    return pltpu.CompilerParams(dimension_semantics=sem, vmem_limit_bytes=VMEM_LIMIT_BYTES)


def _const_spec(shape):
    nd = len(shape)
    return pl.BlockSpec(shape, lambda *_: (0,) * nd, pipeline_mode=pl.Buffered(1))


def _dot(a, b):
    return jnp.dot(a, b, preferred_element_type=F32)


def _split2(x):
    hi = x.astype(BF16)
    lo = (x - hi.astype(F32)).astype(BF16)
    return hi, lo


def _layer_norm(z, g, b):
    mu = jnp.mean(z, axis=-1, keepdims=True)
    zc = z - mu
    var = jnp.mean(zc * zc, axis=-1, keepdims=True)
    return zc * lax.rsqrt(var + LN_EPS) * g + b


def _shift_rows(x, first_row):
    xs = pltpu.roll(x, 1, axis=0)
    row = lax.broadcasted_iota(jnp.int32, x.shape, 0)
    return jnp.where(row == 0, first_row, xs)


def _inproj_kernel(x_ref, xprev_ref, w_in_ref, mu_ref, wA_ref, aA_ref, gA_ref, wB_ref, aB_ref, gB_ref,
                   w0_ref, a0_ref, urkv_ref, hglu_ref, dlog_ref, asig_ref, g_ref, *, seq_tiles, halo):
    x = x_ref[...]
    if halo:
        first = (pl.program_id(0) % seq_tiles) == 0
        prev_row = jnp.where(first, 0.0, xprev_ref[SUBLANES - 1:SUBLANES, :])
        xprev = _shift_rows(x, prev_row)
    else:
        xprev = xprev_ref[...]
    xx = xprev - x
    xb = x.astype(BF16)
    d3 = urkv_ref.shape[1]
    dc = hglu_ref.shape[1]
    urkv_ref[...] = _dot(xb, w_in_ref[:, :d3])
    val = _dot(xb, w_in_ref[:, d3:d3 + dc])
    gate = _dot(xb, w_in_ref[:, d3 + dc:])
    hglu_ref[...] = val * jax.nn.sigmoid(gate)

    xw = (x + xx * mu_ref[0:1, :]).astype(BF16)
    tw = jnp.tanh(_dot(xw, wA_ref[...]))
    wlin = w0_ref[...] + _dot(tw.astype(BF16), wB_ref[...])
    wlog = -jax.nn.softplus(-wlin) - 0.5
    dlog_ref[...] = -jnp.exp(wlog)

    xa = (x + xx * mu_ref[1:2, :]).astype(BF16)
    ta = _dot(xa, aA_ref[...])
    asig_ref[...] = jax.nn.sigmoid(a0_ref[...] + _dot(ta.astype(BF16), aB_ref[...]))

    xg = (x + xx * mu_ref[2:3, :]).astype(BF16)
    tg = jax.nn.sigmoid(_dot(xg, gA_ref[...]))
    g_ref[...] = _dot(tg.astype(BF16), gB_ref[...])


def _inproj(x, xprev, seq_len, lw):
    n, d = x.shape
    d_in = lw["w_in"].shape[1]
    dr = lw["w_B"].shape[1]
    d3 = 3 * dr
    dc = (d_in - d3) // 2
    halo = xprev is None
    tm = _row_tile(seq_len if halo else n, 256)
    seq_tiles = (seq_len // tm) if halo else 1
    if halo:
        prev_spec = pl.BlockSpec((SUBLANES, d), lambda i: (jnp.maximum(i * (tm // SUBLANES) - 1, 0), 0))
        prev_arg = x
    else:
        prev_spec = pl.BlockSpec((tm, d), lambda i: (i, 0))
        prev_arg = xprev
    row = lambda c: pl.BlockSpec((tm, c), lambda i: (i, 0))
    outs = [jax.ShapeDtypeStruct((n, c), F32) for c in (d3, dc, dr, dr, dr)]
    consts = [lw["w_in"], lw["mu_x"], lw["w_A"], lw["a_A"], lw["g_A"], lw["w_B"], lw["a_B"], lw["g_B"],
              lw["w0"], lw["a0"]]
    return pl.pallas_call(
        functools.partial(_inproj_kernel, seq_tiles=seq_tiles, halo=halo),
        grid=(n // tm,),
        in_specs=[row(d), prev_spec] + [_const_spec(c.shape) for c in consts],
        out_specs=[row(c) for c in (d3, dc, dr, dr, dr)],
        out_shape=outs,
        compiler_params=_params("arbitrary"),
        name="inproj",
    )(x, prev_arg, *consts)


def _head_mask(n, dtype):
    r = lax.broadcasted_iota(jnp.int32, (n, n), 0) // RWKV_HEAD
    c = lax.broadcasted_iota(jnp.int32, (n, n), 1) // RWKV_HEAD
    return (r == c).astype(dtype)


def _head_sum(x, ones_bd):
    cols = []
    for c0 in range(0, x.shape[1], MXU_DIM):
        hi, lo = _split2(x[:, c0:c0 + MXU_DIM])
        cols.append(_dot(hi, ones_bd) + _dot(lo, ones_bd))
    return cols[0] if len(cols) == 1 else jnp.concatenate(cols, axis=1)


def _scanprep_kernel(u_ref, uprev_ref, mu_ref, kk_ref, ka_ref, rk_ref, asig_ref,
                     r_ref, k_ref, v_ref, a_ref, b_ref, bonus_ref, *, seq_tiles, halo):
    u = u_ref[...]
    if halo:
        first = (pl.program_id(0) % seq_tiles) == 0
        prev_row = jnp.where(first, 0.0, uprev_ref[SUBLANES - 1:SUBLANES, :])
        uprev = _shift_rows(u, prev_row)
    else:
        uprev = uprev_ref[...]
    rkv = u + (uprev - u) * mu_ref[...]
    dr = r_ref.shape[1]
    r = rkv[:, :dr]
    k = rkv[:, dr:2 * dr]
    v = rkv[:, 2 * dr:]
    asig = asig_ref[...]
    ones_bd = _head_mask(MXU_DIM, BF16)
    kk = k * kk_ref[...]
    kk = kk * lax.rsqrt(jnp.maximum(_head_sum(kk * kk, ones_bd), 1e-24))
    k = k * (1.0 + (asig - 1.0) * ka_ref[...])
    r_ref[...] = r
    k_ref[...] = k
    v_ref[...] = v
    a_ref[...] = -kk
    b_ref[...] = kk * asig
    bonus_ref[...] = _head_sum(r * k * rk_ref[...], ones_bd) * v


def _scanprep(u_rkv, uprev, seq_len, asig, lw):
    n, d3 = u_rkv.shape
    dr = d3 // 3
    halo = uprev is None
    tm = _row_tile(seq_len if halo else n, 256)
    seq_tiles = (seq_len // tm) if halo else 1
    if halo:
        prev_spec = pl.BlockSpec((SUBLANES, d3), lambda i: (jnp.maximum(i * (tm // SUBLANES) - 1, 0), 0))
        prev_arg = u_rkv
    else:
        prev_spec = pl.BlockSpec((tm, d3), lambda i: (i, 0))
        prev_arg = uprev
    row = lambda c: pl.BlockSpec((tm, c), lambda i: (i, 0))
    consts = [lw["mu_rkv"], lw["k_k"], lw["k_a"], lw["r_k"]]
    return pl.pallas_call(
        functools.partial(_scanprep_kernel, seq_tiles=seq_tiles, halo=halo),
        grid=(n // tm,),
        in_specs=[row(d3), prev_spec] + [_const_spec(c.shape) for c in consts] + [row(dr)],
        out_specs=[row(dr)] * 6,
        out_shape=[jax.ShapeDtypeStruct((n, dr), F32)] * 6,
        compiler_params=_params("arbitrary"),
        name="scanprep",
    )(u_rkv, prev_arg, *consts, asig)


def _blockdiag(x, mask):
    reps = mask.shape[0] // x.shape[0]
    return jnp.concatenate([x] * reps, axis=0) * mask


def _dot_nt(a, b):
    return lax.dot_general(a, b, (((1,), (1,)), ((), ())), preferred_element_type=F32)


def _dot_tn(a, b):
    return lax.dot_general(a, b, (((0,), (0,)), ((), ())), preferred_element_type=F32)


def _scan_kernel(r_ref, dl_ref, k_ref, v_ref, a_ref, b_ref, y_ref, s_ref, h_scr):
    c = pl.program_id(0)
    nb = r_ref.shape[0]
    C = r_ref.shape[1]
    W = MXU_DIM
    ng = r_ref.shape[2] // W

    @pl.when(c == 0)
    def _():
        h_scr[...] = jnp.zeros_like(h_scr)

    ti = lax.broadcasted_iota(jnp.int32, (C, C), 0)
    tj = lax.broadcasted_iota(jnp.int32, (C, C), 1)
    tri = (ti >= tj).astype(BF16)
    t_row = lax.broadcasted_iota(jnp.int32, (C, W), 0)
    j_col = lax.broadcasted_iota(jnp.int32, (C, W), 1) % C
    strict = j_col < t_row
    incl = j_col <= t_row
    eye_cat = (j_col == t_row).astype(F32)
    rb = lax.broadcasted_iota(jnp.int32, (HEADS_PER_GROUP * C, W), 0) // C
    cb = lax.broadcasted_iota(jnp.int32, (HEADS_PER_GROUP * C, W), 1) // RWKV_HEAD
    bmask = (rb == cb).astype(BF16)
    hmask = _head_mask(W, F32)

    chains = [(bi, gi) for bi in range(nb) for gi in range(ng)]
    each = lambda f, *lists: [f(*xs) for xs in zip(*lists)]
    load = lambda ref: [ref[bi, :, gi * W:(gi + 1) * W] for bi, gi in chains]
    r, dl, k, v, a, b = (load(ref) for ref in (r_ref, dl_ref, k_ref, v_ref, a_ref, b_ref))
    sbd = [h_scr[bi, gi] for bi, gi in chains]
    bd = lambda x: _blockdiag(x, bmask)

    def cumsum(d):
        d_hi = d.astype(BF16)
        d_r1 = d - d_hi.astype(F32)
        d_mid = d_r1.astype(BF16)
        d_lo = (d_r1 - d_mid.astype(F32)).astype(BF16)
        return _dot(tri, d_hi) + (_dot(tri, d_mid) + _dot(tri, d_lo))

    cum = each(cumsum, dl)
    cum_last = each(lambda x: x[C - 1:C, :], cum)
    e_neg = each(lambda x: jnp.exp(-x), cum)
    e_end = each(lambda x, xl: jnp.exp(xl - x), cum, cum_last)
    at = each(lambda x, cu, d: (x * jnp.exp(cu - d)).astype(BF16), a, cum, dl)
    rt = each(lambda x, cu: (x * jnp.exp(cu)).astype(BF16), r, cum)
    bt = each(lambda x, e: (x * e).astype(BF16), b, e_neg)
    kt = each(lambda x, e: (x * e).astype(BF16), k, e_neg)
    bh = each(lambda x, e: (x * e).astype(BF16), b, e_end)
    kh = each(lambda x, e: (x * e).astype(BF16), k, e_end)
    vb = each(lambda x: x.astype(BF16), v)
    ar = each(lambda x, y: jnp.concatenate([x, y], axis=0), at, rt)

    p_b = each(lambda x, y: _dot_nt(x, bd(y)), ar, bt)
    p_k = each(lambda x, y: _dot_nt(x, bd(y)), ar, kt)
    arh = each(lambda x, s: _dot_nt(x, s.astype(BF16)), ar, sbd)
    p_ab = each(lambda p: jnp.where(strict, p[:C], 0.0), p_b)
    p_rb = each(lambda p: jnp.where(incl, p[C:], 0.0).astype(BF16), p_b)
    p_akrk = each(lambda p: jnp.concatenate([jnp.where(strict, p[:C], 0.0), jnp.where(incl, p[C:], 0.0)],
                                            axis=0).astype(BF16), p_k)

    pv = each(lambda p, x: _dot(p, bd(x)), p_akrk, vb)
    n_sq = int(math.log2(C))
    nn_b = each(lambda p: p.astype(BF16), p_ab)
    nn_b = each(lambda n: _dot(n, bd(n)).astype(BF16), nn_b)
    w = each(lambda x, y: (x[:C] + y[:C]).astype(BF16), arh, pv)
    tm = each(lambda p: eye_cat + p, p_ab)
    for i in range(1, n_sq):
        last = i == n_sq - 1
        lhs = each(lambda t, n: t.astype(BF16) if last else jnp.concatenate([t.astype(BF16), n], axis=0), tm, nn_b)
        prod = each(lambda l, n: _dot(l, bd(n)), lhs, nn_b)
        tm = each(lambda t, p: t + p[:C], tm, prod)
        if not last:
            nn_b = each(lambda p: p[C:].astype(BF16), prod)
    ub = each(lambda t, x: _dot(t.astype(BF16), bd(x)).astype(BF16), tm, w)

    yv = each(lambda x, y, p, u: x[C:] + y[C:] + _dot(p, bd(u)), arh, pv, p_rb, ub)
    upd = each(lambda u, x, y, z: _dot_tn(jnp.concatenate([u, x], axis=0), jnp.concatenate([y, z], axis=0)),
               ub, vb, bh, kh)
    for (bi, gi), y, s, xl, up in zip(chains, yv, sbd, cum_last, upd):
        y_ref[bi, :, gi * W:(gi + 1) * W] = y
        h_scr[bi, gi] = (s * jnp.exp(xl) + up) * hmask

    @pl.when(c == pl.num_programs(0) - 1)
    def _():
        s_ref[...] = h_scr[...]


def _scan(r, dl, k, v, a, b):
    nb, t, dr = r.shape
    ng = dr // MXU_DIM
    C = SCAN_CHUNK
    spec = pl.BlockSpec((nb, C, dr), lambda c: (0, c, 0))
    return pl.pallas_call(
        _scan_kernel,
        grid=(t // C,),
        in_specs=[spec] * 6,
        out_specs=[spec, pl.BlockSpec((nb, ng, MXU_DIM, MXU_DIM), lambda c: (0, 0, 0, 0))],
        out_shape=[jax.ShapeDtypeStruct((nb, t, dr), F32),
                   jax.ShapeDtypeStruct((nb, ng, MXU_DIM, MXU_DIM), F32)],
        scratch_shapes=[pltpu.VMEM((nb, ng, MXU_DIM, MXU_DIM), F32)],
        compiler_params=_params("arbitrary"),
        name="wkv_scan",
    )(r, dl, k, v, a, b)


def _step_kernel(s_ref, r_ref, dl_ref, k_ref, v_ref, a_ref, b_ref, sout_ref, y_ref):
    S = s_ref[...]
    n = S.shape[-1]
    eye = (lax.broadcasted_iota(jnp.int32, (n, n), 0) == lax.broadcasted_iota(jnp.int32, (n, n), 1)).astype(F32)
    row = lambda ref: ref[...][:, :, None, :]
    sa = jnp.sum(S * row(a_ref), axis=-1, keepdims=True)
    v_col = jnp.sum(eye * row(v_ref), axis=-1, keepdims=True)
    s_new = S * jnp.exp(row(dl_ref)) + sa * row(b_ref) + v_col * row(k_ref)
    sout_ref[...] = s_new
    y_col = jnp.sum(s_new * row(r_ref), axis=-1, keepdims=True)
    y_ref[...] = jnp.sum(eye * y_col, axis=-2)


def _step(S0, r, dl, k, v, a, b):
    nb, nh, n, _ = S0.shape
    bb = _row_tile(nb, 8)
    vec = lambda x: x.reshape(nb, nh, n)
    sspec = pl.BlockSpec((bb, nh, n, n), lambda i: (i, 0, 0, 0))
    vspec = pl.BlockSpec((bb, nh, n), lambda i: (i, 0, 0))
    s_new, y = pl.pallas_call(
        _step_kernel,
        grid=(nb // bb,),
        in_specs=[sspec] + [vspec] * 6,
        out_specs=[sspec, vspec],
        out_shape=[jax.ShapeDtypeStruct(S0.shape, F32), jax.ShapeDtypeStruct((nb, nh, n), F32)],
        compiler_params=_params("arbitrary"),
        name="wkv_step",
    )(S0, *(vec(x) for x in (r, dl, k, v, a, b)))
    return s_new, y.reshape(nb, nh * n)


def _post_kernel(y_ref, bonus_ref, g_ref, gng_ref, gnb_ref, o_ref):
    y = y_ref[...]
    ones_bd = _head_mask(MXU_DIM, BF16)
    inv_n = 1.0 / RWKV_HEAD
    mu = _head_sum(y, ones_bd) * inv_n
    yc = y - mu
    var = _head_sum(yc * yc, ones_bd) * inv_n
    yn = yc * lax.rsqrt(var + GN_EPS) * gng_ref[...] + gnb_ref[...]
    o_ref[...] = (yn + bonus_ref[...]) * g_ref[...]


def _post(y, bonus, g, lw):
    n, dr = y.shape
    tm = _row_tile(n, 512)
    row = pl.BlockSpec((tm, dr), lambda i: (i, 0))
    return pl.pallas_call(
        _post_kernel,
        grid=(n // tm,),
        in_specs=[row, row, row, _const_spec(lw["gn_g"].shape), _const_spec(lw["gn_b"].shape)],
        out_specs=row,
        out_shape=jax.ShapeDtypeStruct((n, dr), F32),
        compiler_params=_params("arbitrary"),
        name="wkv_post",
    )(y, bonus, g, lw["gn_g"], lw["gn_b"])


def _conv_seq_kernel(h_ref, halo_ref, w_ref, b_ref, lng_ref, lnb_ref, o_ref, hp_scr, *, seq_tiles):
    tt = h_ref.shape[0]
    first = (pl.program_id(0) % seq_tiles) == 0
    hp_scr[0:CONV_HALO, :] = jnp.where(first, 0.0, halo_ref[...])
    hp_scr[CONV_HALO:, :] = h_ref[...]
    off = CONV_HALO - (CONV_WIDTH - 1)
    acc = jnp.zeros(o_ref.shape, F32) + b_ref[...]
    for j in range(CONV_WIDTH):
        acc = acc + hp_scr[off + j:off + j + tt, :] * w_ref[j:j + 1, :]
    z = _layer_norm(acc, lng_ref[...], lnb_ref[...])
    o_ref[...] = z * jax.nn.sigmoid(z)


def _conv_seq(h, seq_len, lw):
    n, dc = h.shape
    tt = _row_tile(seq_len, 256)
    assert tt % CONV_HALO == 0
    seq_tiles = seq_len // tt
    row = pl.BlockSpec((tt, dc), lambda i: (i, 0))
    halo = pl.BlockSpec((CONV_HALO, dc), lambda i: (jnp.maximum(i * (tt // CONV_HALO) - 1, 0), 0))
    consts = [lw["conv_w"], lw["conv_b"], lw["conv_ln_g"], lw["conv_ln_b"]]
    return pl.pallas_call(
        functools.partial(_conv_seq_kernel, seq_tiles=seq_tiles),
        grid=(n // tt,),
        in_specs=[row, halo] + [_const_spec(c.shape) for c in consts],
        out_specs=row,
        out_shape=jax.ShapeDtypeStruct((n, dc), F32),
        scratch_shapes=[pltpu.VMEM((CONV_HALO + tt, dc), F32)],
        compiler_params=_params("arbitrary"),
        name="conv_seq",
    )(h, h, *consts)


def _conv_step_kernel(buf_ref, h_ref, w_ref, b_ref, lng_ref, lnb_ref, o_ref, nbuf_ref):
    buf = buf_ref[...]
    h = h_ref[...]
    acc = b_ref[...] + h * w_ref[CONV_WIDTH - 1:CONV_WIDTH, :]
    for j in range(CONV_WIDTH - 1):
        acc = acc + buf[:, j, :] * w_ref[j:j + 1, :]
    z = _layer_norm(acc, lng_ref[...], lnb_ref[...])
    o_ref[...] = z * jax.nn.sigmoid(z)
    nbuf_ref[:, 0:CONV_WIDTH - 2, :] = buf[:, 1:, :]
    nbuf_ref[:, CONV_WIDTH - 2, :] = h


def _conv_step(buf, h, lw):
    nb, wm1, dc = buf.shape
    bb = _row_tile(nb, 32)
    bspec = pl.BlockSpec((bb, wm1, dc), lambda i: (i, 0, 0))
    row = pl.BlockSpec((bb, dc), lambda i: (i, 0))
    consts = [lw["conv_w"], lw["conv_b"], lw["conv_ln_g"], lw["conv_ln_b"]]
    return pl.pallas_call(
        _conv_step_kernel,
        grid=(nb // bb,),
        in_specs=[bspec, row] + [_const_spec(c.shape) for c in consts],
        out_specs=[row, bspec],
        out_shape=[jax.ShapeDtypeStruct((nb, dc), F32), jax.ShapeDtypeStruct(buf.shape, F32)],
        compiler_params=_params("arbitrary"),
        name="conv_step",
    )(buf, h, *consts)


def _outproj_kernel(x_ref, yr_ref, yc_ref, wo_ref, g_ref, b_ref, wr_ref, wr_hi_ref, br_ref, h_ref, logit_ref, *,
                    alpha):
    ymix = jnp.concatenate([yr_ref[...].astype(BF16), yc_ref[...].astype(BF16)], axis=1)
    h = _layer_norm(alpha * x_ref[...] + _dot(ymix, wo_ref[...]), g_ref[...], b_ref[...])
    h_ref[...] = h
    ne = logit_ref.shape[1]
    h_hi, h_lo = _split2(h)
    s = _dot(h_hi, wr_ref[...]) + _dot(h_lo, wr_hi_ref[...])
    logit_ref[...] = s[:, :ne] + s[:, ne:] + br_ref[...]


def _outproj(x, y_rwkv, y_conv, lw, alpha):
    n, d = x.shape
    dr = y_rwkv.shape[1]
    dc = y_conv.shape[1]
    ne = lw["w_router"].shape[1]
    tm = _row_tile(n, 256)
    row = lambda c: pl.BlockSpec((tm, c), lambda i: (i, 0))
    w_hi, w_lo = _split2(lw["w_router"])
    consts = [lw["w_out"], lw["ln1_g"], lw["ln1_b"], jnp.concatenate([w_hi, w_lo], axis=1),
              jnp.concatenate([w_hi, jnp.zeros_like(w_hi)], axis=1), lw["b_router"]]
    return pl.pallas_call(
        functools.partial(_outproj_kernel, alpha=alpha),
        grid=(n // tm,),
        in_specs=[row(d), row(dr), row(dc)] + [_const_spec(c.shape) for c in consts],
        out_specs=[row(d), row(ne)],
        out_shape=[jax.ShapeDtypeStruct((n, d), F32), jax.ShapeDtypeStruct((n, ne), F32)],
        compiler_params=_params("arbitrary"),
        name="outproj",
    )(x, y_rwkv, y_conv, *consts)


def _route(logits, moe_tm):
    n_tok, ne = logits.shape
    top_val, top_idx = lax.top_k(logits, TOP_K)
    gates = jax.nn.softmax(top_val, axis=-1)
    n_assign = n_tok * TOP_K
    e_flat = top_idx.reshape(n_assign).astype(jnp.int32)
    tok_flat = jnp.repeat(jnp.arange(n_tok, dtype=jnp.int32), TOP_K)
    onehot = (e_flat[:, None] == jnp.arange(ne, dtype=jnp.int32)[None, :]).astype(jnp.int32)
    counts = onehot.sum(0)
    rank = jnp.take_along_axis(jnp.cumsum(onehot, axis=0), e_flat[:, None], axis=1)[:, 0] - 1
    nblk_e = (counts + moe_tm - 1) // moe_tm
    blk_end = jnp.cumsum(nblk_e)
    blk_start = blk_end - nblk_e
    n_active = blk_end[-1]
    nb_max = n_assign // moe_tm + ne
    pos = blk_start[e_flat] * moe_tm + rank
    row_tok = jnp.zeros((nb_max * moe_tm,), jnp.int32).at[pos].set(tok_flat)
    q = jnp.arange(nb_max, dtype=jnp.int32)
    q_eff = jnp.minimum(q, n_active - 1)
    blk_e = jnp.minimum(jnp.searchsorted(blk_end, q_eff, side="right"), ne - 1).astype(jnp.int32)
    valid = jnp.clip(counts[blk_e] - (q_eff - blk_start[blk_e]) * moe_tm, 0, moe_tm)
    valid = jnp.where(q < n_active, valid, 0).astype(jnp.int32)
    return gates, pos.astype(jnp.int32), row_tok, blk_e, q_eff.astype(jnp.int32), valid


def _gather_kernel(qeff_ref, valid_ref, tok_ref, h_hbm, o_ref, buf, sem):
    q = pl.program_id(0)
    tm = o_ref.shape[0]
    nvalid = valid_ref[q]

    def row_copy(r, tok):
        return pltpu.make_async_copy(h_hbm.at[pl.ds(tok, 1)], buf.at[pl.ds(r, 1)], sem)

    @pl.when(nvalid > 0)
    def _():
        base = q * tm
        ngroups = (nvalid + SUBLANES - 1) // SUBLANES

        def issue(g, c):
            r0 = pl.multiple_of(g * SUBLANES, SUBLANES)
            for i in range(SUBLANES):
                row_copy(r0 + i, tok_ref[base + r0 + i]).start()
            return c

        lax.fori_loop(0, ngroups, issue, 0)

        def wait(g, c):
            r0 = pl.multiple_of(g * SUBLANES, SUBLANES)
            pltpu.make_async_copy(h_hbm.at[pl.ds(0, SUBLANES)], buf.at[pl.ds(r0, SUBLANES)], sem).wait()
            return c

        lax.fori_loop(0, ngroups, wait, 0)
        rows = lax.broadcasted_iota(jnp.int32, buf.shape, 0)
        o_ref[...] = jnp.where(rows < nvalid, buf[...], 0.0).astype(o_ref.dtype)


def _gather_rows(h, row_tok, q_eff, valid, moe_tm):
    n_tok, d = h.shape
    nb_max = q_eff.shape[0]
    grid_spec = pltpu.PrefetchScalarGridSpec(
        num_scalar_prefetch=3,
        grid=(nb_max,),
        in_specs=[pl.BlockSpec(memory_space=pl.ANY)],
        out_specs=pl.BlockSpec((moe_tm, d), lambda q, qe, va, tk: (qe[q], 0)),
        scratch_shapes=[pltpu.VMEM((moe_tm, d), F32), pltpu.SemaphoreType.DMA(())],
    )
    return pl.pallas_call(
        _gather_kernel,
        grid_spec=grid_spec,
        out_shape=jax.ShapeDtypeStruct((nb_max * moe_tm, d), BF16),
        compiler_params=_params("arbitrary"),
        name="moe_gather",
    )(q_eff, valid, row_tok, h)


def _moe_kernel(be_ref, qeff_ref, valid_ref, x_ref, wg_ref, wl_ref, bg_ref, bl_ref, wd_ref, bd_ref, o_ref,
                wg_s, wl_s, wd_s, *, sub):
    q = pl.program_id(0)
    j = pl.program_id(1)
    nvalid = valid_ref[q]
    nsub = (nvalid + sub - 1) // sub
    n_all = o_ref.shape[0] // sub

    def fill(s, value):
        r0 = pl.multiple_of(s * sub, sub)
        o_ref[pl.ds(r0, sub), :] = jnp.broadcast_to(value, (sub, o_ref.shape[1]))

    def sub_block(s):
        r0 = pl.multiple_of(s * sub, sub)
        x = x_ref[pl.ds(r0, sub), :]
        g = jnp.minimum(_dot(x, wg_s[...]) + bg_ref[0], SWIGLU_LIMIT)
        l = jnp.clip(_dot(x, wl_s[...]) + bl_ref[0], -SWIGLU_LIMIT, SWIGLU_LIMIT)
        act = g * jax.nn.sigmoid(SWIGLU_ALPHA * g) * (l + 1.0)
        o_ref[pl.ds(r0, sub), :] += _dot(act.astype(BF16), wd_s[...])

    @pl.when(nvalid > 0)
    def _():
        @pl.when(j == 0)
        def _():
            lax.fori_loop(0, nsub, lambda s, c: (fill(s, bd_ref[0]), c)[1], 0)
            lax.fori_loop(nsub, n_all, lambda s, c: (fill(s, jnp.zeros((1, 1), F32)), c)[1], 0)

        wg_s[...] = wg_ref[0].astype(BF16)
        wl_s[...] = wl_ref[0].astype(BF16)
        wd_s[...] = wd_ref[0].astype(BF16)
        sub_block(0)
        lax.fori_loop(1, nsub, lambda s, c: (sub_block(s), c)[1], 0)


def _moe_experts(x_sorted, blk_e, q_eff, valid, lw, moe_tm, sub):
    n_rows, d = x_sorted.shape
    ne, _, f2 = lw["w_gu"].shape
    f = f2 // 2
    tf = min(f, MXU_DIM)
    nf = f // tf
    nb_max = q_eff.shape[0]
    b_gu = lw["b_gu"].reshape(ne, 1, f2)
    b_down = lw["b_down"].reshape(ne, 1, d)

    def jf(q, j, va):
        return jnp.where(va[q] > 0, j, nf - 1)

    grid_spec = pltpu.PrefetchScalarGridSpec(
        num_scalar_prefetch=3,
        grid=(nb_max, nf),
        in_specs=[
            pl.BlockSpec((moe_tm, d), lambda q, j, be, qe, va: (qe[q], 0)),
            pl.BlockSpec((1, d, tf), lambda q, j, be, qe, va: (be[q], 0, jf(q, j, va))),
            pl.BlockSpec((1, d, tf), lambda q, j, be, qe, va: (be[q], 0, nf + jf(q, j, va))),
            pl.BlockSpec((1, 1, tf), lambda q, j, be, qe, va: (be[q], 0, jf(q, j, va))),
            pl.BlockSpec((1, 1, tf), lambda q, j, be, qe, va: (be[q], 0, nf + jf(q, j, va))),
            pl.BlockSpec((1, tf, d), lambda q, j, be, qe, va: (be[q], jf(q, j, va), 0)),
            pl.BlockSpec((1, 1, d), lambda q, j, be, qe, va: (be[q], 0, 0)),
        ],
        out_specs=pl.BlockSpec((moe_tm, d), lambda q, j, be, qe, va: (qe[q], 0)),
        scratch_shapes=[pltpu.VMEM((d, tf), BF16), pltpu.VMEM((d, tf), BF16), pltpu.VMEM((tf, d), BF16)],
    )
    return pl.pallas_call(
        functools.partial(_moe_kernel, sub=sub),
        grid_spec=grid_spec,
        out_shape=jax.ShapeDtypeStruct((n_rows, d), F32),
        compiler_params=_params("arbitrary", "arbitrary"),
        name="moe_experts",
    )(blk_e, q_eff, valid, x_sorted, lw["w_gu"], lw["w_gu"], b_gu, b_gu, lw["w_down"], b_down)


def _combine_kernel(pos_ref, h_ref, gates_ref, yrows_hbm, g_ref, b_ref, o_ref, buf, sem, *, alpha):
    i = pl.program_id(0)
    tm = h_ref.shape[0]

    def row_copy(r, k, p):
        return pltpu.make_async_copy(yrows_hbm.at[pl.ds(p, 1)], buf.at[k, pl.ds(r, 1)], sem)

    def issue(r2, c):
        for dr in range(2):
            r = r2 * 2 + dr
            for k in range(TOP_K):
                row_copy(r, k, pos_ref[(i * tm + r) * TOP_K + k]).start()
        return c

    lax.fori_loop(0, tm // 2, issue, 0)
    for k in range(TOP_K):
        pltpu.make_async_copy(yrows_hbm.at[pl.ds(0, tm)], buf.at[k], sem).wait()
    gates = gates_ref[...]
    ffn = buf[0] * gates[:, 0:1]
    for k in range(1, TOP_K):
        ffn = ffn + buf[k] * gates[:, k:k + 1]
    o_ref[...] = _layer_norm(alpha * h_ref[...] + ffn, g_ref[...], b_ref[...])


def _combine(h, gates, pos, y_rows, lw, alpha):
    n_tok, d = h.shape
    tm = _row_tile(n_tok, 128)
    grid_spec = pltpu.PrefetchScalarGridSpec(
        num_scalar_prefetch=1,
        grid=(n_tok // tm,),
        in_specs=[
            pl.BlockSpec((tm, d), lambda i, p: (i, 0)),
            pl.BlockSpec((tm, TOP_K), lambda i, p: (i, 0)),
            pl.BlockSpec(memory_space=pl.ANY),
            pl.BlockSpec((1, d), lambda i, p: (0, 0)),
            pl.BlockSpec((1, d), lambda i, p: (0, 0)),
        ],
        out_specs=pl.BlockSpec((tm, d), lambda i, p: (i, 0)),
        scratch_shapes=[pltpu.VMEM((TOP_K, tm, d), F32), pltpu.SemaphoreType.DMA(())],
    )
    return pl.pallas_call(
        functools.partial(_combine_kernel, alpha=alpha),
        grid_spec=grid_spec,
        out_shape=jax.ShapeDtypeStruct((n_tok, d), F32),
        compiler_params=_params("arbitrary"),
        name="moe_combine",
    )(pos, h, gates, y_rows, lw["ln2_g"], lw["ln2_b"])


def _moe_block_rows(n_assign, ne):
    mean = -(-n_assign // ne)
    tm = -(-(mean * 6 // 5) // MOE_SUB) * MOE_SUB
    return max(MOE_SUB, min(tm, 5 * MOE_SUB))


_VEC_PARAMS = ("mu_rkv", "w0", "a0", "k_k", "k_a", "r_k", "gn_g", "gn_b", "conv_b", "conv_ln_g", "conv_ln_b",
               "ln1_g", "ln1_b", "b_router", "ln2_g", "ln2_b")
_BF16_PARAMS = ("w_in", "w_A", "w_B", "a_A", "a_B", "g_A", "g_B", "w_out")


def _diag_blocks(s):
    nb, ng = s.shape[:2]
    s6 = s.reshape(nb, ng, HEADS_PER_GROUP, RWKV_HEAD, HEADS_PER_GROUP, RWKV_HEAD)
    d = jnp.stack([s6[:, :, h, :, h, :] for h in range(HEADS_PER_GROUP)], axis=2)
    return d.reshape(nb, ng * HEADS_PER_GROUP, RWKV_HEAD, RWKV_HEAD)


def _layer(xp, xs, sx, srkv, swkv, sconv, lw, alpha):
    nbp, t, d = xp.shape
    nbs = xs.shape[0]
    xp2 = xp.reshape(nbp * t, d)
    xs2 = xs.reshape(nbs, d)

    up, hgp, dlp, asp, gp = _inproj(xp2, None, t, lw)
    us, hgs, dls, ass, gs = _inproj(xs2, sx, 1, lw)
    dr = dlp.shape[1]
    rp, kp, vp, ap, bp, bonp = _scanprep(up, None, t, asp, lw)
    rs, ks, vs, as_, bs, bons = _scanprep(us, srkv, 1, ass, lw)

    seq = lambda z: z.reshape(nbp, t, dr)
    yp_raw, s_end = _scan(seq(rp), seq(dlp), seq(kp), seq(vp), seq(ap), seq(bp))
    s_new, ys_raw = _step(swkv, rs, dls, ks, vs, as_, bs)
    yrp = _post(yp_raw.reshape(nbp * t, dr), bonp, gp, lw)
    yrs = _post(ys_raw, bons, gs, lw)

    ycp = _conv_seq(hgp, t, lw)
    ycs, nbuf = _conv_step(sconv, hgs, lw)

    hp, lgp = _outproj(xp2, yrp, ycp, lw, alpha)
    hs, lgs = _outproj(xs2, yrs, ycs, lw, alpha)
    h_all = jnp.concatenate([hp, hs], axis=0)
    logits = jnp.concatenate([lgp, lgs], axis=0)

    ne = logits.shape[1]
    moe_tm = _moe_block_rows(h_all.shape[0] * TOP_K, ne)
    gates, pos, row_tok, blk_e, q_eff, valid = _route(logits, moe_tm)
    x_sorted = _gather_rows(h_all, row_tok, q_eff, valid, moe_tm)
    y_rows = _moe_experts(x_sorted, blk_e, q_eff, valid, lw, moe_tm, MOE_SUB)
    y_all = _combine(h_all, gates, pos, y_rows, lw, alpha)

    yp = y_all[:nbp * t].reshape(nbp, t, d)
    ys = y_all[nbp * t:].reshape(nbs, 1, d)
    p_state = (xp[:, -1], up.reshape(nbp, t, -1)[:, -1], _diag_blocks(s_end),
               hgp.reshape(nbp, t, -1)[:, t - (CONV_WIDTH - 1):])
    s_state = (xs2, us, s_new, nbuf)
    return yp, ys, p_state, s_state


def kernel(x_prompt, x_sample, state_shift_x, state_shift_rkv, state_wkv, state_conv, w_in, mu_x, mu_rkv, w0, w_A,
           w_B, a0, a_A, a_B, g_A, g_B, k_k, k_a, r_k, gn_g, gn_b, conv_w, conv_b, conv_ln_g, conv_ln_b, w_out,
           ln1_g, ln1_b, w_router, b_router, w_gu, b_gu, w_down, b_down, ln2_g, ln2_b):
    params = dict(w_in=w_in, mu_x=mu_x, mu_rkv=mu_rkv, w0=w0, w_A=w_A, w_B=w_B, a0=a0, a_A=a_A, a_B=a_B, g_A=g_A,
                  g_B=g_B, k_k=k_k, k_a=k_a, r_k=r_k, gn_g=gn_g, gn_b=gn_b, conv_w=conv_w, conv_b=conv_b,
                  conv_ln_g=conv_ln_g, conv_ln_b=conv_ln_b, w_out=w_out, ln1_g=ln1_g, ln1_b=ln1_b,
                  w_router=w_router, b_router=b_router, w_gu=w_gu, b_gu=b_gu, w_down=w_down, b_down=b_down,
                  ln2_g=ln2_g, ln2_b=ln2_b)
    depth = w_in.shape[0]
    assert x_sample.shape[1] == 1, "the sample group advances one token per step"
    alpha = (2.0 * depth) ** 0.25
    xp, xs = x_prompt, x_sample
    p_states, s_states = [], []
    for l in range(depth):
        lw = {name: p[l] for name, p in params.items()}
        for name in _VEC_PARAMS:
            lw[name] = lw[name].reshape(1, -1)
        for name in _BF16_PARAMS:
            lw[name] = lw[name].astype(BF16)
        xp, xs, p_st, s_st = _layer(xp, xs, state_shift_x[l], state_shift_rkv[l], state_wkv[l], state_conv[l],
                                    lw, alpha)
        p_states.append(p_st)
        s_states.append(s_st)
    stack = lambda states, i: jnp.stack([st[i] for st in states])
    return (xp, xs,
            stack(p_states, 0), stack(p_states, 1), stack(p_states, 2), stack(p_states, 3),
            stack(s_states, 0), stack(s_states, 1), stack(s_states, 2), stack(s_states, 3))
```

```python
import functools
import math

import jax
import jax.numpy as jnp
from jax import lax
from jax.experimental import pallas as pl
from jax.experimental.pallas import tpu as pltpu

F32 = jnp.float32
BF16 = jnp.bfloat16

RWKV_HEAD = 64
CONV_WIDTH = 31
TOP_K = 4
SWIGLU_LIMIT = 7.0
SWIGLU_ALPHA = 1.702
LN_EPS = 1e-5
GN_EPS = 64e-5

LANES = 128
SUBLANES = 8
MXU_DIM = 256
VMEM_LIMIT_BYTES = 56 * 1024 * 1024

SCAN_CHUNK = 64
HEADS_PER_GROUP = MXU_DIM // RWKV_HEAD
CONV_HALO = 32
CONV_ROW_CHUNK = 128
MOE_SUB = 256


def _row_tile(n, target):
    best = None
    for t in range(SUBLANES, min(n, target) + 1, SUBLANES):
        if n % t == 0:
            best = t
    assert best is not None, (n, target)
    return best


def _params(*sem):
    return pltpu.CompilerParams(dimension_semantics=sem, vmem_limit_bytes=VMEM_LIMIT_BYTES)


def _const_spec(shape):
    nd = len(shape)
    return pl.BlockSpec(shape, lambda *_: (0,) * nd, pipeline_mode=pl.Buffered(1))


def _dot(a, b):
    return jnp.dot(a, b, preferred_element_type=F32)


def _split2(x):
    hi = x.astype(BF16)
    lo = (x - hi.astype(F32)).astype(BF16)
    return hi, lo


def _layer_norm(z, g, b):
    mu = jnp.mean(z, axis=-1, keepdims=True)
    zc = z - mu
    var = jnp.mean(zc * zc, axis=-1, keepdims=True)
    return zc * lax.rsqrt(var + LN_EPS) * g + b


def _shift_rows(x, first_row):
    xs = pltpu.roll(x, 1, axis=0)
    row = lax.broadcasted_iota(jnp.int32, x.shape, 0)
    return jnp.where(row == 0, first_row, xs)


def _inproj_kernel(x_ref, xprev_ref, w_in_ref, mu_ref, wA_ref, aA_ref, gA_ref, wB_ref, aB_ref, gB_ref,
                   w0_ref, a0_ref, urkv_ref, hglu_ref, dlog_ref, asig_ref, g_ref, *, seq_tiles, halo):
    x = x_ref[...]
    if halo:
        first = (pl.program_id(0) % seq_tiles) == 0
        prev_row = jnp.where(first, 0.0, xprev_ref[SUBLANES - 1:SUBLANES, :])
        xprev = _shift_rows(x, prev_row)
    else:
        xprev = xprev_ref[...]
    xx = xprev - x
    xb = x.astype(BF16)
    d3 = urkv_ref.shape[1]
    dc = hglu_ref.shape[1]
    urkv_ref[...] = _dot(xb, w_in_ref[:, :d3])
    val = _dot(xb, w_in_ref[:, d3:d3 + dc])
    gate = _dot(xb, w_in_ref[:, d3 + dc:])
    hglu_ref[...] = val * jax.nn.sigmoid(gate)

    xw = (x + xx * mu_ref[0:1, :]).astype(BF16)
    tw = jnp.tanh(_dot(xw, wA_ref[...]))
    wlin = w0_ref[...] + _dot(tw.astype(BF16), wB_ref[...])
    wlog = -jax.nn.softplus(-wlin) - 0.5
    dlog_ref[...] = -jnp.exp(wlog)

    xa = (x + xx * mu_ref[1:2, :]).astype(BF16)
    ta = _dot(xa, aA_ref[...])
    asig_ref[...] = jax.nn.sigmoid(a0_ref[...] + _dot(ta.astype(BF16), aB_ref[...]))

    xg = (x + xx * mu_ref[2:3, :]).astype(BF16)
    tg = jax.nn.sigmoid(_dot(xg, gA_ref[...]))
    g_ref[...] = _dot(tg.astype(BF16), gB_ref[...])


def _inproj(x, xprev, seq_len, lw):
    n, d = x.shape
    d_in = lw["w_in"].shape[1]
    dr = lw["w_B"].shape[1]
    d3 = 3 * dr
    dc = (d_in - d3) // 2
    halo = xprev is None
    tm = _row_tile(seq_len if halo else n, 256)
    seq_tiles = (seq_len // tm) if halo else 1
    if halo:
        prev_spec = pl.BlockSpec((SUBLANES, d), lambda i: (jnp.maximum(i * (tm // SUBLANES) - 1, 0), 0))
        prev_arg = x
    else:
        prev_spec = pl.BlockSpec((tm, d), lambda i: (i, 0))
        prev_arg = xprev
    row = lambda c: pl.BlockSpec((tm, c), lambda i: (i, 0))
    outs = [jax.ShapeDtypeStruct((n, c), F32) for c in (d3, dc, dr, dr, dr)]
    consts = [lw["w_in"], lw["mu_x"], lw["w_A"], lw["a_A"], lw["g_A"], lw["w_B"], lw["a_B"], lw["g_B"],
              lw["w0"], lw["a0"]]
    return pl.pallas_call(
        functools.partial(_inproj_kernel, seq_tiles=seq_tiles, halo=halo),
        grid=(n // tm,),
        in_specs=[row(d), prev_spec] + [_const_spec(c.shape) for c in consts],
        out_specs=[row(c) for c in (d3, dc, dr, dr, dr)],
        out_shape=outs,
        compiler_params=_params("arbitrary"),
        name="inproj",
    )(x, prev_arg, *consts)


def _head_mask(n, dtype):
    r = lax.broadcasted_iota(jnp.int32, (n, n), 0) // RWKV_HEAD
    c = lax.broadcasted_iota(jnp.int32, (n, n), 1) // RWKV_HEAD
    return (r == c).astype(dtype)


def _head_sum(x, ones_bd):
    cols = []
    for c0 in range(0, x.shape[1], MXU_DIM):
        hi, lo = _split2(x[:, c0:c0 + MXU_DIM])
        cols.append(_dot(hi, ones_bd) + _dot(lo, ones_bd))
    return cols[0] if len(cols) == 1 else jnp.concatenate(cols, axis=1)


def _scanprep_kernel(u_ref, uprev_ref, mu_ref, kk_ref, ka_ref, rk_ref, asig_ref,
                     r_ref, k_ref, v_ref, a_ref, b_ref, bonus_ref, *, seq_tiles, halo):
    u = u_ref[...]
    if halo:
        first = (pl.program_id(0) % seq_tiles) == 0
        prev_row = jnp.where(first, 0.0, uprev_ref[SUBLANES - 1:SUBLANES, :])
        uprev = _shift_rows(u, prev_row)
    else:
        uprev = uprev_ref[...]
    rkv = u + (uprev - u) * mu_ref[...]
    dr = r_ref.shape[1]
    r = rkv[:, :dr]
    k = rkv[:, dr:2 * dr]
    v = rkv[:, 2 * dr:]
    asig = asig_ref[...]
    ones_bd = _head_mask(MXU_DIM, BF16)
    kk = k * kk_ref[...]
    kk = kk * lax.rsqrt(jnp.maximum(_head_sum(kk * kk, ones_bd), 1e-24))
    k = k * (1.0 + (asig - 1.0) * ka_ref[...])
    r_ref[...] = r
    k_ref[...] = k
    v_ref[...] = v
    a_ref[...] = -kk
    b_ref[...] = kk * asig
    bonus_ref[...] = _head_sum(r * k * rk_ref[...], ones_bd) * v


def _scanprep(u_rkv, uprev, seq_len, asig, lw):
    n, d3 = u_rkv.shape
    dr = d3 // 3
    halo = uprev is None
    tm = _row_tile(seq_len if halo else n, 256)
    seq_tiles = (seq_len // tm) if halo else 1
    if halo:
        prev_spec = pl.BlockSpec((SUBLANES, d3), lambda i: (jnp.maximum(i * (tm // SUBLANES) - 1, 0), 0))
        prev_arg = u_rkv
    else:
        prev_spec = pl.BlockSpec((tm, d3), lambda i: (i, 0))
        prev_arg = uprev
    row = lambda c: pl.BlockSpec((tm, c), lambda i: (i, 0))
    consts = [lw["mu_rkv"], lw["k_k"], lw["k_a"], lw["r_k"]]
    return pl.pallas_call(
        functools.partial(_scanprep_kernel, seq_tiles=seq_tiles, halo=halo),
        grid=(n // tm,),
        in_specs=[row(d3), prev_spec] + [_const_spec(c.shape) for c in consts] + [row(dr)],
        out_specs=[row(dr)] * 6,
        out_shape=[jax.ShapeDtypeStruct((n, dr), F32)] * 6,
        compiler_params=_params("arbitrary"),
        name="scanprep",
    )(u_rkv, prev_arg, *consts, asig)


def _blockdiag(x, mask):
    reps = mask.shape[0] // x.shape[0]
    return jnp.concatenate([x] * reps, axis=0) * mask


def _dot_nt(a, b):
    return lax.dot_general(a, b, (((1,), (1,)), ((), ())), preferred_element_type=F32)


def _dot_tn(a, b):
    return lax.dot_general(a, b, (((0,), (0,)), ((), ())), preferred_element_type=F32)


def _scan_kernel(r_ref, dl_ref, k_ref, v_ref, a_ref, b_ref, y_ref, s_ref, h_scr):
    c = pl.program_id(0)
    nb = r_ref.shape[0]
    C = r_ref.shape[1]
    W = MXU_DIM
    ng = r_ref.shape[2] // W

    @pl.when(c == 0)
    def _():
        h_scr[...] = jnp.zeros_like(h_scr)

    ti = lax.broadcasted_iota(jnp.int32, (C, C), 0)
    tj = lax.broadcasted_iota(jnp.int32, (C, C), 1)
    tri = (ti >= tj).astype(BF16)
    t_row = lax.broadcasted_iota(jnp.int32, (C, W), 0)
    j_col = lax.broadcasted_iota(jnp.int32, (C, W), 1) % C
    strict = j_col < t_row
    incl = j_col <= t_row
    eye_cat = (j_col == t_row).astype(F32)
    rb = lax.broadcasted_iota(jnp.int32, (HEADS_PER_GROUP * C, W), 0) // C
    cb = lax.broadcasted_iota(jnp.int32, (HEADS_PER_GROUP * C, W), 1) // RWKV_HEAD
    bmask = (rb == cb).astype(BF16)
    hmask = _head_mask(W, F32)

    chains = [(bi, gi) for bi in range(nb) for gi in range(ng)]
    each = lambda f, *lists: [f(*xs) for xs in zip(*lists)]
    load = lambda ref: [ref[bi, :, gi * W:(gi + 1) * W] for bi, gi in chains]
    r, dl, k, v, a, b = (load(ref) for ref in (r_ref, dl_ref, k_ref, v_ref, a_ref, b_ref))
    sbd = [h_scr[bi, gi] for bi, gi in chains]
    bd = lambda x: _blockdiag(x, bmask)

    def cumsum(d):
        d_hi = d.astype(BF16)
        d_r1 = d - d_hi.astype(F32)
        d_mid = d_r1.astype(BF16)
        d_lo = (d_r1 - d_mid.astype(F32)).astype(BF16)
        return _dot(tri, d_hi) + (_dot(tri, d_mid) + _dot(tri, d_lo))

    cum = each(cumsum, dl)
    cum_last = each(lambda x: x[C - 1:C, :], cum)
    e_neg = each(lambda x: jnp.exp(-x), cum)
    e_end = each(lambda x, xl: jnp.exp(xl - x), cum, cum_last)
    at = each(lambda x, cu, d: (x * jnp.exp(cu - d)).astype(BF16), a, cum, dl)
    rt = each(lambda x, cu: (x * jnp.exp(cu)).astype(BF16), r, cum)
    bt = each(lambda x, e: (x * e).astype(BF16), b, e_neg)
    kt = each(lambda x, e: (x * e).astype(BF16), k, e_neg)
    bh = each(lambda x, e: (x * e).astype(BF16), b, e_end)
    kh = each(lambda x, e: (x * e).astype(BF16), k, e_end)
    vb = each(lambda x: x.astype(BF16), v)
    ar = each(lambda x, y: jnp.concatenate([x, y], axis=0), at, rt)

    p_b = each(lambda x, y: _dot_nt(x, bd(y)), ar, bt)
    p_k = each(lambda x, y: _dot_nt(x, bd(y)), ar, kt)
    arh = each(lambda x, s: _dot_nt(x, s.astype(BF16)), ar, sbd)
    p_ab = each(lambda p: jnp.where(strict, p[:C], 0.0), p_b)
    p_rb = each(lambda p: jnp.where(incl, p[C:], 0.0).astype(BF16), p_b)
    p_akrk = each(lambda p: jnp.concatenate([jnp.where(strict, p[:C], 0.0), jnp.where(incl, p[C:], 0.0)],
                                            axis=0).astype(BF16), p_k)

    pv = each(lambda p, x: _dot(p, bd(x)), p_akrk, vb)
    n_sq = int(math.log2(C))
    nn_b = each(lambda p: p.astype(BF16), p_ab)
    nn_b = each(lambda n: _dot(n, bd(n)).astype(BF16), nn_b)
    w = each(lambda x, y: (x[:C] + y[:C]).astype(BF16), arh, pv)
    tm = each(lambda p: eye_cat + p, p_ab)
    for i in range(1, n_sq):
        last = i == n_sq - 1
        lhs = each(lambda t, n: t.astype(BF16) if last else jnp.concatenate([t.astype(BF16), n], axis=0), tm, nn_b)
        prod = each(lambda l, n: _dot(l, bd(n)), lhs, nn_b)
        tm = each(lambda t, p: t + p[:C], tm, prod)
        if not last:
            nn_b = each(lambda p: p[C:].astype(BF16), prod)
    ub = each(lambda t, x: _dot(t.astype(BF16), bd(x)).astype(BF16), tm, w)

    yv = each(lambda x, y, p, u: x[C:] + y[C:] + _dot(p, bd(u)), arh, pv, p_rb, ub)
    upd = each(lambda u, x, y, z: _dot_tn(jnp.concatenate([u, x], axis=0), jnp.concatenate([y, z], axis=0)),
               ub, vb, bh, kh)
    for (bi, gi), y, s, xl, up in zip(chains, yv, sbd, cum_last, upd):
        y_ref[bi, :, gi * W:(gi + 1) * W] = y
        h_scr[bi, gi] = (s * jnp.exp(xl) + up) * hmask

    @pl.when(c == pl.num_programs(0) - 1)
    def _():
        s_ref[...] = h_scr[...]


def _scan(r, dl, k, v, a, b):
    nb, t, dr = r.shape
    ng = dr // MXU_DIM
    C = SCAN_CHUNK
    spec = pl.BlockSpec((nb, C, dr), lambda c: (0, c, 0))
    return pl.pallas_call(
        _scan_kernel,
        grid=(t // C,),
        in_specs=[spec] * 6,
        out_specs=[spec, pl.BlockSpec((nb, ng, MXU_DIM, MXU_DIM), lambda c: (0, 0, 0, 0))],
        out_shape=[jax.ShapeDtypeStruct((nb, t, dr), F32),
                   jax.ShapeDtypeStruct((nb, ng, MXU_DIM, MXU_DIM), F32)],
        scratch_shapes=[pltpu.VMEM((nb, ng, MXU_DIM, MXU_DIM), F32)],
        compiler_params=_params("arbitrary"),
        name="wkv_scan",
    )(r, dl, k, v, a, b)


def _step_kernel(s_ref, r_ref, dl_ref, k_ref, v_ref, a_ref, b_ref, sout_ref, y_ref):
    S = s_ref[...]
    n = S.shape[-1]
    eye = (lax.broadcasted_iota(jnp.int32, (n, n), 0) == lax.broadcasted_iota(jnp.int32, (n, n), 1)).astype(F32)
    row = lambda ref: ref[...][:, :, None, :]
    sa = jnp.sum(S * row(a_ref), axis=-1, keepdims=True)
    v_col = jnp.sum(eye * row(v_ref), axis=-1, keepdims=True)
    s_new = S * jnp.exp(row(dl_ref)) + sa * row(b_ref) + v_col * row(k_ref)
    sout_ref[...] = s_new
    y_col = jnp.sum(s_new * row(r_ref), axis=-1, keepdims=True)
    y_ref[...] = jnp.sum(eye * y_col, axis=-2)


def _step(S0, r, dl, k, v, a, b):
    nb, nh, n, _ = S0.shape
    bb = _row_tile(nb, 8)
    vec = lambda x: x.reshape(nb, nh, n)
    sspec = pl.BlockSpec((bb, nh, n, n), lambda i: (i, 0, 0, 0))
    vspec = pl.BlockSpec((bb, nh, n), lambda i: (i, 0, 0))
    s_new, y = pl.pallas_call(
        _step_kernel,
        grid=(nb // bb,),
        in_specs=[sspec] + [vspec] * 6,
        out_specs=[sspec, vspec],
        out_shape=[jax.ShapeDtypeStruct(S0.shape, F32), jax.ShapeDtypeStruct((nb, nh, n), F32)],
        compiler_params=_params("arbitrary"),
        name="wkv_step",
    )(S0, *(vec(x) for x in (r, dl, k, v, a, b)))
    return s_new, y.reshape(nb, nh * n)


def _post_kernel(y_ref, bonus_ref, g_ref, gng_ref, gnb_ref, o_ref):
    y = y_ref[...]
    ones_bd = _head_mask(MXU_DIM, BF16)
    inv_n = 1.0 / RWKV_HEAD
    mu = _head_sum(y, ones_bd) * inv_n
    yc = y - mu
    var = _head_sum(yc * yc, ones_bd) * inv_n
    yn = yc * lax.rsqrt(var + GN_EPS) * gng_ref[...] + gnb_ref[...]
    o_ref[...] = (yn + bonus_ref[...]) * g_ref[...]


def _post(y, bonus, g, lw):
    n, dr = y.shape
    tm = _row_tile(n, 512)
    row = pl.BlockSpec((tm, dr), lambda i: (i, 0))
    return pl.pallas_call(
        _post_kernel,
        grid=(n // tm,),
        in_specs=[row, row, row, _const_spec(lw["gn_g"].shape), _const_spec(lw["gn_b"].shape)],
        out_specs=row,
        out_shape=jax.ShapeDtypeStruct((n, dr), F32),
        compiler_params=_params("arbitrary"),
        name="wkv_post",
    )(y, bonus, g, lw["gn_g"], lw["gn_b"])


def _conv_seq_kernel(h_ref, halo_ref, w_ref, b_ref, lng_ref, lnb_ref, o_ref, hp_scr, *, seq_tiles):
    tt = h_ref.shape[0]
    first = (pl.program_id(0) % seq_tiles) == 0
    hp_scr[0:CONV_HALO, :] = jnp.where(first, 0.0, halo_ref[...])
    hp_scr[CONV_HALO:, :] = h_ref[...]
    off = CONV_HALO - (CONV_WIDTH - 1)
    rc = CONV_ROW_CHUNK if tt % CONV_ROW_CHUNK == 0 else tt
    for c0 in range(0, o_ref.shape[1], LANES):
        lanes = slice(c0, c0 + LANES)
        for t0 in range(0, tt, rc):
            acc = jnp.broadcast_to(b_ref[:, lanes], (rc, LANES))
            for p in range(SUBLANES):
                rows = rc if p == 0 else rc + SUBLANES
                part = None
                for m in range((CONV_HALO + SUBLANES) // SUBLANES):
                    j = SUBLANES * m + p - off
                    if 0 <= j < CONV_WIDTH:
                        term = hp_scr[t0 + SUBLANES * m:t0 + SUBLANES * m + rows, lanes] * w_ref[j:j + 1, lanes]
                        part = term if part is None else part + term
                acc = acc + part[p:p + rc]
            o_ref[t0:t0 + rc, lanes] = acc
    z = _layer_norm(o_ref[...], lng_ref[...], lnb_ref[...])
    o_ref[...] = z * jax.nn.sigmoid(z)


def _conv_seq(h, seq_len, lw):
    n, dc = h.shape
    tt = _row_tile(seq_len, 256)
    assert tt % CONV_HALO == 0
    seq_tiles = seq_len // tt
    row = pl.BlockSpec((tt, dc), lambda i: (i, 0))
    halo = pl.BlockSpec((CONV_HALO, dc), lambda i: (jnp.maximum(i * (tt // CONV_HALO) - 1, 0), 0))
    consts = [lw["conv_w"], lw["conv_b"], lw["conv_ln_g"], lw["conv_ln_b"]]
    return pl.pallas_call(
        functools.partial(_conv_seq_kernel, seq_tiles=seq_tiles),
        grid=(n // tt,),
        in_specs=[row, halo] + [_const_spec(c.shape) for c in consts],
        out_specs=row,
        out_shape=jax.ShapeDtypeStruct((n, dc), F32),
        scratch_shapes=[pltpu.VMEM((CONV_HALO + tt, dc), F32)],
        compiler_params=_params("arbitrary"),
        name="conv_seq",
    )(h, h, *consts)


def _conv_step_kernel(buf_ref, h_ref, w_ref, b_ref, lng_ref, lnb_ref, o_ref, nbuf_ref):
    buf = buf_ref[...]
    h = h_ref[...]
    acc = b_ref[...] + h * w_ref[CONV_WIDTH - 1:CONV_WIDTH, :]
    for j in range(CONV_WIDTH - 1):
        acc = acc + buf[:, j, :] * w_ref[j:j + 1, :]
    z = _layer_norm(acc, lng_ref[...], lnb_ref[...])
    o_ref[...] = z * jax.nn.sigmoid(z)
    nbuf_ref[:, 0:CONV_WIDTH - 2, :] = buf[:, 1:, :]
    nbuf_ref[:, CONV_WIDTH - 2, :] = h


def _conv_step(buf, h, lw):
    nb, wm1, dc = buf.shape
    bb = _row_tile(nb, 32)
    bspec = pl.BlockSpec((bb, wm1, dc), lambda i: (i, 0, 0))
    row = pl.BlockSpec((bb, dc), lambda i: (i, 0))
    consts = [lw["conv_w"], lw["conv_b"], lw["conv_ln_g"], lw["conv_ln_b"]]
    return pl.pallas_call(
        _conv_step_kernel,
        grid=(nb // bb,),
        in_specs=[bspec, row] + [_const_spec(c.shape) for c in consts],
        out_specs=[row, bspec],
        out_shape=[jax.ShapeDtypeStruct((nb, dc), F32), jax.ShapeDtypeStruct(buf.shape, F32)],
        compiler_params=_params("arbitrary"),
        name="conv_step",
    )(buf, h, *consts)


def _outproj_kernel(*refs, alpha, n_dst):
    x_ref, yr_ref, yc_ref, wo_ref, g_ref, b_ref, wr_ref, wr_hi_ref, br_ref, h_ref, logit_ref = refs[n_dst:]
    ymix = jnp.concatenate([yr_ref[...].astype(BF16), yc_ref[...].astype(BF16)], axis=1)
    h = _layer_norm(alpha * x_ref[...] + _dot(ymix, wo_ref[...]), g_ref[...], b_ref[...])
    h_ref[...] = h
    ne = logit_ref.shape[1]
    h_hi, h_lo = _split2(h)
    s = _dot(h_hi, wr_ref[...]) + _dot(h_lo, wr_hi_ref[...])
    logit_ref[...] = s[:, :ne] + s[:, ne:] + br_ref[...]


def _outproj(x, y_rwkv, y_conv, lw, alpha, n_total, row0, dst=()):
    n, d = x.shape
    dr = y_rwkv.shape[1]
    dc = y_conv.shape[1]
    ne = lw["w_router"].shape[1]
    tm = _row_tile(math.gcd(n, row0) if row0 else n, 256)
    blk0 = row0 // tm
    row = lambda c: pl.BlockSpec((tm, c), lambda i: (i, 0))
    out_row = lambda c: pl.BlockSpec((tm, c), lambda i: (blk0 + i, 0))
    w_hi, w_lo = _split2(lw["w_router"])
    consts = [lw["w_out"], lw["ln1_g"], lw["ln1_b"], jnp.concatenate([w_hi, w_lo], axis=1),
              jnp.concatenate([w_hi, jnp.zeros_like(w_hi)], axis=1), lw["b_router"]]
    return pl.pallas_call(
        functools.partial(_outproj_kernel, alpha=alpha, n_dst=len(dst)),
        grid=(n // tm,),
        in_specs=[pl.BlockSpec(memory_space=pl.ANY)] * len(dst) + [row(d), row(dr), row(dc)]
        + [_const_spec(c.shape) for c in consts],
        out_specs=[out_row(d), out_row(ne)],
        out_shape=[jax.ShapeDtypeStruct((n_total, d), F32), jax.ShapeDtypeStruct((n_total, ne), F32)],
        input_output_aliases={i: i for i in range(len(dst))},
        compiler_params=_params("arbitrary"),
        name="outproj",
    )(*dst, x, y_rwkv, y_conv, *consts)


def _route(logits, moe_tm):
    n_tok, ne = logits.shape
    top_val, top_idx = lax.top_k(logits, TOP_K)
    gates = jax.nn.softmax(top_val, axis=-1)
    n_assign = n_tok * TOP_K
    e_flat = top_idx.reshape(n_assign).astype(jnp.int32)
    tok_flat = jnp.repeat(jnp.arange(n_tok, dtype=jnp.int32), TOP_K)
    onehot = (e_flat[:, None] == jnp.arange(ne, dtype=jnp.int32)[None, :]).astype(jnp.int32)
    counts = onehot.sum(0)
    rank = jnp.take_along_axis(jnp.cumsum(onehot, axis=0), e_flat[:, None], axis=1)[:, 0] - 1
    nblk_e = (counts + moe_tm - 1) // moe_tm
    blk_end = jnp.cumsum(nblk_e)
    blk_start = blk_end - nblk_e
    n_active = blk_end[-1]
    nb_max = n_assign // moe_tm + ne
    pos = blk_start[e_flat] * moe_tm + rank
    row_tok = jnp.zeros((nb_max * moe_tm,), jnp.int32).at[pos].set(tok_flat)
    q = jnp.arange(nb_max, dtype=jnp.int32)
    q_eff = jnp.minimum(q, n_active - 1)
    blk_e = jnp.minimum(jnp.searchsorted(blk_end, q_eff, side="right"), ne - 1).astype(jnp.int32)
    valid = jnp.clip(counts[blk_e] - (q_eff - blk_start[blk_e]) * moe_tm, 0, moe_tm)
    valid = jnp.where(q < n_active, valid, 0).astype(jnp.int32)
    return gates, pos.astype(jnp.int32), row_tok, blk_e, q_eff.astype(jnp.int32), valid


def _gather_kernel(qeff_ref, valid_ref, tok_ref, h_hbm, o_ref, buf, sem):
    q = pl.program_id(0)
    tm = o_ref.shape[0]
    nvalid = valid_ref[q]

    def row_copy(r, tok):
        return pltpu.make_async_copy(h_hbm.at[pl.ds(tok, 1)], buf.at[pl.ds(r, 1)], sem)

    @pl.when(nvalid > 0)
    def _():
        base = q * tm
        ngroups = (nvalid + SUBLANES - 1) // SUBLANES

        def issue(g, c):
            r0 = pl.multiple_of(g * SUBLANES, SUBLANES)
            for i in range(SUBLANES):
                row_copy(r0 + i, tok_ref[base + r0 + i]).start()
            return c

        lax.fori_loop(0, ngroups, issue, 0)

        def wait(g, c):
            r0 = pl.multiple_of(g * SUBLANES, SUBLANES)
            pltpu.make_async_copy(h_hbm.at[pl.ds(0, SUBLANES)], buf.at[pl.ds(r0, SUBLANES)], sem).wait()
            return c

        lax.fori_loop(0, ngroups, wait, 0)
        rows = lax.broadcasted_iota(jnp.int32, buf.shape, 0)
        o_ref[...] = jnp.where(rows < nvalid, buf[...], 0.0).astype(o_ref.dtype)


def _gather_rows(h, row_tok, q_eff, valid, moe_tm):
    n_tok, d = h.shape
    nb_max = q_eff.shape[0]
    grid_spec = pltpu.PrefetchScalarGridSpec(
        num_scalar_prefetch=3,
        grid=(nb_max,),
        in_specs=[pl.BlockSpec(memory_space=pl.ANY)],
        out_specs=pl.BlockSpec((moe_tm, d), lambda q, qe, va, tk: (qe[q], 0)),
        scratch_shapes=[pltpu.VMEM((moe_tm, d), F32), pltpu.SemaphoreType.DMA(())],
    )
    return pl.pallas_call(
        _gather_kernel,
        grid_spec=grid_spec,
        out_shape=jax.ShapeDtypeStruct((nb_max * moe_tm, d), BF16),
        compiler_params=_params("arbitrary"),
        name="moe_gather",
    )(q_eff, valid, row_tok, h)


def _moe_kernel(be_ref, qeff_ref, valid_ref, x_ref, wg_ref, wl_ref, bg_ref, bl_ref, wd_ref, bd_ref, o_ref,
                wg_s, wl_s, wd_s, *, sub):
    q = pl.program_id(0)
    j = pl.program_id(1)
    nvalid = valid_ref[q]
    unit = sub // 2
    n_units = (nvalid + unit - 1) // unit
    n_all = o_ref.shape[0] // unit

    def fill(s, value):
        r0 = pl.multiple_of(s * unit, unit)
        o_ref[pl.ds(r0, unit), :] = jnp.broadcast_to(value, (unit, o_ref.shape[1]))

    def rows_block(r0, rows):
        x = x_ref[pl.ds(r0, rows), :]
        g = jnp.minimum(_dot(x, wg_s[...]) + bg_ref[0], SWIGLU_LIMIT)
        l = jnp.clip(_dot(x, wl_s[...]) + bl_ref[0], -SWIGLU_LIMIT, SWIGLU_LIMIT)
        act = g * jax.nn.sigmoid(SWIGLU_ALPHA * g) * (l + 1.0)
        o_ref[pl.ds(r0, rows), :] += _dot(act.astype(BF16), wd_s[...])

    @pl.when(nvalid > 0)
    def _():
        @pl.when(j == 0)
        def _():
            lax.fori_loop(0, n_units, lambda s, c: (fill(s, bd_ref[0]), c)[1], 0)
            lax.fori_loop(n_units, n_all, lambda s, c: (fill(s, jnp.zeros((1, 1), F32)), c)[1], 0)

        wg_s[...] = wg_ref[0].astype(BF16)
        wl_s[...] = wl_ref[0].astype(BF16)
        wd_s[...] = wd_ref[0].astype(BF16)
        rows_block(0, unit)
        rest = n_units - 1
        n_pairs = rest // 4
        tail = rest - n_pairs * 4

        def pair(i, c):
            r0 = pl.multiple_of(unit + i * 2 * sub, unit)
            rows_block(r0, sub)
            rows_block(r0 + sub, sub)
            return c

        lax.fori_loop(0, n_pairs, pair, 0)
        t0 = pl.multiple_of(unit + n_pairs * 2 * sub, unit)

        @pl.when(tail >= 2)
        def _():
            rows_block(t0, sub)

        @pl.when(tail % 2 == 1)
        def _():
            rows_block(pl.multiple_of(t0 + (tail // 2) * sub, unit), unit)


def _moe_experts(x_sorted, blk_e, q_eff, valid, lw, moe_tm, sub):
    n_rows, d = x_sorted.shape
    ne, _, f2 = lw["w_gu"].shape
    f = f2 // 2
    tf = min(f, MXU_DIM)
    nf = f // tf
    nb_max = q_eff.shape[0]
    b_gu = lw["b_gu"].reshape(ne, 1, f2)
    b_down = lw["b_down"].reshape(ne, 1, d)

    def jf(q, j, va):
        return jnp.where(va[q] > 0, j, nf - 1)

    grid_spec = pltpu.PrefetchScalarGridSpec(
        num_scalar_prefetch=3,
        grid=(nb_max, nf),
        in_specs=[
            pl.BlockSpec((moe_tm, d), lambda q, j, be, qe, va: (qe[q], 0)),
            pl.BlockSpec((1, d, tf), lambda q, j, be, qe, va: (be[q], 0, jf(q, j, va))),
            pl.BlockSpec((1, d, tf), lambda q, j, be, qe, va: (be[q], 0, nf + jf(q, j, va))),
            pl.BlockSpec((1, 1, tf), lambda q, j, be, qe, va: (be[q], 0, jf(q, j, va))),
            pl.BlockSpec((1, 1, tf), lambda q, j, be, qe, va: (be[q], 0, nf + jf(q, j, va))),
            pl.BlockSpec((1, tf, d), lambda q, j, be, qe, va: (be[q], jf(q, j, va), 0)),
            pl.BlockSpec((1, 1, d), lambda q, j, be, qe, va: (be[q], 0, 0)),
        ],
        out_specs=pl.BlockSpec((moe_tm, d), lambda q, j, be, qe, va: (qe[q], 0)),
        scratch_shapes=[pltpu.VMEM((d, tf), BF16), pltpu.VMEM((d, tf), BF16), pltpu.VMEM((tf, d), BF16)],
    )
    return pl.pallas_call(
        functools.partial(_moe_kernel, sub=sub),
        grid_spec=grid_spec,
        out_shape=jax.ShapeDtypeStruct((n_rows, d), F32),
        compiler_params=_params("arbitrary", "arbitrary"),
        name="moe_experts",
    )(blk_e, q_eff, valid, x_sorted, lw["w_gu"], lw["w_gu"], b_gu, b_gu, lw["w_down"], b_down)


def _combine_kernel(pos_ref, h_ref, gates_ref, yrows_hbm, g_ref, b_ref, op_ref, os_ref, buf, sem, *, alpha, n_first):
    i = pl.program_id(0)
    tm = h_ref.shape[0]

    def row_copy(r, k, p):
        return pltpu.make_async_copy(yrows_hbm.at[pl.ds(p, 1)], buf.at[k, pl.ds(r, 1)], sem)

    def issue(r2, c):
        for dr in range(2):
            r = r2 * 2 + dr
            for k in range(TOP_K):
                row_copy(r, k, pos_ref[(i * tm + r) * TOP_K + k]).start()
        return c

    lax.fori_loop(0, tm // 2, issue, 0)
    for k in range(TOP_K):
        pltpu.make_async_copy(yrows_hbm.at[pl.ds(0, tm)], buf.at[k], sem).wait()
    gates = gates_ref[...]
    ffn = buf[0] * gates[:, 0:1]
    for k in range(1, TOP_K):
        ffn = ffn + buf[k] * gates[:, k:k + 1]
    y = _layer_norm(alpha * h_ref[...] + ffn, g_ref[...], b_ref[...])

    @pl.when(i < n_first)
    def _():
        op_ref[...] = y

    @pl.when(i >= n_first)
    def _():
        os_ref[...] = y


def _combine(h, gates, pos, y_rows, lw, alpha, n_prompt):
    n_tok, d = h.shape
    tm = _row_tile(math.gcd(n_prompt, n_tok - n_prompt), 128)
    n_first = n_prompt // tm
    grid_spec = pltpu.PrefetchScalarGridSpec(
        num_scalar_prefetch=1,
        grid=(n_tok // tm,),
        in_specs=[
            pl.BlockSpec((tm, d), lambda i, p: (i, 0)),
            pl.BlockSpec((tm, TOP_K), lambda i, p: (i, 0)),
            pl.BlockSpec(memory_space=pl.ANY),
            pl.BlockSpec((1, d), lambda i, p: (0, 0)),
            pl.BlockSpec((1, d), lambda i, p: (0, 0)),
        ],
        out_specs=[pl.BlockSpec((tm, d), lambda i, p: (jnp.minimum(i, n_first - 1), 0)),
                   pl.BlockSpec((tm, d), lambda i, p: (jnp.maximum(i - n_first, 0), 0))],
        scratch_shapes=[pltpu.VMEM((TOP_K, tm, d), F32), pltpu.SemaphoreType.DMA(())],
    )
    return pl.pallas_call(
        functools.partial(_combine_kernel, alpha=alpha, n_first=n_first),
        grid_spec=grid_spec,
        out_shape=[jax.ShapeDtypeStruct((n_prompt, d), F32), jax.ShapeDtypeStruct((n_tok - n_prompt, d), F32)],
        compiler_params=_params("arbitrary"),
        name="moe_combine",
    )(pos, h, gates, y_rows, lw["ln2_g"], lw["ln2_b"])


def _moe_block_rows(n_assign, ne):
    mean = -(-n_assign // ne)
    tm = -(-(mean * 6 // 5) // MOE_SUB) * MOE_SUB
    return max(MOE_SUB, min(tm, 5 * MOE_SUB))


_VEC_PARAMS = ("mu_rkv", "w0", "a0", "k_k", "k_a", "r_k", "gn_g", "gn_b", "conv_b", "conv_ln_g", "conv_ln_b",
               "ln1_g", "ln1_b", "b_router", "ln2_g", "ln2_b")
_BF16_PARAMS = ("w_in", "w_A", "w_B", "a_A", "a_B", "g_A", "g_B", "w_out")


def _diag_blocks(s):
    nb, ng = s.shape[:2]
    s6 = s.reshape(nb, ng, HEADS_PER_GROUP, RWKV_HEAD, HEADS_PER_GROUP, RWKV_HEAD)
    d = jnp.stack([s6[:, :, h, :, h, :] for h in range(HEADS_PER_GROUP)], axis=2)
    return d.reshape(nb, ng * HEADS_PER_GROUP, RWKV_HEAD, RWKV_HEAD)


def _layer(xp, xs, sx, srkv, swkv, sconv, lw, alpha):
    nbp, t, d = xp.shape
    nbs = xs.shape[0]
    xp2 = xp.reshape(nbp * t, d)
    xs2 = xs.reshape(nbs, d)

    up, hgp, dlp, asp, gp = _inproj(xp2, None, t, lw)
    us, hgs, dls, ass, gs = _inproj(xs2, sx, 1, lw)
    dr = dlp.shape[1]
    rp, kp, vp, ap, bp, bonp = _scanprep(up, None, t, asp, lw)
    rs, ks, vs, as_, bs, bons = _scanprep(us, srkv, 1, ass, lw)

    seq = lambda z: z.reshape(nbp, t, dr)
    yp_raw, s_end = _scan(seq(rp), seq(dlp), seq(kp), seq(vp), seq(ap), seq(bp))
    s_new, ys_raw = _step(swkv, rs, dls, ks, vs, as_, bs)
    yrp = _post(yp_raw.reshape(nbp * t, dr), bonp, gp, lw)
    yrs = _post(ys_raw, bons, gs, lw)

    ycp = _conv_seq(hgp, t, lw)
    ycs, nbuf = _conv_step(sconv, hgs, lw)

    n_prompt = nbp * t
    n_tok = n_prompt + nbs
    dst = _outproj(xp2, yrp, ycp, lw, alpha, n_tok, 0)
    h_all, logits = _outproj(xs2, yrs, ycs, lw, alpha, n_tok, n_prompt, dst=tuple(dst))

    ne = logits.shape[1]
    moe_tm = _moe_block_rows(n_tok * TOP_K, ne)
    gates, pos, row_tok, blk_e, q_eff, valid = _route(logits, moe_tm)
    x_sorted = _gather_rows(h_all, row_tok, q_eff, valid, moe_tm)
    y_rows = _moe_experts(x_sorted, blk_e, q_eff, valid, lw, moe_tm, MOE_SUB)
    yp, ys = _combine(h_all, gates, pos, y_rows, lw, alpha, n_prompt)
    yp = yp.reshape(nbp, t, d)
    ys = ys.reshape(nbs, 1, d)
    p_state = (xp[:, -1], up.reshape(nbp, t, -1)[:, -1], _diag_blocks(s_end),
               hgp.reshape(nbp, t, -1)[:, t - (CONV_WIDTH - 1):])
    s_state = (xs2, us, s_new, nbuf)
    return yp, ys, p_state, s_state


def kernel(x_prompt, x_sample, state_shift_x, state_shift_rkv, state_wkv, state_conv, w_in, mu_x, mu_rkv, w0, w_A,
           w_B, a0, a_A, a_B, g_A, g_B, k_k, k_a, r_k, gn_g, gn_b, conv_w, conv_b, conv_ln_g, conv_ln_b, w_out,
           ln1_g, ln1_b, w_router, b_router, w_gu, b_gu, w_down, b_down, ln2_g, ln2_b):
    params = dict(w_in=w_in, mu_x=mu_x, mu_rkv=mu_rkv, w0=w0, w_A=w_A, w_B=w_B, a0=a0, a_A=a_A, a_B=a_B, g_A=g_A,
                  g_B=g_B, k_k=k_k, k_a=k_a, r_k=r_k, gn_g=gn_g, gn_b=gn_b, conv_w=conv_w, conv_b=conv_b,
                  conv_ln_g=conv_ln_g, conv_ln_b=conv_ln_b, w_out=w_out, ln1_g=ln1_g, ln1_b=ln1_b,
                  w_router=w_router, b_router=b_router, w_gu=w_gu, b_gu=b_gu, w_down=w_down, b_down=b_down,
                  ln2_g=ln2_g, ln2_b=ln2_b)
    depth = w_in.shape[0]
    assert x_sample.shape[1] == 1, "the sample group advances one token per step"
    alpha = (2.0 * depth) ** 0.25
    xp, xs = x_prompt, x_sample
    p_states, s_states = [], []
    for l in range(depth):
        lw = {name: p[l] for name, p in params.items()}
        for name in _VEC_PARAMS:
            lw[name] = lw[name].reshape(1, -1)
        for name in _BF16_PARAMS:
            lw[name] = lw[name].astype(BF16)
        xp, xs, p_st, s_st = _layer(xp, xs, state_shift_x[l], state_shift_rkv[l], state_wkv[l], state_conv[l],
                                    lw, alpha)
        p_states.append(p_st)
        s_states.append(s_st)
    stack = lambda states, i: jnp.stack([st[i] for st in states])
    return (xp, xs,
            stack(p_states, 0), stack(p_states, 1), stack(p_states, 2), stack(p_states, 3),
            stack(s_states, 0), stack(s_states, 1), stack(s_states, 2), stack(s_states, 3))
```

```python
import functools
import math

import jax
import jax.numpy as jnp
from jax import lax
from jax.experimental import pallas as pl
from jax.experimental.pallas import tpu as pltpu

F32 = jnp.float32
BF16 = jnp.bfloat16

RWKV_HEAD = 64
CONV_WIDTH = 31
TOP_K = 4
SWIGLU_LIMIT = 7.0
SWIGLU_ALPHA = 1.702
LN_EPS = 1e-5
GN_EPS = 64e-5

LANES = 128
SUBLANES = 8
MXU_DIM = 256
VMEM_BYTES = 64 * 1024 * 1024
VMEM_LIMIT_BYTES = 56 * 1024 * 1024
MOE_VMEM_LIMIT_BYTES = VMEM_BYTES - 3 * 1024 * 1024

SCAN_CHUNK = 64
HEADS_PER_GROUP = MXU_DIM // RWKV_HEAD
CONV_HALO = 32
CONV_ROW_CHUNK = 128
MOE_SUB = 256


def _row_tile(n, target):
    best = None
    for t in range(SUBLANES, min(n, target) + 1, SUBLANES):
        if n % t == 0:
            best = t
    assert best is not None, (n, target)
    return best


def _params(*sem, vmem_limit_bytes=VMEM_LIMIT_BYTES):
    return pltpu.CompilerParams(dimension_semantics=sem, vmem_limit_bytes=vmem_limit_bytes)


def _const_spec(shape):
    nd = len(shape)
    return pl.BlockSpec(shape, lambda *_: (0,) * nd, pipeline_mode=pl.Buffered(1))


def _dot(a, b):
    return jnp.dot(a, b, preferred_element_type=F32)


def _split2(x):
    hi = x.astype(BF16)
    lo = (x - hi.astype(F32)).astype(BF16)
    return hi, lo


def _layer_norm(z, g, b):
    mu = jnp.mean(z, axis=-1, keepdims=True)
    zc = z - mu
    var = jnp.mean(zc * zc, axis=-1, keepdims=True)
    return zc * lax.rsqrt(var + LN_EPS) * g + b


def _shift_rows(x, first_row):
    xs = pltpu.roll(x, 1, axis=0)
    row = lax.broadcasted_iota(jnp.int32, x.shape, 0)
    return jnp.where(row == 0, first_row, xs)


def _inproj_kernel(x_ref, xprev_ref, w_in_ref, mu_ref, wA_ref, aA_ref, gA_ref, wB_ref, aB_ref, gB_ref,
                   w0_ref, a0_ref, urkv_ref, hglu_ref, dlog_ref, asig_ref, g_ref, *, seq_tiles, halo):
    x = x_ref[...]
    if halo:
        first = (pl.program_id(0) % seq_tiles) == 0
        prev_row = jnp.where(first, 0.0, xprev_ref[SUBLANES - 1:SUBLANES, :])
        xprev = _shift_rows(x, prev_row)
    else:
        xprev = xprev_ref[...]
    xx = xprev - x
    xb = x.astype(BF16)
    d3 = urkv_ref.shape[1]
    dc = hglu_ref.shape[1]
    urkv_ref[...] = _dot(xb, w_in_ref[:, :d3])
    val = _dot(xb, w_in_ref[:, d3:d3 + dc])
    gate = _dot(xb, w_in_ref[:, d3 + dc:])
    hglu_ref[...] = val * jax.nn.sigmoid(gate)

    xw = (x + xx * mu_ref[0:1, :]).astype(BF16)
    tw = jnp.tanh(_dot(xw, wA_ref[...]))
    wlin = w0_ref[...] + _dot(tw.astype(BF16), wB_ref[...])
    wlog = -jax.nn.softplus(-wlin) - 0.5
    dlog_ref[...] = -jnp.exp(wlog)

    xa = (x + xx * mu_ref[1:2, :]).astype(BF16)
    ta = _dot(xa, aA_ref[...])
    asig_ref[...] = jax.nn.sigmoid(a0_ref[...] + _dot(ta.astype(BF16), aB_ref[...]))

    xg = (x + xx * mu_ref[2:3, :]).astype(BF16)
    tg = jax.nn.sigmoid(_dot(xg, gA_ref[...]))
    g_ref[...] = _dot(tg.astype(BF16), gB_ref[...])


def _inproj(x, xprev, seq_len, lw):
    n, d = x.shape
    d_in = lw["w_in"].shape[1]
    dr = lw["w_B"].shape[1]
    d3 = 3 * dr
    dc = (d_in - d3) // 2
    halo = xprev is None
    tm = _row_tile(seq_len if halo else n, 256)
    seq_tiles = (seq_len // tm) if halo else 1
    if halo:
        prev_spec = pl.BlockSpec((SUBLANES, d), lambda i: (jnp.maximum(i * (tm // SUBLANES) - 1, 0), 0))
        prev_arg = x
    else:
        prev_spec = pl.BlockSpec((tm, d), lambda i: (i, 0))
        prev_arg = xprev
    row = lambda c: pl.BlockSpec((tm, c), lambda i: (i, 0))
    outs = [jax.ShapeDtypeStruct((n, c), F32) for c in (d3, dc, dr, dr, dr)]
    consts = [lw["w_in"], lw["mu_x"], lw["w_A"], lw["a_A"], lw["g_A"], lw["w_B"], lw["a_B"], lw["g_B"],
              lw["w0"], lw["a0"]]
    return pl.pallas_call(
        functools.partial(_inproj_kernel, seq_tiles=seq_tiles, halo=halo),
        grid=(n // tm,),
        in_specs=[row(d), prev_spec] + [_const_spec(c.shape) for c in consts],
        out_specs=[row(c) for c in (d3, dc, dr, dr, dr)],
        out_shape=outs,
        compiler_params=_params("arbitrary"),
        name="inproj",
    )(x, prev_arg, *consts)


def _head_mask(n, dtype):
    r = lax.broadcasted_iota(jnp.int32, (n, n), 0) // RWKV_HEAD
    c = lax.broadcasted_iota(jnp.int32, (n, n), 1) // RWKV_HEAD
    return (r == c).astype(dtype)


def _head_sum(x, ones_bd):
    cols = []
    for c0 in range(0, x.shape[1], MXU_DIM):
        hi, lo = _split2(x[:, c0:c0 + MXU_DIM])
        cols.append(_dot(hi, ones_bd) + _dot(lo, ones_bd))
    return cols[0] if len(cols) == 1 else jnp.concatenate(cols, axis=1)


def _scanprep_kernel(u_ref, uprev_ref, mu_ref, kk_ref, ka_ref, rk_ref, asig_ref,
                     r_ref, k_ref, v_ref, a_ref, b_ref, bonus_ref, *, seq_tiles, halo):
    u = u_ref[...]
    if halo:
        first = (pl.program_id(0) % seq_tiles) == 0
        prev_row = jnp.where(first, 0.0, uprev_ref[SUBLANES - 1:SUBLANES, :])
        uprev = _shift_rows(u, prev_row)
    else:
        uprev = uprev_ref[...]
    rkv = u + (uprev - u) * mu_ref[...]
    dr = r_ref.shape[1]
    r = rkv[:, :dr]
    k = rkv[:, dr:2 * dr]
    v = rkv[:, 2 * dr:]
    asig = asig_ref[...]
    ones_bd = _head_mask(MXU_DIM, BF16)
    kk = k * kk_ref[...]
    kk = kk * lax.rsqrt(jnp.maximum(_head_sum(kk * kk, ones_bd), 1e-24))
    k = k * (1.0 + (asig - 1.0) * ka_ref[...])
    r_ref[...] = r
    k_ref[...] = k
    v_ref[...] = v
    a_ref[...] = -kk
    b_ref[...] = kk * asig
    bonus_ref[...] = _head_sum(r * k * rk_ref[...], ones_bd) * v


def _scanprep(u_rkv, uprev, seq_len, asig, lw):
    n, d3 = u_rkv.shape
    dr = d3 // 3
    halo = uprev is None
    tm = _row_tile(seq_len if halo else n, 256)
    seq_tiles = (seq_len // tm) if halo else 1
    if halo:
        prev_spec = pl.BlockSpec((SUBLANES, d3), lambda i: (jnp.maximum(i * (tm // SUBLANES) - 1, 0), 0))
        prev_arg = u_rkv
    else:
        prev_spec = pl.BlockSpec((tm, d3), lambda i: (i, 0))
        prev_arg = uprev
    row = lambda c: pl.BlockSpec((tm, c), lambda i: (i, 0))
    consts = [lw["mu_rkv"], lw["k_k"], lw["k_a"], lw["r_k"]]
    return pl.pallas_call(
        functools.partial(_scanprep_kernel, seq_tiles=seq_tiles, halo=halo),
        grid=(n // tm,),
        in_specs=[row(d3), prev_spec] + [_const_spec(c.shape) for c in consts] + [row(dr)],
        out_specs=[row(dr)] * 6,
        out_shape=[jax.ShapeDtypeStruct((n, dr), F32)] * 6,
        compiler_params=_params("arbitrary"),
        name="scanprep",
    )(u_rkv, prev_arg, *consts, asig)


def _blockdiag(x, mask):
    reps = mask.shape[0] // x.shape[0]
    return jnp.concatenate([x] * reps, axis=0) * mask


def _dot_nt(a, b):
    return lax.dot_general(a, b, (((1,), (1,)), ((), ())), preferred_element_type=F32)


def _dot_tn(a, b):
    return lax.dot_general(a, b, (((0,), (0,)), ((), ())), preferred_element_type=F32)


def _scan_kernel(r_ref, dl_ref, k_ref, v_ref, a_ref, b_ref, y_ref, s_ref, h_scr):
    c = pl.program_id(0)
    nb = r_ref.shape[0]
    C = r_ref.shape[1]
    W = MXU_DIM
    ng = r_ref.shape[2] // W

    @pl.when(c == 0)
    def _():
        h_scr[...] = jnp.zeros_like(h_scr)

    ti = lax.broadcasted_iota(jnp.int32, (C, C), 0)
    tj = lax.broadcasted_iota(jnp.int32, (C, C), 1)
    tri = (ti >= tj).astype(BF16)
    t_row = lax.broadcasted_iota(jnp.int32, (C, W), 0)
    j_col = lax.broadcasted_iota(jnp.int32, (C, W), 1) % C
    strict = j_col < t_row
    incl = j_col <= t_row
    eye_cat = (j_col == t_row).astype(F32)
    rb = lax.broadcasted_iota(jnp.int32, (HEADS_PER_GROUP * C, W), 0) // C
    cb = lax.broadcasted_iota(jnp.int32, (HEADS_PER_GROUP * C, W), 1) // RWKV_HEAD
    bmask = (rb == cb).astype(BF16)
    hmask = _head_mask(W, F32)

    chains = [(bi, gi) for bi in range(nb) for gi in range(ng)]
    each = lambda f, *lists: [f(*xs) for xs in zip(*lists)]
    load = lambda ref: [ref[bi, :, gi * W:(gi + 1) * W] for bi, gi in chains]
    r, dl, k, v, a, b = (load(ref) for ref in (r_ref, dl_ref, k_ref, v_ref, a_ref, b_ref))
    sbd = [h_scr[bi, gi] for bi, gi in chains]
    bd = lambda x: _blockdiag(x, bmask)

    def cumsum(d):
        d_hi = d.astype(BF16)
        d_r1 = d - d_hi.astype(F32)
        d_mid = d_r1.astype(BF16)
        d_lo = (d_r1 - d_mid.astype(F32)).astype(BF16)
        return _dot(tri, d_hi) + (_dot(tri, d_mid) + _dot(tri, d_lo))

    cum = each(cumsum, dl)
    cum_last = each(lambda x: x[C - 1:C, :], cum)
    e_neg = each(lambda x: jnp.exp(-x), cum)
    e_end = each(lambda x, xl: jnp.exp(xl - x), cum, cum_last)
    at = each(lambda x, cu, d: (x * jnp.exp(cu - d)).astype(BF16), a, cum, dl)
    rt = each(lambda x, cu: (x * jnp.exp(cu)).astype(BF16), r, cum)
    bt = each(lambda x, e: (x * e).astype(BF16), b, e_neg)
    kt = each(lambda x, e: (x * e).astype(BF16), k, e_neg)
    bh = each(lambda x, e: (x * e).astype(BF16), b, e_end)
    kh = each(lambda x, e: (x * e).astype(BF16), k, e_end)
    vb = each(lambda x: x.astype(BF16), v)
    ar = each(lambda x, y: jnp.concatenate([x, y], axis=0), at, rt)

    p_b = each(lambda x, y: _dot_nt(x, bd(y)), ar, bt)
    p_k = each(lambda x, y: _dot_nt(x, bd(y)), ar, kt)
    arh = each(lambda x, s: _dot_nt(x, s.astype(BF16)), ar, sbd)
    p_ab = each(lambda p: jnp.where(strict, p[:C], 0.0), p_b)
    p_rb = each(lambda p: jnp.where(incl, p[C:], 0.0).astype(BF16), p_b)
    p_akrk = each(lambda p: jnp.concatenate([jnp.where(strict, p[:C], 0.0), jnp.where(incl, p[C:], 0.0)],
                                            axis=0).astype(BF16), p_k)

    pv = each(lambda p, x: _dot(p, bd(x)), p_akrk, vb)
    n_sq = int(math.log2(C))
    nn_b = each(lambda p: p.astype(BF16), p_ab)
    nn_b = each(lambda n: _dot(n, bd(n)).astype(BF16), nn_b)
    w = each(lambda x, y: (x[:C] + y[:C]).astype(BF16), arh, pv)
    tm = each(lambda p: eye_cat + p, p_ab)
    for i in range(1, n_sq):
        last = i == n_sq - 1
        lhs = each(lambda t, n: t.astype(BF16) if last else jnp.concatenate([t.astype(BF16), n], axis=0), tm, nn_b)
        prod = each(lambda l, n: _dot(l, bd(n)), lhs, nn_b)
        tm = each(lambda t, p: t + p[:C], tm, prod)
        if not last:
            nn_b = each(lambda p: p[C:].astype(BF16), prod)
    ub = each(lambda t, x: _dot(t.astype(BF16), bd(x)).astype(BF16), tm, w)

    yv = each(lambda x, y, p, u: x[C:] + y[C:] + _dot(p, bd(u)), arh, pv, p_rb, ub)
    upd = each(lambda u, x, y, z: _dot_tn(jnp.concatenate([u, x], axis=0), jnp.concatenate([y, z], axis=0)),
               ub, vb, bh, kh)
    for (bi, gi), y, s, xl, up in zip(chains, yv, sbd, cum_last, upd):
        y_ref[bi, :, gi * W:(gi + 1) * W] = y
        h_scr[bi, gi] = (s * jnp.exp(xl) + up) * hmask

    @pl.when(c == pl.num_programs(0) - 1)
    def _():
        s_ref[...] = h_scr[...]


def _scan(r, dl, k, v, a, b):
    nb, t, dr = r.shape
    ng = dr // MXU_DIM
    C = SCAN_CHUNK
    spec = pl.BlockSpec((nb, C, dr), lambda c: (0, c, 0))
    return pl.pallas_call(
        _scan_kernel,
        grid=(t // C,),
        in_specs=[spec] * 6,
        out_specs=[spec, pl.BlockSpec((nb, ng, MXU_DIM, MXU_DIM), lambda c: (0, 0, 0, 0))],
        out_shape=[jax.ShapeDtypeStruct((nb, t, dr), F32),
                   jax.ShapeDtypeStruct((nb, ng, MXU_DIM, MXU_DIM), F32)],
        scratch_shapes=[pltpu.VMEM((nb, ng, MXU_DIM, MXU_DIM), F32)],
        compiler_params=_params("arbitrary"),
        name="wkv_scan",
    )(r, dl, k, v, a, b)


def _step_kernel(s_ref, r_ref, dl_ref, k_ref, v_ref, a_ref, b_ref, sout_ref, y_ref):
    S = s_ref[...]
    n = S.shape[-1]
    eye = (lax.broadcasted_iota(jnp.int32, (n, n), 0) == lax.broadcasted_iota(jnp.int32, (n, n), 1)).astype(F32)
    row = lambda ref: ref[...][:, :, None, :]
    sa = jnp.sum(S * row(a_ref), axis=-1, keepdims=True)
    v_col = jnp.sum(eye * row(v_ref), axis=-1, keepdims=True)
    s_new = S * jnp.exp(row(dl_ref)) + sa * row(b_ref) + v_col * row(k_ref)
    sout_ref[...] = s_new
    y_col = jnp.sum(s_new * row(r_ref), axis=-1, keepdims=True)
    y_ref[...] = jnp.sum(eye * y_col, axis=-2)


def _step(S0, r, dl, k, v, a, b):
    nb, nh, n, _ = S0.shape
    bb = _row_tile(nb, 8)
    vec = lambda x: x.reshape(nb, nh, n)
    sspec = pl.BlockSpec((bb, nh, n, n), lambda i: (i, 0, 0, 0))
    vspec = pl.BlockSpec((bb, nh, n), lambda i: (i, 0, 0))
    s_new, y = pl.pallas_call(
        _step_kernel,
        grid=(nb // bb,),
        in_specs=[sspec] + [vspec] * 6,
        out_specs=[sspec, vspec],
        out_shape=[jax.ShapeDtypeStruct(S0.shape, F32), jax.ShapeDtypeStruct((nb, nh, n), F32)],
        compiler_params=_params("arbitrary"),
        name="wkv_step",
    )(S0, *(vec(x) for x in (r, dl, k, v, a, b)))
    return s_new, y.reshape(nb, nh * n)


def _post_kernel(y_ref, bonus_ref, g_ref, gng_ref, gnb_ref, o_ref):
    y = y_ref[...]
    ones_bd = _head_mask(MXU_DIM, BF16)
    inv_n = 1.0 / RWKV_HEAD
    mu = _head_sum(y, ones_bd) * inv_n
    yc = y - mu
    var = _head_sum(yc * yc, ones_bd) * inv_n
    yn = yc * lax.rsqrt(var + GN_EPS) * gng_ref[...] + gnb_ref[...]
    o_ref[...] = (yn + bonus_ref[...]) * g_ref[...]


def _post(y, bonus, g, lw):
    n, dr = y.shape
    tm = _row_tile(n, 512)
    row = pl.BlockSpec((tm, dr), lambda i: (i, 0))
    return pl.pallas_call(
        _post_kernel,
        grid=(n // tm,),
        in_specs=[row, row, row, _const_spec(lw["gn_g"].shape), _const_spec(lw["gn_b"].shape)],
        out_specs=row,
        out_shape=jax.ShapeDtypeStruct((n, dr), F32),
        compiler_params=_params("arbitrary"),
        name="wkv_post",
    )(y, bonus, g, lw["gn_g"], lw["gn_b"])


def _conv_seq_kernel(h_ref, halo_ref, w_ref, b_ref, lng_ref, lnb_ref, o_ref, hp_scr, *, seq_tiles):
    tt = h_ref.shape[0]
    first = (pl.program_id(0) % seq_tiles) == 0
    hp_scr[0:CONV_HALO, :] = jnp.where(first, 0.0, halo_ref[...])
    hp_scr[CONV_HALO:, :] = h_ref[...]
    off = CONV_HALO - (CONV_WIDTH - 1)
    rc = CONV_ROW_CHUNK if tt % CONV_ROW_CHUNK == 0 else tt
    for c0 in range(0, o_ref.shape[1], LANES):
        lanes = slice(c0, c0 + LANES)
        for t0 in range(0, tt, rc):
            acc = jnp.broadcast_to(b_ref[:, lanes], (rc, LANES))
            for p in range(SUBLANES):
                rows = rc if p == 0 else rc + SUBLANES
                part = None
                for m in range((CONV_HALO + SUBLANES) // SUBLANES):
                    j = SUBLANES * m + p - off
                    if 0 <= j < CONV_WIDTH:
                        term = hp_scr[t0 + SUBLANES * m:t0 + SUBLANES * m + rows, lanes] * w_ref[j:j + 1, lanes]
                        part = term if part is None else part + term
                acc = acc + part[p:p + rc]
            o_ref[t0:t0 + rc, lanes] = acc
    z = _layer_norm(o_ref[...], lng_ref[...], lnb_ref[...])
    o_ref[...] = z * jax.nn.sigmoid(z)


def _conv_seq(h, seq_len, lw):
    n, dc = h.shape
    tt = _row_tile(seq_len, 256)
    assert tt % CONV_HALO == 0
    seq_tiles = seq_len // tt
    row = pl.BlockSpec((tt, dc), lambda i: (i, 0))
    halo = pl.BlockSpec((CONV_HALO, dc), lambda i: (jnp.maximum(i * (tt // CONV_HALO) - 1, 0), 0))
    consts = [lw["conv_w"], lw["conv_b"], lw["conv_ln_g"], lw["conv_ln_b"]]
    return pl.pallas_call(
        functools.partial(_conv_seq_kernel, seq_tiles=seq_tiles),
        grid=(n // tt,),
        in_specs=[row, halo] + [_const_spec(c.shape) for c in consts],
        out_specs=row,
        out_shape=jax.ShapeDtypeStruct((n, dc), F32),
        scratch_shapes=[pltpu.VMEM((CONV_HALO + tt, dc), F32)],
        compiler_params=_params("arbitrary"),
        name="conv_seq",
    )(h, h, *consts)


def _conv_step_kernel(buf_ref, h_ref, w_ref, b_ref, lng_ref, lnb_ref, o_ref, nbuf_ref):
    buf = buf_ref[...]
    h = h_ref[...]
    acc = b_ref[...] + h * w_ref[CONV_WIDTH - 1:CONV_WIDTH, :]
    for j in range(CONV_WIDTH - 1):
        acc = acc + buf[:, j, :] * w_ref[j:j + 1, :]
    z = _layer_norm(acc, lng_ref[...], lnb_ref[...])
    o_ref[...] = z * jax.nn.sigmoid(z)
    nbuf_ref[:, 0:CONV_WIDTH - 2, :] = buf[:, 1:, :]
    nbuf_ref[:, CONV_WIDTH - 2, :] = h


def _conv_step(buf, h, lw):
    nb, wm1, dc = buf.shape
    bb = _row_tile(nb, 32)
    bspec = pl.BlockSpec((bb, wm1, dc), lambda i: (i, 0, 0))
    row = pl.BlockSpec((bb, dc), lambda i: (i, 0))
    consts = [lw["conv_w"], lw["conv_b"], lw["conv_ln_g"], lw["conv_ln_b"]]
    return pl.pallas_call(
        _conv_step_kernel,
        grid=(nb // bb,),
        in_specs=[bspec, row] + [_const_spec(c.shape) for c in consts],
        out_specs=[row, bspec],
        out_shape=[jax.ShapeDtypeStruct((nb, dc), F32), jax.ShapeDtypeStruct(buf.shape, F32)],
        compiler_params=_params("arbitrary"),
        name="conv_step",
    )(buf, h, *consts)


def _outproj_kernel(*refs, alpha, n_dst):
    x_ref, yr_ref, yc_ref, wo_ref, g_ref, b_ref, wr_ref, br_ref, h_ref, logit_ref = refs[n_dst:]
    ymix = jnp.concatenate([yr_ref[...].astype(BF16), yc_ref[...].astype(BF16)], axis=1)
    h = _layer_norm(alpha * x_ref[...] + _dot(ymix, wo_ref[...]), g_ref[...], b_ref[...])
    h_ref[...] = h
    logit_ref[...] = _dot(h.astype(BF16), wr_ref[...]) + br_ref[...]


def _outproj(x, y_rwkv, y_conv, lw, alpha, n_total, row0, dst=()):
    n, d = x.shape
    dr = y_rwkv.shape[1]
    dc = y_conv.shape[1]
    ne = lw["w_router"].shape[1]
    tm = _row_tile(math.gcd(n, row0) if row0 else n, 256)
    blk0 = row0 // tm
    row = lambda c: pl.BlockSpec((tm, c), lambda i: (i, 0))
    out_row = lambda c: pl.BlockSpec((tm, c), lambda i: (blk0 + i, 0))
    consts = [lw["w_out"], lw["ln1_g"], lw["ln1_b"], lw["w_router"], lw["b_router"]]
    return pl.pallas_call(
        functools.partial(_outproj_kernel, alpha=alpha, n_dst=len(dst)),
        grid=(n // tm,),
        in_specs=[pl.BlockSpec(memory_space=pl.ANY)] * len(dst) + [row(d), row(dr), row(dc)]
        + [_const_spec(c.shape) for c in consts],
        out_specs=[out_row(d), out_row(ne)],
        out_shape=[jax.ShapeDtypeStruct((n_total, d), F32), jax.ShapeDtypeStruct((n_total, ne), F32)],
        input_output_aliases={i: i for i in range(len(dst))},
        compiler_params=_params("arbitrary"),
        name="outproj",
    )(*dst, x, y_rwkv, y_conv, *consts)


def _route(logits, moe_tm):
    n_tok, ne = logits.shape
    top_val, top_idx = lax.top_k(logits, TOP_K)
    gates = jax.nn.softmax(top_val, axis=-1)
    n_assign = n_tok * TOP_K
    e_flat = top_idx.reshape(n_assign).astype(jnp.int32)
    onehot = (e_flat[:, None] == jnp.arange(ne, dtype=jnp.int32)[None, :]).astype(jnp.int32)
    counts = onehot.sum(0)
    rank = jnp.take_along_axis(jnp.cumsum(onehot, axis=0), e_flat[:, None], axis=1)[:, 0] - 1
    nblk_e = (counts + moe_tm - 1) // moe_tm
    blk_end = jnp.cumsum(nblk_e)
    blk_start = blk_end - nblk_e
    n_active = blk_end[-1]
    nb_max = n_assign // moe_tm + ne
    pos = blk_start[e_flat] * moe_tm + rank
    sorted_tok = (jnp.argsort(e_flat, stable=True) // TOP_K).astype(jnp.int32)
    sorted_tok = jnp.concatenate([sorted_tok, jnp.zeros((SUBLANES,), jnp.int32)])
    cstart = jnp.cumsum(counts) - counts
    q = jnp.arange(nb_max, dtype=jnp.int32)
    q_eff = jnp.minimum(q, n_active - 1)
    blk_e = jnp.minimum(jnp.searchsorted(blk_end, q_eff, side="right"), ne - 1).astype(jnp.int32)
    row_in_e = (q_eff - blk_start[blk_e]) * moe_tm
    valid = jnp.clip(counts[blk_e] - row_in_e, 0, moe_tm)
    valid = jnp.where(q < n_active, valid, 0).astype(jnp.int32)
    src_start = (cstart[blk_e] + row_in_e).astype(jnp.int32)
    return gates, pos.astype(jnp.int32), sorted_tok, src_start, blk_e, q_eff.astype(jnp.int32), valid


def _gather_kernel(qeff_ref, valid_ref, src_ref, tok_ref, h_hbm, o_ref, buf, sem):
    q = pl.program_id(0)
    nvalid = valid_ref[q]

    def row_copy(r, tok):
        return pltpu.make_async_copy(h_hbm.at[pl.ds(tok, 1)], buf.at[pl.ds(r, 1)], sem)

    @pl.when(nvalid > 0)
    def _():
        base = src_ref[q]
        ngroups = (nvalid + SUBLANES - 1) // SUBLANES

        def issue(g, c):
            r0 = pl.multiple_of(g * SUBLANES, SUBLANES)
            for i in range(SUBLANES):
                row_copy(r0 + i, tok_ref[base + r0 + i]).start()
            return c

        lax.fori_loop(0, ngroups, issue, 0)

        def wait(g, c):
            r0 = pl.multiple_of(g * SUBLANES, SUBLANES)
            pltpu.make_async_copy(h_hbm.at[pl.ds(0, SUBLANES)], buf.at[pl.ds(r0, SUBLANES)], sem).wait()
            return c

        lax.fori_loop(0, ngroups, wait, 0)
        rows = lax.broadcasted_iota(jnp.int32, buf.shape, 0)
        o_ref[...] = jnp.where(rows < nvalid, buf[...], 0.0).astype(o_ref.dtype)


def _gather_rows(h, sorted_tok, src_start, q_eff, valid, moe_tm):
    n_tok, d = h.shape
    nb_max = q_eff.shape[0]
    grid_spec = pltpu.PrefetchScalarGridSpec(
        num_scalar_prefetch=4,
        grid=(nb_max,),
        in_specs=[pl.BlockSpec(memory_space=pl.ANY)],
        out_specs=pl.BlockSpec((moe_tm, d), lambda q, qe, va, sr, tk: (qe[q], 0)),
        scratch_shapes=[pltpu.VMEM((moe_tm, d), F32), pltpu.SemaphoreType.DMA(())],
    )
    return pl.pallas_call(
        _gather_kernel,
        grid_spec=grid_spec,
        out_shape=jax.ShapeDtypeStruct((nb_max * moe_tm, d), BF16),
        compiler_params=_params("arbitrary"),
        name="moe_gather",
    )(q_eff, valid, src_start, sorted_tok, h)


def _moe_kernel(be_ref, qeff_ref, valid_ref, x_ref, wg_ref, wl_ref, bg_ref, bl_ref, wd_ref, bd_ref, o_ref,
                wg_s, wl_s, wd_s, *, sub):
    q = pl.program_id(0)
    j = pl.program_id(1)
    nvalid = valid_ref[q]
    unit = sub // 2
    n_units = (nvalid + unit - 1) // unit
    n_all = o_ref.shape[0] // unit

    def fill(s, value):
        r0 = pl.multiple_of(s * unit, unit)
        o_ref[pl.ds(r0, unit), :] = jnp.broadcast_to(value, (unit, o_ref.shape[1]))

    def rows_block(r0, rows):
        x = x_ref[pl.ds(r0, rows), :]
        g = jnp.minimum(_dot(x, wg_s[...]) + bg_ref[0], SWIGLU_LIMIT)
        l = jnp.clip(_dot(x, wl_s[...]) + bl_ref[0], -SWIGLU_LIMIT, SWIGLU_LIMIT)
        act = g * jax.nn.sigmoid(SWIGLU_ALPHA * g) * (l + 1.0)
        o_ref[pl.ds(r0, rows), :] += _dot(act.astype(BF16), wd_s[...])

    @pl.when(nvalid > 0)
    def _():
        @pl.when(j == 0)
        def _():
            lax.fori_loop(0, n_units, lambda s, c: (fill(s, bd_ref[0]), c)[1], 0)
            lax.fori_loop(n_units, n_all, lambda s, c: (fill(s, jnp.zeros((1, 1), F32)), c)[1], 0)

        wg_s[...] = wg_ref[0].astype(BF16)
        wl_s[...] = wl_ref[0].astype(BF16)
        wd_s[...] = wd_ref[0].astype(BF16)
        rows_block(0, unit)
        rest = n_units - 1
        n_pairs = rest // 4
        tail = rest - n_pairs * 4

        def pair(i, c):
            r0 = pl.multiple_of(unit + i * 2 * sub, unit)
            rows_block(r0, sub)
            rows_block(r0 + sub, sub)
            return c

        lax.fori_loop(0, n_pairs, pair, 0)
        t0 = pl.multiple_of(unit + n_pairs * 2 * sub, unit)

        @pl.when(tail >= 2)
        def _():
            rows_block(t0, sub)

        @pl.when(tail % 2 == 1)
        def _():
            rows_block(pl.multiple_of(t0 + (tail // 2) * sub, unit), unit)


def _moe_experts(x_sorted, blk_e, q_eff, valid, lw, moe_tm, sub, tf):
    n_rows, d = x_sorted.shape
    ne, _, f2 = lw["w_gu"].shape
    f = f2 // 2
    nf = f // tf
    nb_max = q_eff.shape[0]
    b_gu = lw["b_gu"].reshape(ne, 1, f2)
    b_down = lw["b_down"].reshape(ne, 1, d)

    def jf(q, j, va):
        return jnp.where(va[q] > 0, j, nf - 1)

    grid_spec = pltpu.PrefetchScalarGridSpec(
        num_scalar_prefetch=3,
        grid=(nb_max, nf),
        in_specs=[
            pl.BlockSpec((moe_tm, d), lambda q, j, be, qe, va: (qe[q], 0)),
            pl.BlockSpec((1, d, tf), lambda q, j, be, qe, va: (be[q], 0, jf(q, j, va))),
            pl.BlockSpec((1, d, tf), lambda q, j, be, qe, va: (be[q], 0, nf + jf(q, j, va))),
            pl.BlockSpec((1, 1, tf), lambda q, j, be, qe, va: (be[q], 0, jf(q, j, va))),
            pl.BlockSpec((1, 1, tf), lambda q, j, be, qe, va: (be[q], 0, nf + jf(q, j, va))),
            pl.BlockSpec((1, tf, d), lambda q, j, be, qe, va: (be[q], jf(q, j, va), 0)),
            pl.BlockSpec((1, 1, d), lambda q, j, be, qe, va: (be[q], 0, 0)),
        ],
        out_specs=pl.BlockSpec((moe_tm, d), lambda q, j, be, qe, va: (qe[q], 0)),
        scratch_shapes=[pltpu.VMEM((d, tf), BF16), pltpu.VMEM((d, tf), BF16), pltpu.VMEM((tf, d), BF16)],
    )
    return pl.pallas_call(
        functools.partial(_moe_kernel, sub=sub),
        grid_spec=grid_spec,
        out_shape=jax.ShapeDtypeStruct((n_rows, d), F32),
        compiler_params=_params("arbitrary", "arbitrary", vmem_limit_bytes=MOE_VMEM_LIMIT_BYTES),
        name="moe_experts",
    )(blk_e, q_eff, valid, x_sorted, lw["w_gu"], lw["w_gu"], b_gu, b_gu, lw["w_down"], b_down)


def _combine_kernel(pos_ref, h_ref, gates_ref, yrows_hbm, g_ref, b_ref, op_ref, os_ref, buf, sem, *, alpha, n_first):
    i = pl.program_id(0)
    tm = h_ref.shape[0]

    def row_copy(r, k, p):
        return pltpu.make_async_copy(yrows_hbm.at[pl.ds(p, 1)], buf.at[k, pl.ds(r, 1)], sem)

    def issue(r2, c):
        for dr in range(2):
            r = r2 * 2 + dr
            for k in range(TOP_K):
                row_copy(r, k, pos_ref[(i * tm + r) * TOP_K + k]).start()
        return c

    lax.fori_loop(0, tm // 2, issue, 0)
    for k in range(TOP_K):
        pltpu.make_async_copy(yrows_hbm.at[pl.ds(0, tm)], buf.at[k], sem).wait()
    gates = gates_ref[...]
    ffn = buf[0] * gates[:, 0:1]
    for k in range(1, TOP_K):
        ffn = ffn + buf[k] * gates[:, k:k + 1]
    y = _layer_norm(alpha * h_ref[...] + ffn, g_ref[...], b_ref[...])

    @pl.when(i < n_first)
    def _():
        op_ref[...] = y

    @pl.when(i >= n_first)
    def _():
        os_ref[...] = y


def _combine(h, gates, pos, y_rows, lw, alpha, n_prompt):
    n_tok, d = h.shape
    tm = _row_tile(math.gcd(n_prompt, n_tok - n_prompt), 128)
    n_first = n_prompt // tm
    grid_spec = pltpu.PrefetchScalarGridSpec(
        num_scalar_prefetch=1,
        grid=(n_tok // tm,),
        in_specs=[
            pl.BlockSpec((tm, d), lambda i, p: (i, 0)),
            pl.BlockSpec((tm, TOP_K), lambda i, p: (i, 0)),
            pl.BlockSpec(memory_space=pl.ANY),
            pl.BlockSpec((1, d), lambda i, p: (0, 0)),
            pl.BlockSpec((1, d), lambda i, p: (0, 0)),
        ],
        out_specs=[pl.BlockSpec((tm, d), lambda i, p: (jnp.minimum(i, n_first - 1), 0)),
                   pl.BlockSpec((tm, d), lambda i, p: (jnp.maximum(i - n_first, 0), 0))],
        scratch_shapes=[pltpu.VMEM((TOP_K, tm, d), F32), pltpu.SemaphoreType.DMA(())],
    )
    return pl.pallas_call(
        functools.partial(_combine_kernel, alpha=alpha, n_first=n_first),
        grid_spec=grid_spec,
        out_shape=[jax.ShapeDtypeStruct((n_prompt, d), F32), jax.ShapeDtypeStruct((n_tok - n_prompt, d), F32)],
        compiler_params=_params("arbitrary"),
        name="moe_combine",
    )(pos, h, gates, y_rows, lw["ln2_g"], lw["ln2_b"])


def _moe_tiles(n_assign, ne, d, f):
    unit = MOE_SUB // 2
    mean = -(-n_assign // ne)
    tm = max(MOE_SUB, -(-(mean * 11 // 10) // unit) * unit)
    for tf in (512, 256):
        tf = min(tf, f)
        blocks = 2 * tm * d * 2 + 2 * tm * d * 4
        weights = 3 * d * tf * (2 * 4 + 2)
        temps = MOE_SUB * 2 * tf * 4
        if blocks + weights + temps <= MOE_VMEM_LIMIT_BYTES:
            return tm, tf
    return min(tm, 4 * MOE_SUB), min(256, f)


_VEC_PARAMS = ("mu_rkv", "w0", "a0", "k_k", "k_a", "r_k", "gn_g", "gn_b", "conv_b", "conv_ln_g", "conv_ln_b",
               "ln1_g", "ln1_b", "b_router", "ln2_g", "ln2_b")
_BF16_PARAMS = ("w_in", "w_A", "w_B", "a_A", "a_B", "g_A", "g_B", "w_out", "w_router")


def _diag_blocks(s):
    nb, ng = s.shape[:2]
    s6 = s.reshape(nb, ng, HEADS_PER_GROUP, RWKV_HEAD, HEADS_PER_GROUP, RWKV_HEAD)
    d = jnp.stack([s6[:, :, h, :, h, :] for h in range(HEADS_PER_GROUP)], axis=2)
    return d.reshape(nb, ng * HEADS_PER_GROUP, RWKV_HEAD, RWKV_HEAD)


def _layer(xp, xs, sx, srkv, swkv, sconv, lw, alpha):
    nbp, t, d = xp.shape
    nbs = xs.shape[0]
    xp2 = xp.reshape(nbp * t, d)
    xs2 = xs.reshape(nbs, d)

    up, hgp, dlp, asp, gp = _inproj(xp2, None, t, lw)
    us, hgs, dls, ass, gs = _inproj(xs2, sx, 1, lw)
    dr = dlp.shape[1]
    rp, kp, vp, ap, bp, bonp = _scanprep(up, None, t, asp, lw)
    rs, ks, vs, as_, bs, bons = _scanprep(us, srkv, 1, ass, lw)

    seq = lambda z: z.reshape(nbp, t, dr)
    yp_raw, s_end = _scan(seq(rp), seq(dlp), seq(kp), seq(vp), seq(ap), seq(bp))
    s_new, ys_raw = _step(swkv, rs, dls, ks, vs, as_, bs)
    yrp = _post(yp_raw.reshape(nbp * t, dr), bonp, gp, lw)
    yrs = _post(ys_raw, bons, gs, lw)

    ycp = _conv_seq(hgp, t, lw)
    ycs, nbuf = _conv_step(sconv, hgs, lw)

    n_prompt = nbp * t
    n_tok = n_prompt + nbs
    dst = _outproj(xp2, yrp, ycp, lw, alpha, n_tok, 0)
    h_all, logits = _outproj(xs2, yrs, ycs, lw, alpha, n_tok, n_prompt, dst=tuple(dst))

    ne = logits.shape[1]
    moe_tm, moe_tf = _moe_tiles(n_tok * TOP_K, ne, d, lw["w_down"].shape[1])
    gates, pos, sorted_tok, src_start, blk_e, q_eff, valid = _route(logits, moe_tm)
    x_sorted = _gather_rows(h_all, sorted_tok, src_start, q_eff, valid, moe_tm)
    y_rows = _moe_experts(x_sorted, blk_e, q_eff, valid, lw, moe_tm, MOE_SUB, moe_tf)
    yp, ys = _combine(h_all, gates, pos, y_rows, lw, alpha, n_prompt)
    yp = yp.reshape(nbp, t, d)
    ys = ys.reshape(nbs, 1, d)
    p_state = (xp[:, -1], up.reshape(nbp, t, -1)[:, -1], _diag_blocks(s_end),
               hgp.reshape(nbp, t, -1)[:, t - (CONV_WIDTH - 1):])
    s_state = (xs2, us, s_new, nbuf)
    return yp, ys, p_state, s_state


def kernel(x_prompt, x_sample, state_shift_x, state_shift_rkv, state_wkv, state_conv, w_in, mu_x, mu_rkv, w0, w_A,
           w_B, a0, a_A, a_B, g_A, g_B, k_k, k_a, r_k, gn_g, gn_b, conv_w, conv_b, conv_ln_g, conv_ln_b, w_out,
           ln1_g, ln1_b, w_router, b_router, w_gu, b_gu, w_down, b_down, ln2_g, ln2_b):
    params = dict(w_in=w_in, mu_x=mu_x, mu_rkv=mu_rkv, w0=w0, w_A=w_A, w_B=w_B, a0=a0, a_A=a_A, a_B=a_B, g_A=g_A,
                  g_B=g_B, k_k=k_k, k_a=k_a, r_k=r_k, gn_g=gn_g, gn_b=gn_b, conv_w=conv_w, conv_b=conv_b,
                  conv_ln_g=conv_ln_g, conv_ln_b=conv_ln_b, w_out=w_out, ln1_g=ln1_g, ln1_b=ln1_b,
                  w_router=w_router, b_router=b_router, w_gu=w_gu, b_gu=b_gu, w_down=w_down, b_down=b_down,
                  ln2_g=ln2_g, ln2_b=ln2_b)
    depth = w_in.shape[0]
    assert x_sample.shape[1] == 1, "the sample group advances one token per step"
    alpha = (2.0 * depth) ** 0.25
    xp, xs = x_prompt, x_sample
    p_states, s_states = [], []
    for l in range(depth):
        lw = {name: p[l] for name, p in params.items()}
        for name in _VEC_PARAMS:
            lw[name] = lw[name].reshape(1, -1)
        for name in _BF16_PARAMS:
            lw[name] = lw[name].astype(BF16)
        xp, xs, p_st, s_st = _layer(xp, xs, state_shift_x[l], state_shift_rkv[l], state_wkv[l], state_conv[l],
                                    lw, alpha)
        p_states.append(p_st)
        s_states.append(s_st)
    stack = lambda states, i: jnp.stack([st[i] for st in states])
    return (xp, xs,
            stack(p_states, 0), stack(p_states, 1), stack(p_states, 2), stack(p_states, 3),
            stack(s_states, 0), stack(s_states, 1), stack(s_states, 2), stack(s_states, 3))
```

```python
import functools
import math

import jax
import jax.numpy as jnp
from jax import lax
from jax.experimental import pallas as pl
from jax.experimental.pallas import tpu as pltpu

F32 = jnp.float32
BF16 = jnp.bfloat16

RWKV_HEAD = 64
CONV_WIDTH = 31
TOP_K = 4
SWIGLU_LIMIT = 7.0
SWIGLU_ALPHA = 1.702
LN_EPS = 1e-5
GN_EPS = 64e-5

LANES = 128
SUBLANES = 8
MXU_DIM = 256
VMEM_BYTES = 64 * 1024 * 1024
VMEM_LIMIT_BYTES = 56 * 1024 * 1024
MOE_VMEM_LIMIT_BYTES = VMEM_BYTES - 3 * 1024 * 1024

SCAN_CHUNK = 64
HEADS_PER_GROUP = MXU_DIM // RWKV_HEAD
CONV_HALO = 32
CONV_ROW_CHUNK = 128
MOE_SUB = 256


def _row_tile(n, target):
    best = None
    for t in range(SUBLANES, min(n, target) + 1, SUBLANES):
        if n % t == 0:
            best = t
    assert best is not None, (n, target)
    return best


def _params(*sem, vmem_limit_bytes=VMEM_LIMIT_BYTES):
    return pltpu.CompilerParams(dimension_semantics=sem, vmem_limit_bytes=vmem_limit_bytes)


def _const_spec(shape):
    nd = len(shape)
    return pl.BlockSpec(shape, lambda *_: (0,) * nd, pipeline_mode=pl.Buffered(1))


def _dot(a, b):
    return jnp.dot(a, b, preferred_element_type=F32)


def _split2(x):
    hi = x.astype(BF16)
    lo = (x - hi.astype(F32)).astype(BF16)
    return hi, lo


def _layer_norm(z, g, b):
    mu = jnp.mean(z, axis=-1, keepdims=True)
    zc = z - mu
    var = jnp.mean(zc * zc, axis=-1, keepdims=True)
    return zc * lax.rsqrt(var + LN_EPS) * g + b


def _shift_rows(x, first_row):
    xs = pltpu.roll(x, 1, axis=0)
    row = lax.broadcasted_iota(jnp.int32, x.shape, 0)
    return jnp.where(row == 0, first_row, xs)


def _inproj_kernel(x_ref, xprev_ref, w_in_ref, mu_ref, wA_ref, aA_ref, gA_ref, wB_ref, aB_ref, gB_ref,
                   w0_ref, a0_ref, urkv_ref, hglu_ref, dlog_ref, asig_ref, g_ref, *, seq_tiles, halo):
    x = x_ref[...]
    if halo:
        first = (pl.program_id(0) % seq_tiles) == 0
        prev_row = jnp.where(first, 0.0, xprev_ref[SUBLANES - 1:SUBLANES, :])
        xprev = _shift_rows(x, prev_row)
    else:
        xprev = xprev_ref[...]
    xx = xprev - x
    xb = x.astype(BF16)
    d3 = urkv_ref.shape[1]
    dc = hglu_ref.shape[1]
    urkv_ref[...] = _dot(xb, w_in_ref[:, :d3])
    val = _dot(xb, w_in_ref[:, d3:d3 + dc])
    gate = _dot(xb, w_in_ref[:, d3 + dc:])
    hglu_ref[...] = val * jax.nn.sigmoid(gate)

    xw = (x + xx * mu_ref[0:1, :]).astype(BF16)
    tw = jnp.tanh(_dot(xw, wA_ref[...]))
    wlin = w0_ref[...] + _dot(tw.astype(BF16), wB_ref[...])
    wlog = -jax.nn.softplus(-wlin) - 0.5
    dlog_ref[...] = -jnp.exp(wlog)

    xa = (x + xx * mu_ref[1:2, :]).astype(BF16)
    ta = _dot(xa, aA_ref[...])
    asig_ref[...] = jax.nn.sigmoid(a0_ref[...] + _dot(ta.astype(BF16), aB_ref[...]))

    xg = (x + xx * mu_ref[2:3, :]).astype(BF16)
    tg = jax.nn.sigmoid(_dot(xg, gA_ref[...]))
    g_ref[...] = _dot(tg.astype(BF16), gB_ref[...])


def _inproj(x, xprev, seq_len, lw):
    n, d = x.shape
    d_in = lw["w_in"].shape[1]
    dr = lw["w_B"].shape[1]
    d3 = 3 * dr
    dc = (d_in - d3) // 2
    halo = xprev is None
    tm = _row_tile(seq_len if halo else n, 256)
    seq_tiles = (seq_len // tm) if halo else 1
    if halo:
        prev_spec = pl.BlockSpec((SUBLANES, d), lambda i: (jnp.maximum(i * (tm // SUBLANES) - 1, 0), 0))
        prev_arg = x
    else:
        prev_spec = pl.BlockSpec((tm, d), lambda i: (i, 0))
        prev_arg = xprev
    row = lambda c: pl.BlockSpec((tm, c), lambda i: (i, 0))
    outs = [jax.ShapeDtypeStruct((n, c), F32) for c in (d3, dc, dr, dr, dr)]
    consts = [lw["w_in"], lw["mu_x"], lw["w_A"], lw["a_A"], lw["g_A"], lw["w_B"], lw["a_B"], lw["g_B"],
              lw["w0"], lw["a0"]]
    return pl.pallas_call(
        functools.partial(_inproj_kernel, seq_tiles=seq_tiles, halo=halo),
        grid=(n // tm,),
        in_specs=[row(d), prev_spec] + [_const_spec(c.shape) for c in consts],
        out_specs=[row(c) for c in (d3, dc, dr, dr, dr)],
        out_shape=outs,
        compiler_params=_params("arbitrary"),
        name="inproj",
    )(x, prev_arg, *consts)


def _head_mask(n, dtype):
    r = lax.broadcasted_iota(jnp.int32, (n, n), 0) // RWKV_HEAD
    c = lax.broadcasted_iota(jnp.int32, (n, n), 1) // RWKV_HEAD
    return (r == c).astype(dtype)


def _head_sum(x, ones_bd):
    cols = []
    for c0 in range(0, x.shape[1], MXU_DIM):
        hi, lo = _split2(x[:, c0:c0 + MXU_DIM])
        cols.append(_dot(hi, ones_bd) + _dot(lo, ones_bd))
    return cols[0] if len(cols) == 1 else jnp.concatenate(cols, axis=1)


def _scanprep_kernel(u_ref, uprev_ref, mu_ref, kk_ref, ka_ref, rk_ref, asig_ref,
                     r_ref, k_ref, v_ref, a_ref, b_ref, bonus_ref, *, seq_tiles, halo):
    u = u_ref[...]
    if halo:
        first = (pl.program_id(0) % seq_tiles) == 0
        prev_row = jnp.where(first, 0.0, uprev_ref[SUBLANES - 1:SUBLANES, :])
        uprev = _shift_rows(u, prev_row)
    else:
        uprev = uprev_ref[...]
    rkv = u + (uprev - u) * mu_ref[...]
    dr = r_ref.shape[1]
    r = rkv[:, :dr]
    k = rkv[:, dr:2 * dr]
    v = rkv[:, 2 * dr:]
    asig = asig_ref[...]
    ones_bd = _head_mask(MXU_DIM, BF16)
    kk = k * kk_ref[...]
    kk = kk * lax.rsqrt(jnp.maximum(_head_sum(kk * kk, ones_bd), 1e-24))
    k = k * (1.0 + (asig - 1.0) * ka_ref[...])
    r_ref[...] = r
    k_ref[...] = k
    v_ref[...] = v
    a_ref[...] = -kk
    b_ref[...] = kk * asig
    bonus_ref[...] = _head_sum(r * k * rk_ref[...], ones_bd) * v


def _scanprep(u_rkv, uprev, seq_len, asig, lw):
    n, d3 = u_rkv.shape
    dr = d3 // 3
    halo = uprev is None
    tm = _row_tile(seq_len if halo else n, 256)
    seq_tiles = (seq_len // tm) if halo else 1
    if halo:
        prev_spec = pl.BlockSpec((SUBLANES, d3), lambda i: (jnp.maximum(i * (tm // SUBLANES) - 1, 0), 0))
        prev_arg = u_rkv
    else:
        prev_spec = pl.BlockSpec((tm, d3), lambda i: (i, 0))
        prev_arg = uprev
    row = lambda c: pl.BlockSpec((tm, c), lambda i: (i, 0))
    consts = [lw["mu_rkv"], lw["k_k"], lw["k_a"], lw["r_k"]]
    return pl.pallas_call(
        functools.partial(_scanprep_kernel, seq_tiles=seq_tiles, halo=halo),
        grid=(n // tm,),
        in_specs=[row(d3), prev_spec] + [_const_spec(c.shape) for c in consts] + [row(dr)],
        out_specs=[row(dr)] * 6,
        out_shape=[jax.ShapeDtypeStruct((n, dr), F32)] * 6,
        compiler_params=_params("arbitrary"),
        name="scanprep",
    )(u_rkv, prev_arg, *consts, asig)


def _blockdiag(x, mask):
    reps = mask.shape[0] // x.shape[0]
    return jnp.concatenate([x] * reps, axis=0) * mask


def _dot_nt(a, b):
    return lax.dot_general(a, b, (((1,), (1,)), ((), ())), preferred_element_type=F32)


def _dot_tn(a, b):
    return lax.dot_general(a, b, (((0,), (0,)), ((), ())), preferred_element_type=F32)


def _scan_kernel(r_ref, dl_ref, k_ref, v_ref, a_ref, b_ref, y_ref, s_ref, h_scr):
    c = pl.program_id(0)
    nb = r_ref.shape[0]
    C = r_ref.shape[1]
    W = MXU_DIM
    ng = r_ref.shape[2] // W

    @pl.when(c == 0)
    def _():
        h_scr[...] = jnp.zeros_like(h_scr)

    ti = lax.broadcasted_iota(jnp.int32, (C, C), 0)
    tj = lax.broadcasted_iota(jnp.int32, (C, C), 1)
    tri = (ti >= tj).astype(BF16)
    t_row = lax.broadcasted_iota(jnp.int32, (C, W), 0)
    j_col = lax.broadcasted_iota(jnp.int32, (C, W), 1) % C
    strict = j_col < t_row
    incl = j_col <= t_row
    eye_cat = (j_col == t_row).astype(F32)
    rb = lax.broadcasted_iota(jnp.int32, (HEADS_PER_GROUP * C, W), 0) // C
    cb = lax.broadcasted_iota(jnp.int32, (HEADS_PER_GROUP * C, W), 1) // RWKV_HEAD
    bmask = (rb == cb).astype(BF16)
    hmask = _head_mask(W, F32)

    chains = [(bi, gi) for bi in range(nb) for gi in range(ng)]
    each = lambda f, *lists: [f(*xs) for xs in zip(*lists)]
    load = lambda ref: [ref[bi, :, gi * W:(gi + 1) * W] for bi, gi in chains]
    r, dl, k, v, a, b = (load(ref) for ref in (r_ref, dl_ref, k_ref, v_ref, a_ref, b_ref))
    sbd = [h_scr[bi, gi] for bi, gi in chains]
    bd = lambda x: _blockdiag(x, bmask)

    def cumsum(d):
        d_hi = d.astype(BF16)
        d_r1 = d - d_hi.astype(F32)
        d_mid = d_r1.astype(BF16)
        d_lo = (d_r1 - d_mid.astype(F32)).astype(BF16)
        return _dot(tri, d_hi) + (_dot(tri, d_mid) + _dot(tri, d_lo))

    cum = each(cumsum, dl)
    cum_last = each(lambda x: x[C - 1:C, :], cum)
    e_neg = each(lambda x: jnp.exp(-x), cum)
    e_end = each(lambda x, xl: jnp.exp(xl - x), cum, cum_last)
    at = each(lambda x, cu, d: (x * jnp.exp(cu - d)).astype(BF16), a, cum, dl)
    rt = each(lambda x, cu: (x * jnp.exp(cu)).astype(BF16), r, cum)
    bt = each(lambda x, e: (x * e).astype(BF16), b, e_neg)
    kt = each(lambda x, e: (x * e).astype(BF16), k, e_neg)
    bh = each(lambda x, e: (x * e).astype(BF16), b, e_end)
    kh = each(lambda x, e: (x * e).astype(BF16), k, e_end)
    vb = each(lambda x: x.astype(BF16), v)
    ar = each(lambda x, y: jnp.concatenate([x, y], axis=0), at, rt)

    p_b = each(lambda x, y: _dot_nt(x, bd(y)), ar, bt)
    p_k = each(lambda x, y: _dot_nt(x, bd(y)), ar, kt)
    arh = each(lambda x, s: _dot_nt(x, s.astype(BF16)), ar, sbd)
    p_ab = each(lambda p: jnp.where(strict, p[:C], 0.0), p_b)
    p_rb = each(lambda p: jnp.where(incl, p[C:], 0.0).astype(BF16), p_b)
    p_akrk = each(lambda p: jnp.concatenate([jnp.where(strict, p[:C], 0.0), jnp.where(incl, p[C:], 0.0)],
                                            axis=0).astype(BF16), p_k)

    pv = each(lambda p, x: _dot(p, bd(x)), p_akrk, vb)
    n_sq = int(math.log2(C))
    nn_b = each(lambda p: p.astype(BF16), p_ab)
    nn_b = each(lambda n: _dot(n, bd(n)).astype(BF16), nn_b)
    w = each(lambda x, y: (x[:C] + y[:C]).astype(BF16), arh, pv)
    tm = each(lambda p: eye_cat + p, p_ab)
    for i in range(1, n_sq):
        last = i == n_sq - 1
        lhs = each(lambda t, n: t.astype(BF16) if last else jnp.concatenate([t.astype(BF16), n], axis=0), tm, nn_b)
        prod = each(lambda l, n: _dot(l, bd(n)), lhs, nn_b)
        tm = each(lambda t, p: t + p[:C], tm, prod)
        if not last:
            nn_b = each(lambda p: p[C:].astype(BF16), prod)
    ub = each(lambda t, x: _dot(t.astype(BF16), bd(x)).astype(BF16), tm, w)

    yv = each(lambda x, y, p, u: x[C:] + y[C:] + _dot(p, bd(u)), arh, pv, p_rb, ub)
    upd = each(lambda u, x, y, z: _dot_tn(jnp.concatenate([u, x], axis=0), jnp.concatenate([y, z], axis=0)),
               ub, vb, bh, kh)
    for (bi, gi), y, s, xl, up in zip(chains, yv, sbd, cum_last, upd):
        y_ref[bi, :, gi * W:(gi + 1) * W] = y
        h_scr[bi, gi] = (s * jnp.exp(xl) + up) * hmask

    @pl.when(c == pl.num_programs(0) - 1)
    def _():
        s_ref[...] = h_scr[...]


def _scan(r, dl, k, v, a, b):
    nb, t, dr = r.shape
    ng = dr // MXU_DIM
    C = SCAN_CHUNK
    spec = pl.BlockSpec((nb, C, dr), lambda c: (0, c, 0))
    return pl.pallas_call(
        _scan_kernel,
        grid=(t // C,),
        in_specs=[spec] * 6,
        out_specs=[spec, pl.BlockSpec((nb, ng, MXU_DIM, MXU_DIM), lambda c: (0, 0, 0, 0))],
        out_shape=[jax.ShapeDtypeStruct((nb, t, dr), F32),
                   jax.ShapeDtypeStruct((nb, ng, MXU_DIM, MXU_DIM), F32)],
        scratch_shapes=[pltpu.VMEM((nb, ng, MXU_DIM, MXU_DIM), F32)],
        compiler_params=_params("arbitrary"),
        name="wkv_scan",
    )(r, dl, k, v, a, b)


def _step_kernel(s_ref, r_ref, dl_ref, k_ref, v_ref, a_ref, b_ref, sout_ref, y_ref):
    S = s_ref[0]
    per_key = lambda ref: ref[...][None, :, :]
    sa = jnp.sum(S * per_key(a_ref), axis=1, keepdims=True)
    v = v_ref[...][:, None, :]
    s_new = S * jnp.exp(per_key(dl_ref)) + sa * per_key(b_ref) + v * per_key(k_ref)
    sout_ref[0] = s_new
    y_ref[...] = jnp.sum(s_new * per_key(r_ref), axis=1)


def _step(S0, r, dl, k, v, a, b):
    nb, nh, n, _ = S0.shape
    st = jnp.transpose(S0, (1, 2, 3, 0))
    sspec = pl.BlockSpec((1, n, n, nb), lambda h: (h, 0, 0, 0))
    vspec = pl.BlockSpec((n, nb), lambda h: (h, 0))
    s_new, y = pl.pallas_call(
        _step_kernel,
        grid=(nh,),
        in_specs=[sspec] + [vspec] * 6,
        out_specs=[sspec, vspec],
        out_shape=[jax.ShapeDtypeStruct(st.shape, F32), jax.ShapeDtypeStruct((nh * n, nb), F32)],
        compiler_params=_params("arbitrary"),
        name="wkv_step",
    )(st, *(x.T for x in (r, dl, k, v, a, b)))
    return jnp.transpose(s_new, (3, 0, 1, 2)), y.T


def _post_kernel(y_ref, bonus_ref, g_ref, gng_ref, gnb_ref, o_ref):
    y = y_ref[...]
    ones_bd = _head_mask(MXU_DIM, BF16)
    inv_n = 1.0 / RWKV_HEAD
    mu = _head_sum(y, ones_bd) * inv_n
    yc = y - mu
    var = _head_sum(yc * yc, ones_bd) * inv_n
    yn = yc * lax.rsqrt(var + GN_EPS) * gng_ref[...] + gnb_ref[...]
    o_ref[...] = (yn + bonus_ref[...]) * g_ref[...]


def _post(y, bonus, g, lw):
    n, dr = y.shape
    tm = _row_tile(n, 512)
    row = pl.BlockSpec((tm, dr), lambda i: (i, 0))
    return pl.pallas_call(
        _post_kernel,
        grid=(n // tm,),
        in_specs=[row, row, row, _const_spec(lw["gn_g"].shape), _const_spec(lw["gn_b"].shape)],
        out_specs=row,
        out_shape=jax.ShapeDtypeStruct((n, dr), F32),
        compiler_params=_params("arbitrary"),
        name="wkv_post",
    )(y, bonus, g, lw["gn_g"], lw["gn_b"])


def _conv_seq_kernel(h_ref, halo_ref, w_ref, b_ref, lng_ref, lnb_ref, o_ref, hp_scr, *, seq_tiles):
    tt = h_ref.shape[0]
    first = (pl.program_id(0) % seq_tiles) == 0
    hp_scr[0:CONV_HALO, :] = jnp.where(first, 0.0, halo_ref[...])
    hp_scr[CONV_HALO:, :] = h_ref[...]
    off = CONV_HALO - (CONV_WIDTH - 1)
    rc = CONV_ROW_CHUNK if tt % CONV_ROW_CHUNK == 0 else tt
    for c0 in range(0, o_ref.shape[1], LANES):
        lanes = slice(c0, c0 + LANES)
        for t0 in range(0, tt, rc):
            acc = jnp.broadcast_to(b_ref[:, lanes], (rc, LANES))
            for p in range(SUBLANES):
                rows = rc if p == 0 else rc + SUBLANES
                part = None
                for m in range((CONV_HALO + SUBLANES) // SUBLANES):
                    j = SUBLANES * m + p - off
                    if 0 <= j < CONV_WIDTH:
                        term = hp_scr[t0 + SUBLANES * m:t0 + SUBLANES * m + rows, lanes] * w_ref[j:j + 1, lanes]
                        part = term if part is None else part + term
                acc = acc + part[p:p + rc]
            o_ref[t0:t0 + rc, lanes] = acc
    z = _layer_norm(o_ref[...], lng_ref[...], lnb_ref[...])
    o_ref[...] = z * jax.nn.sigmoid(z)


def _conv_seq(h, seq_len, lw):
    n, dc = h.shape
    tt = _row_tile(seq_len, 256)
    assert tt % CONV_HALO == 0
    seq_tiles = seq_len // tt
    row = pl.BlockSpec((tt, dc), lambda i: (i, 0))
    halo = pl.BlockSpec((CONV_HALO, dc), lambda i: (jnp.maximum(i * (tt // CONV_HALO) - 1, 0), 0))
    consts = [lw["conv_w"], lw["conv_b"], lw["conv_ln_g"], lw["conv_ln_b"]]
    return pl.pallas_call(
        functools.partial(_conv_seq_kernel, seq_tiles=seq_tiles),
        grid=(n // tt,),
        in_specs=[row, halo] + [_const_spec(c.shape) for c in consts],
        out_specs=row,
        out_shape=jax.ShapeDtypeStruct((n, dc), F32),
        scratch_shapes=[pltpu.VMEM((CONV_HALO + tt, dc), F32)],
        compiler_params=_params("arbitrary"),
        name="conv_seq",
    )(h, h, *consts)


def _conv_step_kernel(buf_ref, h_ref, w_ref, b_ref, lng_ref, lnb_ref, o_ref, nbuf_ref):
    h = h_ref[...]
    acc = b_ref[...] + h * w_ref[CONV_WIDTH - 1:CONV_WIDTH, :]
    for j in range(CONV_WIDTH - 1):
        tap = buf_ref[j]
        acc = acc + tap * w_ref[j:j + 1, :]
        if j > 0:
            nbuf_ref[j - 1] = tap
    nbuf_ref[CONV_WIDTH - 2] = h
    z = _layer_norm(acc, lng_ref[...], lnb_ref[...])
    o_ref[...] = z * jax.nn.sigmoid(z)


def _conv_step(buf, h, lw):
    nb, wm1, dc = buf.shape
    bb = _row_tile(nb, 32)
    bspec = pl.BlockSpec((wm1, bb, dc), lambda i: (0, i, 0))
    row = pl.BlockSpec((bb, dc), lambda i: (i, 0))
    consts = [lw["conv_w"], lw["conv_b"], lw["conv_ln_g"], lw["conv_ln_b"]]
    out, nbuf = pl.pallas_call(
        _conv_step_kernel,
        grid=(nb // bb,),
        in_specs=[bspec, row] + [_const_spec(c.shape) for c in consts],
        out_specs=[row, bspec],
        out_shape=[jax.ShapeDtypeStruct((nb, dc), F32), jax.ShapeDtypeStruct((wm1, nb, dc), F32)],
        compiler_params=_params("arbitrary"),
        name="conv_step",
    )(jnp.transpose(buf, (1, 0, 2)), h, *consts)
    return out, jnp.transpose(nbuf, (1, 0, 2))


def _outproj_kernel(*refs, alpha, n_dst):
    x_ref, yr_ref, yc_ref, wo_ref, g_ref, b_ref, wr_ref, br_ref, h_ref, logit_ref = refs[n_dst:]
    ymix = jnp.concatenate([yr_ref[...].astype(BF16), yc_ref[...].astype(BF16)], axis=1)
    h = _layer_norm(alpha * x_ref[...] + _dot(ymix, wo_ref[...]), g_ref[...], b_ref[...])
    h_ref[...] = h
    logit_ref[...] = _dot(h.astype(BF16), wr_ref[...]) + br_ref[...]


def _outproj(x, y_rwkv, y_conv, lw, alpha, n_total, row0, dst=()):
    n, d = x.shape
    dr = y_rwkv.shape[1]
    dc = y_conv.shape[1]
    ne = lw["w_router"].shape[1]
    tm = _row_tile(math.gcd(n, row0) if row0 else n, 256)
    blk0 = row0 // tm
    row = lambda c: pl.BlockSpec((tm, c), lambda i: (i, 0))
    out_row = lambda c: pl.BlockSpec((tm, c), lambda i: (blk0 + i, 0))
    consts = [lw["w_out"], lw["ln1_g"], lw["ln1_b"], lw["w_router"], lw["b_router"]]
    return pl.pallas_call(
        functools.partial(_outproj_kernel, alpha=alpha, n_dst=len(dst)),
        grid=(n // tm,),
        in_specs=[pl.BlockSpec(memory_space=pl.ANY)] * len(dst) + [row(d), row(dr), row(dc)]
        + [_const_spec(c.shape) for c in consts],
        out_specs=[out_row(d), out_row(ne)],
        out_shape=[jax.ShapeDtypeStruct((n_total, d), F32), jax.ShapeDtypeStruct((n_total, ne), F32)],
        input_output_aliases={i: i for i in range(len(dst))},
        compiler_params=_params("arbitrary"),
        name="outproj",
    )(*dst, x, y_rwkv, y_conv, *consts)


def _route(logits, moe_tm):
    n_tok, ne = logits.shape
    top_val, top_idx = lax.top_k(logits, TOP_K)
    gates = jax.nn.softmax(top_val, axis=-1)
    n_assign = n_tok * TOP_K
    e_flat = top_idx.reshape(n_assign).astype(jnp.int32)
    onehot = (e_flat[:, None] == jnp.arange(ne, dtype=jnp.int32)[None, :]).astype(jnp.int32)
    counts = onehot.sum(0)
    rank = jnp.take_along_axis(jnp.cumsum(onehot, axis=0), e_flat[:, None], axis=1)[:, 0] - 1
    nblk_e = (counts + moe_tm - 1) // moe_tm
    blk_end = jnp.cumsum(nblk_e)
    blk_start = blk_end - nblk_e
    n_active = blk_end[-1]
    nb_max = n_assign // moe_tm + ne
    pos = blk_start[e_flat] * moe_tm + rank
    sorted_tok = (jnp.argsort(e_flat, stable=True) // TOP_K).astype(jnp.int32)
    sorted_tok = jnp.concatenate([sorted_tok, jnp.zeros((SUBLANES,), jnp.int32)])
    cstart = jnp.cumsum(counts) - counts
    q = jnp.arange(nb_max, dtype=jnp.int32)
    q_eff = jnp.minimum(q, n_active - 1)
    blk_e = jnp.minimum(jnp.searchsorted(blk_end, q_eff, side="right"), ne - 1).astype(jnp.int32)
    row_in_e = (q_eff - blk_start[blk_e]) * moe_tm
    valid = jnp.clip(counts[blk_e] - row_in_e, 0, moe_tm)
    valid = jnp.where(q < n_active, valid, 0).astype(jnp.int32)
    src_start = (cstart[blk_e] + row_in_e).astype(jnp.int32)
    return gates, pos.astype(jnp.int32), sorted_tok, src_start, blk_e, q_eff.astype(jnp.int32), valid


def _gather_kernel(qeff_ref, valid_ref, src_ref, tok_ref, h_hbm, o_ref, buf, sem):
    q = pl.program_id(0)
    nvalid = valid_ref[q]

    def row_copy(r, tok):
        return pltpu.make_async_copy(h_hbm.at[pl.ds(tok, 1)], buf.at[pl.ds(r, 1)], sem)

    @pl.when(nvalid > 0)
    def _():
        base = src_ref[q]
        ngroups = (nvalid + SUBLANES - 1) // SUBLANES

        def issue(g, c):
            r0 = pl.multiple_of(g * SUBLANES, SUBLANES)
            for i in range(SUBLANES):
                row_copy(r0 + i, tok_ref[base + r0 + i]).start()
            return c

        lax.fori_loop(0, ngroups, issue, 0)

        def wait(g, c):
            r0 = pl.multiple_of(g * SUBLANES, SUBLANES)
            pltpu.make_async_copy(h_hbm.at[pl.ds(0, SUBLANES)], buf.at[pl.ds(r0, SUBLANES)], sem).wait()
            return c

        lax.fori_loop(0, ngroups, wait, 0)
        rows = lax.broadcasted_iota(jnp.int32, buf.shape, 0)
        o_ref[...] = jnp.where(rows < nvalid, buf[...], 0.0).astype(o_ref.dtype)


def _gather_rows(h, sorted_tok, src_start, q_eff, valid, moe_tm):
    n_tok, d = h.shape
    nb_max = q_eff.shape[0]
    grid_spec = pltpu.PrefetchScalarGridSpec(
        num_scalar_prefetch=4,
        grid=(nb_max,),
        in_specs=[pl.BlockSpec(memory_space=pl.ANY)],
        out_specs=pl.BlockSpec((moe_tm, d), lambda q, qe, va, sr, tk: (qe[q], 0)),
        scratch_shapes=[pltpu.VMEM((moe_tm, d), F32), pltpu.SemaphoreType.DMA(())],
    )
    return pl.pallas_call(
        _gather_kernel,
        grid_spec=grid_spec,
        out_shape=jax.ShapeDtypeStruct((nb_max * moe_tm, d), BF16),
        compiler_params=_params("arbitrary"),
        name="moe_gather",
    )(q_eff, valid, src_start, sorted_tok, h)


def _moe_kernel(be_ref, qeff_ref, valid_ref, x_ref, wg_ref, wl_ref, bg_ref, bl_ref, wd_ref, bd_ref, o_ref,
                wg_s, wl_s, wd_s, *, sub):
    q = pl.program_id(0)
    j = pl.program_id(1)
    nvalid = valid_ref[q]
    unit = sub // 2
    n_units = (nvalid + unit - 1) // unit
    n_all = o_ref.shape[0] // unit

    def fill(s, value):
        r0 = pl.multiple_of(s * unit, unit)
        o_ref[pl.ds(r0, unit), :] = jnp.broadcast_to(value, (unit, o_ref.shape[1]))

    def rows_block(r0, rows):
        x = x_ref[pl.ds(r0, rows), :]
        g = jnp.minimum(_dot(x, wg_s[...]) + bg_ref[0], SWIGLU_LIMIT)
        l = jnp.clip(_dot(x, wl_s[...]) + bl_ref[0], -SWIGLU_LIMIT, SWIGLU_LIMIT)
        act = g * jax.nn.sigmoid(SWIGLU_ALPHA * g) * (l + 1.0)
        o_ref[pl.ds(r0, rows), :] += _dot(act.astype(BF16), wd_s[...])

    @pl.when(nvalid > 0)
    def _():
        @pl.when(j == 0)
        def _():
            lax.fori_loop(0, n_units, lambda s, c: (fill(s, bd_ref[0]), c)[1], 0)
            lax.fori_loop(n_units, n_all, lambda s, c: (fill(s, jnp.zeros((1, 1), F32)), c)[1], 0)

        wg_s[...] = wg_ref[0].astype(BF16)
        wl_s[...] = wl_ref[0].astype(BF16)
        wd_s[...] = wd_ref[0].astype(BF16)
        rows_block(0, unit)
        rest = n_units - 1
        n_pairs = rest // 4
        tail = rest - n_pairs * 4

        def pair(i, c):
            r0 = pl.multiple_of(unit + i * 2 * sub, unit)
            rows_block(r0, sub)
            rows_block(r0 + sub, sub)
            return c

        lax.fori_loop(0, n_pairs, pair, 0)
        t0 = pl.multiple_of(unit + n_pairs * 2 * sub, unit)

        @pl.when(tail >= 2)
        def _():
            rows_block(t0, sub)

        @pl.when(tail % 2 == 1)
        def _():
            rows_block(pl.multiple_of(t0 + (tail // 2) * sub, unit), unit)


def _moe_experts(x_sorted, blk_e, q_eff, valid, lw, moe_tm, sub, tf):
    n_rows, d = x_sorted.shape
    ne, _, f2 = lw["w_gu"].shape
    f = f2 // 2
    nf = f // tf
    nb_max = q_eff.shape[0]
    b_gu = lw["b_gu"].reshape(ne, 1, f2)
    b_down = lw["b_down"].reshape(ne, 1, d)

    def jf(q, j, va):
        return jnp.where(va[q] > 0, j, nf - 1)

    grid_spec = pltpu.PrefetchScalarGridSpec(
        num_scalar_prefetch=3,
        grid=(nb_max, nf),
        in_specs=[
            pl.BlockSpec((moe_tm, d), lambda q, j, be, qe, va: (qe[q], 0)),
            pl.BlockSpec((1, d, tf), lambda q, j, be, qe, va: (be[q], 0, jf(q, j, va))),
            pl.BlockSpec((1, d, tf), lambda q, j, be, qe, va: (be[q], 0, nf + jf(q, j, va))),
            pl.BlockSpec((1, 1, tf), lambda q, j, be, qe, va: (be[q], 0, jf(q, j, va))),
            pl.BlockSpec((1, 1, tf), lambda q, j, be, qe, va: (be[q], 0, nf + jf(q, j, va))),
            pl.BlockSpec((1, tf, d), lambda q, j, be, qe, va: (be[q], jf(q, j, va), 0)),
            pl.BlockSpec((1, 1, d), lambda q, j, be, qe, va: (be[q], 0, 0)),
        ],
        out_specs=pl.BlockSpec((moe_tm, d), lambda q, j, be, qe, va: (qe[q], 0)),
        scratch_shapes=[pltpu.VMEM((d, tf), BF16), pltpu.VMEM((d, tf), BF16), pltpu.VMEM((tf, d), BF16)],
    )
    return pl.pallas_call(
        functools.partial(_moe_kernel, sub=sub),
        grid_spec=grid_spec,
        out_shape=jax.ShapeDtypeStruct((n_rows, d), F32),
        compiler_params=_params("arbitrary", "arbitrary", vmem_limit_bytes=MOE_VMEM_LIMIT_BYTES),
        name="moe_experts",
    )(blk_e, q_eff, valid, x_sorted, lw["w_gu"], lw["w_gu"], b_gu, b_gu, lw["w_down"], b_down)


def _combine_kernel(pos_ref, h_ref, gates_ref, yrows_hbm, g_ref, b_ref, op_ref, os_ref, buf, sem, *, alpha, n_first):
    i = pl.program_id(0)
    tm = h_ref.shape[0]

    def row_copy(r, k, p):
        return pltpu.make_async_copy(yrows_hbm.at[pl.ds(p, 1)], buf.at[k, pl.ds(r, 1)], sem)

    def issue(r2, c):
        for dr in range(2):
            r = r2 * 2 + dr
            for k in range(TOP_K):
                row_copy(r, k, pos_ref[(i * tm + r) * TOP_K + k]).start()
        return c

    lax.fori_loop(0, tm // 2, issue, 0)
    for k in range(TOP_K):
        pltpu.make_async_copy(yrows_hbm.at[pl.ds(0, tm)], buf.at[k], sem).wait()
    gates = gates_ref[...]
    ffn = buf[0] * gates[:, 0:1]
    for k in range(1, TOP_K):
        ffn = ffn + buf[k] * gates[:, k:k + 1]
    y = _layer_norm(alpha * h_ref[...] + ffn, g_ref[...], b_ref[...])

    @pl.when(i < n_first)
    def _():
        op_ref[...] = y

    @pl.when(i >= n_first)
    def _():
        os_ref[...] = y


def _combine(h, gates, pos, y_rows, lw, alpha, n_prompt):
    n_tok, d = h.shape
    tm = _row_tile(math.gcd(n_prompt, n_tok - n_prompt), 128)
    n_first = n_prompt // tm
    grid_spec = pltpu.PrefetchScalarGridSpec(
        num_scalar_prefetch=1,
        grid=(n_tok // tm,),
        in_specs=[
            pl.BlockSpec((tm, d), lambda i, p: (i, 0)),
            pl.BlockSpec((tm, TOP_K), lambda i, p: (i, 0)),
            pl.BlockSpec(memory_space=pl.ANY),
            pl.BlockSpec((1, d), lambda i, p: (0, 0)),
            pl.BlockSpec((1, d), lambda i, p: (0, 0)),
        ],
        out_specs=[pl.BlockSpec((tm, d), lambda i, p: (jnp.minimum(i, n_first - 1), 0)),
                   pl.BlockSpec((tm, d), lambda i, p: (jnp.maximum(i - n_first, 0), 0))],
        scratch_shapes=[pltpu.VMEM((TOP_K, tm, d), F32), pltpu.SemaphoreType.DMA(())],
    )
    return pl.pallas_call(
        functools.partial(_combine_kernel, alpha=alpha, n_first=n_first),
        grid_spec=grid_spec,
        out_shape=[jax.ShapeDtypeStruct((n_prompt, d), F32), jax.ShapeDtypeStruct((n_tok - n_prompt, d), F32)],
        compiler_params=_params("arbitrary"),
        name="moe_combine",
    )(pos, h, gates, y_rows, lw["ln2_g"], lw["ln2_b"])


def _moe_tiles(n_assign, ne, d, f):
    unit = MOE_SUB // 2
    mean = -(-n_assign // ne)
    tm = max(MOE_SUB, -(-(mean * 11 // 10) // unit) * unit)
    for tf in (512, 256):
        tf = min(tf, f)
        blocks = 2 * tm * d * 2 + 2 * tm * d * 4
        weights = 3 * d * tf * (2 * 4 + 2)
        temps = MOE_SUB * 2 * tf * 4
        if blocks + weights + temps <= MOE_VMEM_LIMIT_BYTES:
            return tm, tf
    return min(tm, 4 * MOE_SUB), min(256, f)


_VEC_PARAMS = ("mu_rkv", "w0", "a0", "k_k", "k_a", "r_k", "gn_g", "gn_b", "conv_b", "conv_ln_g", "conv_ln_b",
               "ln1_g", "ln1_b", "b_router", "ln2_g", "ln2_b")
_BF16_PARAMS = ("w_in", "w_A", "w_B", "a_A", "a_B", "g_A", "g_B", "w_out", "w_router")


def _diag_blocks(s):
    nb, ng = s.shape[:2]
    s6 = s.reshape(nb, ng, HEADS_PER_GROUP, RWKV_HEAD, HEADS_PER_GROUP, RWKV_HEAD)
    d = jnp.stack([s6[:, :, h, :, h, :] for h in range(HEADS_PER_GROUP)], axis=2)
    return d.reshape(nb, ng * HEADS_PER_GROUP, RWKV_HEAD, RWKV_HEAD)


def _layer(xp, xs, sx, srkv, swkv, sconv, lw, alpha):
    nbp, t, d = xp.shape
    nbs = xs.shape[0]
    xp2 = xp.reshape(nbp * t, d)
    xs2 = xs.reshape(nbs, d)

    up, hgp, dlp, asp, gp = _inproj(xp2, None, t, lw)
    us, hgs, dls, ass, gs = _inproj(xs2, sx, 1, lw)
    dr = dlp.shape[1]
    rp, kp, vp, ap, bp, bonp = _scanprep(up, None, t, asp, lw)
    rs, ks, vs, as_, bs, bons = _scanprep(us, srkv, 1, ass, lw)

    seq = lambda z: z.reshape(nbp, t, dr)
    yp_raw, s_end = _scan(seq(rp), seq(dlp), seq(kp), seq(vp), seq(ap), seq(bp))
    s_new, ys_raw = _step(swkv, rs, dls, ks, vs, as_, bs)
    yrp = _post(yp_raw.reshape(nbp * t, dr), bonp, gp, lw)
    yrs = _post(ys_raw, bons, gs, lw)

    ycp = _conv_seq(hgp, t, lw)
    ycs, nbuf = _conv_step(sconv, hgs, lw)

    n_prompt = nbp * t
    n_tok = n_prompt + nbs
    dst = _outproj(xp2, yrp, ycp, lw, alpha, n_tok, 0)
    h_all, logits = _outproj(xs2, yrs, ycs, lw, alpha, n_tok, n_prompt, dst=tuple(dst))

    ne = logits.shape[1]
    moe_tm, moe_tf = _moe_tiles(n_tok * TOP_K, ne, d, lw["w_down"].shape[1])
    gates, pos, sorted_tok, src_start, blk_e, q_eff, valid = _route(logits, moe_tm)
    x_sorted = _gather_rows(h_all, sorted_tok, src_start, q_eff, valid, moe_tm)
    y_rows = _moe_experts(x_sorted, blk_e, q_eff, valid, lw, moe_tm, MOE_SUB, moe_tf)
    yp, ys = _combine(h_all, gates, pos, y_rows, lw, alpha, n_prompt)
    yp = yp.reshape(nbp, t, d)
    ys = ys.reshape(nbs, 1, d)
    p_state = (xp[:, -1], up.reshape(nbp, t, -1)[:, -1], _diag_blocks(s_end),
               hgp.reshape(nbp, t, -1)[:, t - (CONV_WIDTH - 1):])
    s_state = (xs2, us, s_new, nbuf)
    return yp, ys, p_state, s_state


def kernel(x_prompt, x_sample, state_shift_x, state_shift_rkv, state_wkv, state_conv, w_in, mu_x, mu_rkv, w0, w_A,
           w_B, a0, a_A, a_B, g_A, g_B, k_k, k_a, r_k, gn_g, gn_b, conv_w, conv_b, conv_ln_g, conv_ln_b, w_out,
           ln1_g, ln1_b, w_router, b_router, w_gu, b_gu, w_down, b_down, ln2_g, ln2_b):
    params = dict(w_in=w_in, mu_x=mu_x, mu_rkv=mu_rkv, w0=w0, w_A=w_A, w_B=w_B, a0=a0, a_A=a_A, a_B=a_B, g_A=g_A,
                  g_B=g_B, k_k=k_k, k_a=k_a, r_k=r_k, gn_g=gn_g, gn_b=gn_b, conv_w=conv_w, conv_b=conv_b,
                  conv_ln_g=conv_ln_g, conv_ln_b=conv_ln_b, w_out=w_out, ln1_g=ln1_g, ln1_b=ln1_b,
                  w_router=w_router, b_router=b_router, w_gu=w_gu, b_gu=b_gu, w_down=w_down, b_down=b_down,
                  ln2_g=ln2_g, ln2_b=ln2_b)
    depth = w_in.shape[0]
    assert x_sample.shape[1] == 1, "the sample group advances one token per step"
    alpha = (2.0 * depth) ** 0.25
    xp, xs = x_prompt, x_sample
    p_states, s_states = [], []
    for l in range(depth):
        lw = {name: p[l] for name, p in params.items()}
        for name in _VEC_PARAMS:
            lw[name] = lw[name].reshape(1, -1)
        for name in _BF16_PARAMS:
            lw[name] = lw[name].astype(BF16)
        xp, xs, p_st, s_st = _layer(xp, xs, state_shift_x[l], state_shift_rkv[l], state_wkv[l], state_conv[l],
                                    lw, alpha)
        p_states.append(p_st)
        s_states.append(s_st)
    stack = lambda states, i: jnp.stack([st[i] for st in states])
    return (xp, xs,
            stack(p_states, 0), stack(p_states, 1), stack(p_states, 2), stack(p_states, 3),
            stack(s_states, 0), stack(s_states, 1), stack(s_states, 2), stack(s_states, 3))
```

```python
import functools
import math

import jax
import jax.numpy as jnp
from jax import lax
from jax.experimental import pallas as pl
from jax.experimental.pallas import tpu as pltpu

F32 = jnp.float32
BF16 = jnp.bfloat16

RWKV_HEAD = 64
CONV_WIDTH = 31
TOP_K = 4
SWIGLU_LIMIT = 7.0
SWIGLU_ALPHA = 1.702
LN_EPS = 1e-5
GN_EPS = 64e-5

LANES = 128
SUBLANES = 8
MXU_DIM = 256
VMEM_BYTES = 64 * 1024 * 1024
VMEM_LIMIT_BYTES = 56 * 1024 * 1024
MOE_VMEM_LIMIT_BYTES = VMEM_BYTES - 3 * 1024 * 1024

SCAN_CHUNK = 64
HEADS_PER_GROUP = MXU_DIM // RWKV_HEAD
CONV_HALO = 32
CONV_ROW_CHUNK = 128
MOE_SUB = 256


def _row_tile(n, target):
    best = None
    for t in range(SUBLANES, min(n, target) + 1, SUBLANES):
        if n % t == 0:
            best = t
    assert best is not None, (n, target)
    return best


def _params(*sem, vmem_limit_bytes=VMEM_LIMIT_BYTES):
    return pltpu.CompilerParams(dimension_semantics=sem, vmem_limit_bytes=vmem_limit_bytes)


def _const_spec(shape):
    nd = len(shape)
    return pl.BlockSpec(shape, lambda *_: (0,) * nd, pipeline_mode=pl.Buffered(1))


def _dot(a, b):
    return jnp.dot(a, b, preferred_element_type=F32)


def _split2(x):
    hi = x.astype(BF16)
    lo = (x - hi.astype(F32)).astype(BF16)
    return hi, lo


def _layer_norm(z, g, b):
    mu = jnp.mean(z, axis=-1, keepdims=True)
    zc = z - mu
    var = jnp.mean(zc * zc, axis=-1, keepdims=True)
    return zc * lax.rsqrt(var + LN_EPS) * g + b


def _shift_rows(x, first_row):
    xs = pltpu.roll(x, 1, axis=0)
    row = lax.broadcasted_iota(jnp.int32, x.shape, 0)
    return jnp.where(row == 0, first_row, xs)


def _head_mask(n, dtype):
    r = lax.broadcasted_iota(jnp.int32, (n, n), 0) // RWKV_HEAD
    c = lax.broadcasted_iota(jnp.int32, (n, n), 1) // RWKV_HEAD
    return (r == c).astype(dtype)


def _head_sum(x, ones_bd):
    cols = []
    for c0 in range(0, x.shape[1], MXU_DIM):
        hi, lo = _split2(x[:, c0:c0 + MXU_DIM])
        cols.append(_dot(hi, ones_bd) + _dot(lo, ones_bd))
    return cols[0] if len(cols) == 1 else jnp.concatenate(cols, axis=1)


def _inproj_kernel(x_ref, xprev_ref, uprev_ref, w_in_ref, mu_ref, wA_ref, aA_ref, gA_ref, wB_ref, aB_ref, gB_ref,
                   w0_ref, a0_ref, murkv_ref, kk_ref, ka_ref, rk_ref,
                   ulast_ref, hglu_ref, dlog_ref, g_ref, r_ref, k_ref, v_ref, a_ref, b_ref, bonus_ref,
                   ucarry_scr, *, seq_tiles, halo):
    x = x_ref[...]
    tm = x.shape[0]
    first = (pl.program_id(0) % seq_tiles) == 0
    if halo:
        prev_row = jnp.where(first, 0.0, xprev_ref[SUBLANES - 1:SUBLANES, :])
        xprev = _shift_rows(x, prev_row)
    else:
        xprev = xprev_ref[...]
    xx = xprev - x
    xb = x.astype(BF16)
    d3 = murkv_ref.shape[1]
    dc = hglu_ref.shape[1]
    u = _dot(xb, w_in_ref[:, :d3])
    val = _dot(xb, w_in_ref[:, d3:d3 + dc])
    gate = _dot(xb, w_in_ref[:, d3 + dc:])
    hglu_ref[...] = val * jax.nn.sigmoid(gate)

    xw = (x + xx * mu_ref[0:1, :]).astype(BF16)
    tw = jnp.tanh(_dot(xw, wA_ref[...]))
    wlin = w0_ref[...] + _dot(tw.astype(BF16), wB_ref[...])
    wlog = -jax.nn.softplus(-wlin) - 0.5
    dlog_ref[...] = -jnp.exp(wlog)

    xa = (x + xx * mu_ref[1:2, :]).astype(BF16)
    ta = _dot(xa, aA_ref[...])
    asig = jax.nn.sigmoid(a0_ref[...] + _dot(ta.astype(BF16), aB_ref[...]))

    xg = (x + xx * mu_ref[2:3, :]).astype(BF16)
    tg = jax.nn.sigmoid(_dot(xg, gA_ref[...]))
    g_ref[...] = _dot(tg.astype(BF16), gB_ref[...])

    if halo:
        @pl.when(pl.program_id(0) == 0)
        def _():
            ucarry_scr[...] = jnp.zeros_like(ucarry_scr)

        uprev = _shift_rows(u, jnp.where(first, 0.0, ucarry_scr[0:1, :]))
        ucarry_scr[0:1, :] = u[tm - 1:tm, :]
        ulast_ref[...] = u[tm - SUBLANES:tm, :]
    else:
        uprev = uprev_ref[...]
        ulast_ref[...] = u
    rkv = u + (uprev - u) * murkv_ref[...]
    dr = r_ref.shape[1]
    r = rkv[:, :dr]
    k = rkv[:, dr:2 * dr]
    v = rkv[:, 2 * dr:]
    ones_bd = _head_mask(MXU_DIM, BF16)
    kk = k * kk_ref[...]
    kk = kk * lax.rsqrt(jnp.maximum(_head_sum(kk * kk, ones_bd), 1e-24))
    k = k * (1.0 + (asig - 1.0) * ka_ref[...])
    r_ref[...] = r
    k_ref[...] = k
    v_ref[...] = v
    a_ref[...] = -kk
    b_ref[...] = kk * asig
    bonus_ref[...] = _head_sum(r * k * rk_ref[...], ones_bd) * v


def _inproj(x, xprev, uprev, seq_len, lw):
    n, d = x.shape
    d_in = lw["w_in"].shape[1]
    dr = lw["w_B"].shape[1]
    d3 = 3 * dr
    dc = (d_in - d3) // 2
    halo = xprev is None
    tm = _row_tile(seq_len if halo else n, 256)
    seq_tiles = (seq_len // tm) if halo else 1
    row = lambda c: pl.BlockSpec((tm, c), lambda i: (i, 0))
    if halo:
        prev_specs = [pl.BlockSpec((SUBLANES, d), lambda i: (jnp.maximum(i * (tm // SUBLANES) - 1, 0), 0)),
                      pl.BlockSpec((SUBLANES, d3), lambda i: (0, 0))]
        prev_args = [x, jnp.zeros((SUBLANES, d3), F32)]
        ulast_spec = pl.BlockSpec((SUBLANES, d3), lambda i: (i, 0))
        ulast_shape = jax.ShapeDtypeStruct((n // tm * SUBLANES, d3), F32)
    else:
        prev_specs = [row(d), row(d3)]
        prev_args = [xprev, uprev]
        ulast_spec = row(d3)
        ulast_shape = jax.ShapeDtypeStruct((n, d3), F32)
    consts = [lw["w_in"], lw["mu_x"], lw["w_A"], lw["a_A"], lw["g_A"], lw["w_B"], lw["a_B"], lw["g_B"],
              lw["w0"], lw["a0"], lw["mu_rkv"], lw["k_k"], lw["k_a"], lw["r_k"]]
    widths = (dc, dr, dr) + (dr,) * 6
    return pl.pallas_call(
        functools.partial(_inproj_kernel, seq_tiles=seq_tiles, halo=halo),
        grid=(n // tm,),
        in_specs=[row(d)] + prev_specs + [_const_spec(c.shape) for c in consts],
        out_specs=[ulast_spec] + [row(c) for c in widths],
        out_shape=[ulast_shape] + [jax.ShapeDtypeStruct((n, c), F32) for c in widths],
        scratch_shapes=[pltpu.VMEM((SUBLANES, d3), F32)],
        compiler_params=_params("arbitrary"),
        name="inproj",
    )(x, *prev_args, *consts)


def _blockdiag(x, mask):
    reps = mask.shape[0] // x.shape[0]
    return jnp.concatenate([x] * reps, axis=0) * mask


def _dot_nt(a, b):
    return lax.dot_general(a, b, (((1,), (1,)), ((), ())), preferred_element_type=F32)


def _dot_tn(a, b):
    return lax.dot_general(a, b, (((0,), (0,)), ((), ())), preferred_element_type=F32)


def _scan_kernel(r_ref, dl_ref, k_ref, v_ref, a_ref, b_ref, y_ref, s_ref, h_scr):
    c = pl.program_id(0)
    nb = r_ref.shape[0]
    C = r_ref.shape[1]
    W = MXU_DIM
    ng = r_ref.shape[2] // W

    @pl.when(c == 0)
    def _():
        h_scr[...] = jnp.zeros_like(h_scr)

    ti = lax.broadcasted_iota(jnp.int32, (C, C), 0)
    tj = lax.broadcasted_iota(jnp.int32, (C, C), 1)
    tri = (ti >= tj).astype(BF16)
    t_row = lax.broadcasted_iota(jnp.int32, (C, W), 0)
    j_col = lax.broadcasted_iota(jnp.int32, (C, W), 1) % C
    strict = j_col < t_row
    incl = j_col <= t_row
    eye_cat = (j_col == t_row).astype(F32)
    rb = lax.broadcasted_iota(jnp.int32, (HEADS_PER_GROUP * C, W), 0) // C
    cb = lax.broadcasted_iota(jnp.int32, (HEADS_PER_GROUP * C, W), 1) // RWKV_HEAD
    bmask = (rb == cb).astype(BF16)
    hmask = _head_mask(W, F32)

    chains = [(bi, gi) for bi in range(nb) for gi in range(ng)]
    each = lambda f, *lists: [f(*xs) for xs in zip(*lists)]
    load = lambda ref: [ref[bi, :, gi * W:(gi + 1) * W] for bi, gi in chains]
    r, dl, k, v, a, b = (load(ref) for ref in (r_ref, dl_ref, k_ref, v_ref, a_ref, b_ref))
    sbd = [h_scr[bi, gi] for bi, gi in chains]
    bd = lambda x: _blockdiag(x, bmask)

    def cumsum(d):
        d_hi = d.astype(BF16)
        d_r1 = d - d_hi.astype(F32)
        d_mid = d_r1.astype(BF16)
        d_lo = (d_r1 - d_mid.astype(F32)).astype(BF16)
        return _dot(tri, d_hi) + (_dot(tri, d_mid) + _dot(tri, d_lo))

    cum = each(cumsum, dl)
    cum_last = each(lambda x: x[C - 1:C, :], cum)
    e_neg = each(lambda x: jnp.exp(-x), cum)
    e_end = each(lambda x, xl: jnp.exp(xl - x), cum, cum_last)
    at = each(lambda x, cu, d: (x * jnp.exp(cu - d)).astype(BF16), a, cum, dl)
    rt = each(lambda x, cu: (x * jnp.exp(cu)).astype(BF16), r, cum)
    bt = each(lambda x, e: (x * e).astype(BF16), b, e_neg)
    kt = each(lambda x, e: (x * e).astype(BF16), k, e_neg)
    bh = each(lambda x, e: (x * e).astype(BF16), b, e_end)
    kh = each(lambda x, e: (x * e).astype(BF16), k, e_end)
    vb = each(lambda x: x.astype(BF16), v)
    ar = each(lambda x, y: jnp.concatenate([x, y], axis=0), at, rt)

    p_b = each(lambda x, y: _dot_nt(x, bd(y)), ar, bt)
    p_k = each(lambda x, y: _dot_nt(x, bd(y)), ar, kt)
    arh = each(lambda x, s: _dot_nt(x, s.astype(BF16)), ar, sbd)
    p_ab = each(lambda p: jnp.where(strict, p[:C], 0.0), p_b)
    p_rb = each(lambda p: jnp.where(incl, p[C:], 0.0).astype(BF16), p_b)
    p_akrk = each(lambda p: jnp.concatenate([jnp.where(strict, p[:C], 0.0), jnp.where(incl, p[C:], 0.0)],
                                            axis=0).astype(BF16), p_k)

    pv = each(lambda p, x: _dot(p, bd(x)), p_akrk, vb)
    n_sq = int(math.log2(C))
    nn_b = each(lambda p: p.astype(BF16), p_ab)
    nn_b = each(lambda n: _dot(n, bd(n)).astype(BF16), nn_b)
    w = each(lambda x, y: (x[:C] + y[:C]).astype(BF16), arh, pv)
    tm = each(lambda p: eye_cat + p, p_ab)
    for i in range(1, n_sq):
        last = i == n_sq - 1
        lhs = each(lambda t, n: t.astype(BF16) if last else jnp.concatenate([t.astype(BF16), n], axis=0), tm, nn_b)
        prod = each(lambda l, n: _dot(l, bd(n)), lhs, nn_b)
        tm = each(lambda t, p: t + p[:C], tm, prod)
        if not last:
            nn_b = each(lambda p: p[C:].astype(BF16), prod)
    ub = each(lambda t, x: _dot(t.astype(BF16), bd(x)).astype(BF16), tm, w)

    yv = each(lambda x, y, p, u: x[C:] + y[C:] + _dot(p, bd(u)), arh, pv, p_rb, ub)
    upd = each(lambda u, x, y, z: _dot_tn(jnp.concatenate([u, x], axis=0), jnp.concatenate([y, z], axis=0)),
               ub, vb, bh, kh)
    for (bi, gi), y, s, xl, up in zip(chains, yv, sbd, cum_last, upd):
        y_ref[bi, :, gi * W:(gi + 1) * W] = y
        h_scr[bi, gi] = (s * jnp.exp(xl) + up) * hmask

    @pl.when(c == pl.num_programs(0) - 1)
    def _():
        s_ref[...] = h_scr[...]


def _scan(r, dl, k, v, a, b):
    nb, t, dr = r.shape
    ng = dr // MXU_DIM
    C = SCAN_CHUNK
    spec = pl.BlockSpec((nb, C, dr), lambda c: (0, c, 0))
    return pl.pallas_call(
        _scan_kernel,
        grid=(t // C,),
        in_specs=[spec] * 6,
        out_specs=[spec, pl.BlockSpec((nb, ng, MXU_DIM, MXU_DIM), lambda c: (0, 0, 0, 0))],
        out_shape=[jax.ShapeDtypeStruct((nb, t, dr), F32),
                   jax.ShapeDtypeStruct((nb, ng, MXU_DIM, MXU_DIM), F32)],
        scratch_shapes=[pltpu.VMEM((nb, ng, MXU_DIM, MXU_DIM), F32)],
        compiler_params=_params("arbitrary"),
        name="wkv_scan",
    )(r, dl, k, v, a, b)


def _step_kernel(s_ref, r_ref, dl_ref, k_ref, v_ref, a_ref, b_ref, sout_ref, y_ref):
    S = s_ref[0]
    per_key = lambda ref: ref[...][None, :, :]
    sa = jnp.sum(S * per_key(a_ref), axis=1, keepdims=True)
    v = v_ref[...][:, None, :]
    s_new = S * jnp.exp(per_key(dl_ref)) + sa * per_key(b_ref) + v * per_key(k_ref)
    sout_ref[0] = s_new
    y_ref[...] = jnp.sum(s_new * per_key(r_ref), axis=1)


def _step(S0, r, dl, k, v, a, b):
    nb, nh, n, _ = S0.shape
    st = jnp.transpose(S0, (1, 2, 3, 0))
    sspec = pl.BlockSpec((1, n, n, nb), lambda h: (h, 0, 0, 0))
    vspec = pl.BlockSpec((n, nb), lambda h: (h, 0))
    s_new, y = pl.pallas_call(
        _step_kernel,
        grid=(nh,),
        in_specs=[sspec] + [vspec] * 6,
        out_specs=[sspec, vspec],
        out_shape=[jax.ShapeDtypeStruct(st.shape, F32), jax.ShapeDtypeStruct((nh * n, nb), F32)],
        compiler_params=_params("arbitrary"),
        name="wkv_step",
    )(st, *(x.T for x in (r, dl, k, v, a, b)))
    return jnp.transpose(s_new, (3, 0, 1, 2)), y.T


def _post_kernel(y_ref, bonus_ref, g_ref, gng_ref, gnb_ref, o_ref):
    y = y_ref[...]
    ones_bd = _head_mask(MXU_DIM, BF16)
    inv_n = 1.0 / RWKV_HEAD
    mu = _head_sum(y, ones_bd) * inv_n
    yc = y - mu
    var = _head_sum(yc * yc, ones_bd) * inv_n
    yn = yc * lax.rsqrt(var + GN_EPS) * gng_ref[...] + gnb_ref[...]
    o_ref[...] = (yn + bonus_ref[...]) * g_ref[...]


def _post(y, bonus, g, lw):
    n, dr = y.shape
    tm = _row_tile(n, 512)
    row = pl.BlockSpec((tm, dr), lambda i: (i, 0))
    return pl.pallas_call(
        _post_kernel,
        grid=(n // tm,),
        in_specs=[row, row, row, _const_spec(lw["gn_g"].shape), _const_spec(lw["gn_b"].shape)],
        out_specs=row,
        out_shape=jax.ShapeDtypeStruct((n, dr), F32),
        compiler_params=_params("arbitrary"),
        name="wkv_post",
    )(y, bonus, g, lw["gn_g"], lw["gn_b"])


def _conv_seq_kernel(h_ref, halo_ref, w_ref, b_ref, lng_ref, lnb_ref, o_ref, hp_scr, *, seq_tiles):
    tt = h_ref.shape[0]
    first = (pl.program_id(0) % seq_tiles) == 0
    hp_scr[0:CONV_HALO, :] = jnp.where(first, 0.0, halo_ref[...])
    hp_scr[CONV_HALO:, :] = h_ref[...]
    off = CONV_HALO - (CONV_WIDTH - 1)
    rc = CONV_ROW_CHUNK if tt % CONV_ROW_CHUNK == 0 else tt
    for c0 in range(0, o_ref.shape[1], LANES):
        lanes = slice(c0, c0 + LANES)
        for t0 in range(0, tt, rc):
            acc = jnp.broadcast_to(b_ref[:, lanes], (rc, LANES))
            for p in range(SUBLANES):
                rows = rc if p == 0 else rc + SUBLANES
                part = None
                for m in range((CONV_HALO + SUBLANES) // SUBLANES):
                    j = SUBLANES * m + p - off
                    if 0 <= j < CONV_WIDTH:
                        term = hp_scr[t0 + SUBLANES * m:t0 + SUBLANES * m + rows, lanes] * w_ref[j:j + 1, lanes]
                        part = term if part is None else part + term
                acc = acc + part[p:p + rc]
            o_ref[t0:t0 + rc, lanes] = acc
    z = _layer_norm(o_ref[...], lng_ref[...], lnb_ref[...])
    o_ref[...] = z * jax.nn.sigmoid(z)


def _conv_seq(h, seq_len, lw):
    n, dc = h.shape
    tt = _row_tile(seq_len, 256)
    assert tt % CONV_HALO == 0
    seq_tiles = seq_len // tt
    row = pl.BlockSpec((tt, dc), lambda i: (i, 0))
    halo = pl.BlockSpec((CONV_HALO, dc), lambda i: (jnp.maximum(i * (tt // CONV_HALO) - 1, 0), 0))
    consts = [lw["conv_w"], lw["conv_b"], lw["conv_ln_g"], lw["conv_ln_b"]]
    return pl.pallas_call(
        functools.partial(_conv_seq_kernel, seq_tiles=seq_tiles),
        grid=(n // tt,),
        in_specs=[row, halo] + [_const_spec(c.shape) for c in consts],
        out_specs=row,
        out_shape=jax.ShapeDtypeStruct((n, dc), F32),
        scratch_shapes=[pltpu.VMEM((CONV_HALO + tt, dc), F32)],
        compiler_params=_params("arbitrary"),
        name="conv_seq",
    )(h, h, *consts)


def _conv_step_kernel(buf_ref, h_ref, w_ref, b_ref, lng_ref, lnb_ref, o_ref, nbuf_ref):
    h = h_ref[...]
    acc = b_ref[...] + h * w_ref[CONV_WIDTH - 1:CONV_WIDTH, :]
    for j in range(CONV_WIDTH - 1):
        tap = buf_ref[j]
        acc = acc + tap * w_ref[j:j + 1, :]
        if j > 0:
            nbuf_ref[j - 1] = tap
    nbuf_ref[CONV_WIDTH - 2] = h
    z = _layer_norm(acc, lng_ref[...], lnb_ref[...])
    o_ref[...] = z * jax.nn.sigmoid(z)


def _conv_step(buf, h, lw):
    nb, wm1, dc = buf.shape
    bb = _row_tile(nb, 32)
    bspec = pl.BlockSpec((wm1, bb, dc), lambda i: (0, i, 0))
    row = pl.BlockSpec((bb, dc), lambda i: (i, 0))
    consts = [lw["conv_w"], lw["conv_b"], lw["conv_ln_g"], lw["conv_ln_b"]]
    out, nbuf = pl.pallas_call(
        _conv_step_kernel,
        grid=(nb // bb,),
        in_specs=[bspec, row] + [_const_spec(c.shape) for c in consts],
        out_specs=[row, bspec],
        out_shape=[jax.ShapeDtypeStruct((nb, dc), F32), jax.ShapeDtypeStruct((wm1, nb, dc), F32)],
        compiler_params=_params("arbitrary"),
        name="conv_step",
    )(jnp.transpose(buf, (1, 0, 2)), h, *consts)
    return out, jnp.transpose(nbuf, (1, 0, 2))


def _outproj_kernel(*refs, alpha, n_dst):
    x_ref, yr_ref, yc_ref, wo_ref, g_ref, b_ref, wr_ref, br_ref, h_ref, logit_ref = refs[n_dst:]
    ymix = jnp.concatenate([yr_ref[...].astype(BF16), yc_ref[...].astype(BF16)], axis=1)
    h = _layer_norm(alpha * x_ref[...] + _dot(ymix, wo_ref[...]), g_ref[...], b_ref[...])
    h_ref[...] = h
    logit_ref[...] = _dot(h.astype(BF16), wr_ref[...]) + br_ref[...]


def _outproj(x, y_rwkv, y_conv, lw, alpha, n_total, row0, dst=()):
    n, d = x.shape
    dr = y_rwkv.shape[1]
    dc = y_conv.shape[1]
    ne = lw["w_router"].shape[1]
    tm = _row_tile(math.gcd(n, row0) if row0 else n, 256)
    blk0 = row0 // tm
    row = lambda c: pl.BlockSpec((tm, c), lambda i: (i, 0))
    out_row = lambda c: pl.BlockSpec((tm, c), lambda i: (blk0 + i, 0))
    consts = [lw["w_out"], lw["ln1_g"], lw["ln1_b"], lw["w_router"], lw["b_router"]]
    return pl.pallas_call(
        functools.partial(_outproj_kernel, alpha=alpha, n_dst=len(dst)),
        grid=(n // tm,),
        in_specs=[pl.BlockSpec(memory_space=pl.ANY)] * len(dst) + [row(d), row(dr), row(dc)]
        + [_const_spec(c.shape) for c in consts],
        out_specs=[out_row(d), out_row(ne)],
        out_shape=[jax.ShapeDtypeStruct((n_total, d), F32), jax.ShapeDtypeStruct((n_total, ne), F32)],
        input_output_aliases={i: i for i in range(len(dst))},
        compiler_params=_params("arbitrary"),
        name="outproj",
    )(*dst, x, y_rwkv, y_conv, *consts)


def _route(logits, moe_tm):
    n_tok, ne = logits.shape
    top_val, top_idx = lax.top_k(logits, TOP_K)
    gates = jax.nn.softmax(top_val, axis=-1)
    n_assign = n_tok * TOP_K
    e_flat = top_idx.reshape(n_assign).astype(jnp.int32)
    onehot = (e_flat[:, None] == jnp.arange(ne, dtype=jnp.int32)[None, :]).astype(jnp.int32)
    counts = onehot.sum(0)
    rank = jnp.take_along_axis(jnp.cumsum(onehot, axis=0), e_flat[:, None], axis=1)[:, 0] - 1
    nblk_e = (counts + moe_tm - 1) // moe_tm
    blk_end = jnp.cumsum(nblk_e)
    blk_start = blk_end - nblk_e
    n_active = blk_end[-1]
    nb_max = n_assign // moe_tm + ne
    pos = blk_start[e_flat] * moe_tm + rank
    sorted_tok = (jnp.argsort(e_flat, stable=True) // TOP_K).astype(jnp.int32)
    sorted_tok = jnp.concatenate([sorted_tok, jnp.zeros((SUBLANES,), jnp.int32)])
    cstart = jnp.cumsum(counts) - counts
    q = jnp.arange(nb_max, dtype=jnp.int32)
    q_eff = jnp.minimum(q, n_active - 1)
    blk_e = jnp.minimum(jnp.searchsorted(blk_end, q_eff, side="right"), ne - 1).astype(jnp.int32)
    row_in_e = (q_eff - blk_start[blk_e]) * moe_tm
    valid = jnp.clip(counts[blk_e] - row_in_e, 0, moe_tm)
    valid = jnp.where(q < n_active, valid, 0).astype(jnp.int32)
    src_start = (cstart[blk_e] + row_in_e).astype(jnp.int32)
    return gates, pos.astype(jnp.int32), sorted_tok, src_start, blk_e, q_eff.astype(jnp.int32), valid


def _gather_kernel(qeff_ref, valid_ref, src_ref, tok_ref, h_hbm, o_ref, buf, sem):
    q = pl.program_id(0)
    nvalid = valid_ref[q]

    def row_copy(r, tok):
        return pltpu.make_async_copy(h_hbm.at[pl.ds(tok, 1)], buf.at[pl.ds(r, 1)], sem)

    @pl.when(nvalid > 0)
    def _():
        base = src_ref[q]
        ngroups = (nvalid + SUBLANES - 1) // SUBLANES

        def issue(g, c):
            r0 = pl.multiple_of(g * SUBLANES, SUBLANES)
            for i in range(SUBLANES):
                row_copy(r0 + i, tok_ref[base + r0 + i]).start()
            return c

        lax.fori_loop(0, ngroups, issue, 0)

        def wait(g, c):
            r0 = pl.multiple_of(g * SUBLANES, SUBLANES)
            pltpu.make_async_copy(h_hbm.at[pl.ds(0, SUBLANES)], buf.at[pl.ds(r0, SUBLANES)], sem).wait()
            return c

        lax.fori_loop(0, ngroups, wait, 0)
        rows = lax.broadcasted_iota(jnp.int32, buf.shape, 0)
        o_ref[...] = jnp.where(rows < nvalid, buf[...], 0.0).astype(o_ref.dtype)


def _gather_rows(h, sorted_tok, src_start, q_eff, valid, moe_tm):
    n_tok, d = h.shape
    nb_max = q_eff.shape[0]
    grid_spec = pltpu.PrefetchScalarGridSpec(
        num_scalar_prefetch=4,
        grid=(nb_max,),
        in_specs=[pl.BlockSpec(memory_space=pl.ANY)],
        out_specs=pl.BlockSpec((moe_tm, d), lambda q, qe, va, sr, tk: (qe[q], 0)),
        scratch_shapes=[pltpu.VMEM((moe_tm, d), F32), pltpu.SemaphoreType.DMA(())],
    )
    return pl.pallas_call(
        _gather_kernel,
        grid_spec=grid_spec,
        out_shape=jax.ShapeDtypeStruct((nb_max * moe_tm, d), BF16),
        compiler_params=_params("arbitrary"),
        name="moe_gather",
    )(q_eff, valid, src_start, sorted_tok, h)


def _moe_kernel(be_ref, qeff_ref, valid_ref, x_ref, wg_ref, wl_ref, bg_ref, bl_ref, wd_ref, bd_ref, o_ref,
                wg_s, wl_s, wd_s, *, sub):
    q = pl.program_id(0)
    j = pl.program_id(1)
    nvalid = valid_ref[q]
    unit = sub // 2
    n_units = (nvalid + unit - 1) // unit
    n_all = o_ref.shape[0] // unit

    def fill(s, value):
        r0 = pl.multiple_of(s * unit, unit)
        o_ref[pl.ds(r0, unit), :] = jnp.broadcast_to(value, (unit, o_ref.shape[1]))

    def rows_block(r0, rows):
        x = x_ref[pl.ds(r0, rows), :]
        g = jnp.minimum(_dot(x, wg_s[...]) + bg_ref[0], SWIGLU_LIMIT)
        l = jnp.clip(_dot(x, wl_s[...]) + bl_ref[0], -SWIGLU_LIMIT, SWIGLU_LIMIT)
        act = g * jax.nn.sigmoid(SWIGLU_ALPHA * g) * (l + 1.0)
        o_ref[pl.ds(r0, rows), :] += _dot(act.astype(BF16), wd_s[...])

    @pl.when(nvalid > 0)
    def _():
        @pl.when(j == 0)
        def _():
            lax.fori_loop(0, n_units, lambda s, c: (fill(s, bd_ref[0]), c)[1], 0)
            lax.fori_loop(n_units, n_all, lambda s, c: (fill(s, jnp.zeros((1, 1), F32)), c)[1], 0)

        wg_s[...] = wg_ref[0].astype(BF16)
        wl_s[...] = wl_ref[0].astype(BF16)
        wd_s[...] = wd_ref[0].astype(BF16)
        rows_block(0, unit)
        rest = n_units - 1
        n_pairs = rest // 4
        tail = rest - n_pairs * 4

        def pair(i, c):
            r0 = pl.multiple_of(unit + i * 2 * sub, unit)
            rows_block(r0, sub)
            rows_block(r0 + sub, sub)
            return c

        lax.fori_loop(0, n_pairs, pair, 0)
        t0 = pl.multiple_of(unit + n_pairs * 2 * sub, unit)

        @pl.when(tail >= 2)
        def _():
            rows_block(t0, sub)

        @pl.when(tail % 2 == 1)
        def _():
            rows_block(pl.multiple_of(t0 + (tail // 2) * sub, unit), unit)


def _moe_experts(x_sorted, blk_e, q_eff, valid, lw, moe_tm, sub, tf):
    n_rows, d = x_sorted.shape
    ne, _, f2 = lw["w_gu"].shape
    f = f2 // 2
    nf = f // tf
    nb_max = q_eff.shape[0]
    b_gu = lw["b_gu"].reshape(ne, 1, f2)
    b_down = lw["b_down"].reshape(ne, 1, d)

    def jf(q, j, va):
        return jnp.where(va[q] > 0, j, nf - 1)

    grid_spec = pltpu.PrefetchScalarGridSpec(
        num_scalar_prefetch=3,
        grid=(nb_max, nf),
        in_specs=[
            pl.BlockSpec((moe_tm, d), lambda q, j, be, qe, va: (qe[q], 0)),
            pl.BlockSpec((1, d, tf), lambda q, j, be, qe, va: (be[q], 0, jf(q, j, va))),
            pl.BlockSpec((1, d, tf), lambda q, j, be, qe, va: (be[q], 0, nf + jf(q, j, va))),
            pl.BlockSpec((1, 1, tf), lambda q, j, be, qe, va: (be[q], 0, jf(q, j, va))),
            pl.BlockSpec((1, 1, tf), lambda q, j, be, qe, va: (be[q], 0, nf + jf(q, j, va))),
            pl.BlockSpec((1, tf, d), lambda q, j, be, qe, va: (be[q], jf(q, j, va), 0)),
            pl.BlockSpec((1, 1, d), lambda q, j, be, qe, va: (be[q], 0, 0)),
        ],
        out_specs=pl.BlockSpec((moe_tm, d), lambda q, j, be, qe, va: (qe[q], 0)),
        scratch_shapes=[pltpu.VMEM((d, tf), BF16), pltpu.VMEM((d, tf), BF16), pltpu.VMEM((tf, d), BF16)],
    )
    return pl.pallas_call(
        functools.partial(_moe_kernel, sub=sub),
        grid_spec=grid_spec,
        out_shape=jax.ShapeDtypeStruct((n_rows, d), F32),
        compiler_params=_params("arbitrary", "arbitrary", vmem_limit_bytes=MOE_VMEM_LIMIT_BYTES),
        name="moe_experts",
    )(blk_e, q_eff, valid, x_sorted, lw["w_gu"], lw["w_gu"], b_gu, b_gu, lw["w_down"], b_down)


def _combine_kernel(pos_ref, h_ref, gates_ref, yrows_hbm, g_ref, b_ref, op_ref, os_ref, buf, sem, *, alpha, n_first):
    i = pl.program_id(0)
    tm = h_ref.shape[0]

    def row_copy(r, k, p):
        return pltpu.make_async_copy(yrows_hbm.at[pl.ds(p, 1)], buf.at[k, pl.ds(r, 1)], sem)

    def issue(r2, c):
        for dr in range(2):
            r = r2 * 2 + dr
            for k in range(TOP_K):
                row_copy(r, k, pos_ref[(i * tm + r) * TOP_K + k]).start()
        return c

    lax.fori_loop(0, tm // 2, issue, 0)
    for k in range(TOP_K):
        pltpu.make_async_copy(yrows_hbm.at[pl.ds(0, tm)], buf.at[k], sem).wait()
    gates = gates_ref[...]
    ffn = buf[0] * gates[:, 0:1]
    for k in range(1, TOP_K):
        ffn = ffn + buf[k] * gates[:, k:k + 1]
    y = _layer_norm(alpha * h_ref[...] + ffn, g_ref[...], b_ref[...])

    @pl.when(i < n_first)
    def _():
        op_ref[...] = y

    @pl.when(i >= n_first)
    def _():
        os_ref[...] = y


def _combine(h, gates, pos, y_rows, lw, alpha, n_prompt):
    n_tok, d = h.shape
    tm = _row_tile(math.gcd(n_prompt, n_tok - n_prompt), 128)
    n_first = n_prompt // tm
    grid_spec = pltpu.PrefetchScalarGridSpec(
        num_scalar_prefetch=1,
        grid=(n_tok // tm,),
        in_specs=[
            pl.BlockSpec((tm, d), lambda i, p: (i, 0)),
            pl.BlockSpec((tm, TOP_K), lambda i, p: (i, 0)),
            pl.BlockSpec(memory_space=pl.ANY),
            pl.BlockSpec((1, d), lambda i, p: (0, 0)),
            pl.BlockSpec((1, d), lambda i, p: (0, 0)),
        ],
        out_specs=[pl.BlockSpec((tm, d), lambda i, p: (jnp.minimum(i, n_first - 1), 0)),
                   pl.BlockSpec((tm, d), lambda i, p: (jnp.maximum(i - n_first, 0), 0))],
        scratch_shapes=[pltpu.VMEM((TOP_K, tm, d), F32), pltpu.SemaphoreType.DMA(())],
    )
    return pl.pallas_call(
        functools.partial(_combine_kernel, alpha=alpha, n_first=n_first),
        grid_spec=grid_spec,
        out_shape=[jax.ShapeDtypeStruct((n_prompt, d), F32), jax.ShapeDtypeStruct((n_tok - n_prompt, d), F32)],
        compiler_params=_params("arbitrary"),
        name="moe_combine",
    )(pos, h, gates, y_rows, lw["ln2_g"], lw["ln2_b"])


def _moe_tiles(n_assign, ne, d, f):
    unit = MOE_SUB // 2
    mean = -(-n_assign // ne)
    tm = max(MOE_SUB, -(-(mean * 11 // 10) // unit) * unit)
    for tf in (512, 256):
        tf = min(tf, f)
        blocks = 2 * tm * d * 2 + 2 * tm * d * 4
        weights = 3 * d * tf * (2 * 4 + 2)
        temps = MOE_SUB * 2 * tf * 4
        if blocks + weights + temps <= MOE_VMEM_LIMIT_BYTES:
            return tm, tf
    return min(tm, 4 * MOE_SUB), min(256, f)


_VEC_PARAMS = ("mu_rkv", "w0", "a0", "k_k", "k_a", "r_k", "gn_g", "gn_b", "conv_b", "conv_ln_g", "conv_ln_b",
               "ln1_g", "ln1_b", "b_router", "ln2_g", "ln2_b")
_BF16_PARAMS = ("w_in", "w_A", "w_B", "a_A", "a_B", "g_A", "g_B", "w_out", "w_router")


def _diag_blocks(s):
    nb, ng = s.shape[:2]
    s6 = s.reshape(nb, ng, HEADS_PER_GROUP, RWKV_HEAD, HEADS_PER_GROUP, RWKV_HEAD)
    d = jnp.stack([s6[:, :, h, :, h, :] for h in range(HEADS_PER_GROUP)], axis=2)
    return d.reshape(nb, ng * HEADS_PER_GROUP, RWKV_HEAD, RWKV_HEAD)


def _layer(xp, xs, sx, srkv, swkv, sconv, lw, alpha):
    nbp, t, d = xp.shape
    nbs = xs.shape[0]
    xp2 = xp.reshape(nbp * t, d)
    xs2 = xs.reshape(nbs, d)

    up_last, hgp, dlp, gp, rp, kp, vp, ap, bp, bonp = _inproj(xp2, None, None, t, lw)
    us, hgs, dls, gs, rs, ks, vs, as_, bs, bons = _inproj(xs2, sx, srkv, 1, lw)
    dr = dlp.shape[1]

    seq = lambda z: z.reshape(nbp, t, dr)
    yp_raw, s_end = _scan(seq(rp), seq(dlp), seq(kp), seq(vp), seq(ap), seq(bp))
    s_new, ys_raw = _step(swkv, rs, dls, ks, vs, as_, bs)
    yrp = _post(yp_raw.reshape(nbp * t, dr), bonp, gp, lw)
    yrs = _post(ys_raw, bons, gs, lw)

    ycp = _conv_seq(hgp, t, lw)
    ycs, nbuf = _conv_step(sconv, hgs, lw)

    n_prompt = nbp * t
    n_tok = n_prompt + nbs
    dst = _outproj(xp2, yrp, ycp, lw, alpha, n_tok, 0)
    h_all, logits = _outproj(xs2, yrs, ycs, lw, alpha, n_tok, n_prompt, dst=tuple(dst))

    ne = logits.shape[1]
    moe_tm, moe_tf = _moe_tiles(n_tok * TOP_K, ne, d, lw["w_down"].shape[1])
    gates, pos, sorted_tok, src_start, blk_e, q_eff, valid = _route(logits, moe_tm)
    x_sorted = _gather_rows(h_all, sorted_tok, src_start, q_eff, valid, moe_tm)
    y_rows = _moe_experts(x_sorted, blk_e, q_eff, valid, lw, moe_tm, MOE_SUB, moe_tf)
    yp, ys = _combine(h_all, gates, pos, y_rows, lw, alpha, n_prompt)
    yp = yp.reshape(nbp, t, d)
    ys = ys.reshape(nbs, 1, d)
    p_state = (xp[:, -1], up_last.reshape(nbp, -1, SUBLANES, 3 * dr)[:, -1, -1], _diag_blocks(s_end),
               hgp.reshape(nbp, t, -1)[:, t - (CONV_WIDTH - 1):])
    s_state = (xs2, us, s_new, nbuf)
    return yp, ys, p_state, s_state


def kernel(x_prompt, x_sample, state_shift_x, state_shift_rkv, state_wkv, state_conv, w_in, mu_x, mu_rkv, w0, w_A,
           w_B, a0, a_A, a_B, g_A, g_B, k_k, k_a, r_k, gn_g, gn_b, conv_w, conv_b, conv_ln_g, conv_ln_b, w_out,
           ln1_g, ln1_b, w_router, b_router, w_gu, b_gu, w_down, b_down, ln2_g, ln2_b):
    params = dict(w_in=w_in, mu_x=mu_x, mu_rkv=mu_rkv, w0=w0, w_A=w_A, w_B=w_B, a0=a0, a_A=a_A, a_B=a_B, g_A=g_A,
                  g_B=g_B, k_k=k_k, k_a=k_a, r_k=r_k, gn_g=gn_g, gn_b=gn_b, conv_w=conv_w, conv_b=conv_b,
                  conv_ln_g=conv_ln_g, conv_ln_b=conv_ln_b, w_out=w_out, ln1_g=ln1_g, ln1_b=ln1_b,
                  w_router=w_router, b_router=b_router, w_gu=w_gu, b_gu=b_gu, w_down=w_down, b_down=b_down,
                  ln2_g=ln2_g, ln2_b=ln2_b)
    depth = w_in.shape[0]
    assert x_sample.shape[1] == 1, "the sample group advances one token per step"
    alpha = (2.0 * depth) ** 0.25
    xp, xs = x_prompt, x_sample
    p_states, s_states = [], []
    for l in range(depth):
        lw = {name: p[l] for name, p in params.items()}
        for name in _VEC_PARAMS:
            lw[name] = lw[name].reshape(1, -1)
        for name in _BF16_PARAMS:
            lw[name] = lw[name].astype(BF16)
        xp, xs, p_st, s_st = _layer(xp, xs, state_shift_x[l], state_shift_rkv[l], state_wkv[l], state_conv[l],
                                    lw, alpha)
        p_states.append(p_st)
        s_states.append(s_st)
    stack = lambda states, i: jnp.stack([st[i] for st in states])
    return (xp, xs,
            stack(p_states, 0), stack(p_states, 1), stack(p_states, 2), stack(p_states, 3),
            stack(s_states, 0), stack(s_states, 1), stack(s_states, 2), stack(s_states, 3))
```

```python
import functools
import math

import jax
import jax.numpy as jnp
from jax import lax
from jax.experimental import pallas as pl
from jax.experimental.pallas import tpu as pltpu

F32 = jnp.float32
BF16 = jnp.bfloat16

RWKV_HEAD = 64
CONV_WIDTH = 31
TOP_K = 4
SWIGLU_LIMIT = 7.0
SWIGLU_ALPHA = 1.702
LN_EPS = 1e-5
GN_EPS = 64e-5

LANES = 128
SUBLANES = 8
MXU_DIM = 256
VMEM_BYTES = 64 * 1024 * 1024
VMEM_LIMIT_BYTES = 56 * 1024 * 1024
MOE_VMEM_LIMIT_BYTES = VMEM_BYTES - 3 * 1024 * 1024

SCAN_CHUNK = 64
HEADS_PER_GROUP = MXU_DIM // RWKV_HEAD
CONV_HALO = 32
CONV_ROW_CHUNK = 128
GATHER_GROUP = 32
MOE_SUB = 256


def _row_tile(n, target):
    best = None
    for t in range(SUBLANES, min(n, target) + 1, SUBLANES):
        if n % t == 0:
            best = t
    assert best is not None, (n, target)
    return best


def _params(*sem, vmem_limit_bytes=VMEM_LIMIT_BYTES):
    return pltpu.CompilerParams(dimension_semantics=sem, vmem_limit_bytes=vmem_limit_bytes)


def _const_spec(shape):
    nd = len(shape)
    return pl.BlockSpec(shape, lambda *_: (0,) * nd, pipeline_mode=pl.Buffered(1))


def _dot(a, b):
    return jnp.dot(a, b, preferred_element_type=F32)


def _split2(x):
    hi = x.astype(BF16)
    lo = (x - hi.astype(F32)).astype(BF16)
    return hi, lo


def _layer_norm(z, g, b):
    mu = jnp.mean(z, axis=-1, keepdims=True)
    zc = z - mu
    var = jnp.mean(zc * zc, axis=-1, keepdims=True)
    return zc * lax.rsqrt(var + LN_EPS) * g + b


def _shift_rows(x, first_row):
    xs = pltpu.roll(x, 1, axis=0)
    row = lax.broadcasted_iota(jnp.int32, x.shape, 0)
    return jnp.where(row == 0, first_row, xs)


def _head_mask(n, dtype):
    r = lax.broadcasted_iota(jnp.int32, (n, n), 0) // RWKV_HEAD
    c = lax.broadcasted_iota(jnp.int32, (n, n), 1) // RWKV_HEAD
    return (r == c).astype(dtype)


def _head_sum(x, ones_bd):
    cols = []
    for c0 in range(0, x.shape[1], MXU_DIM):
        hi, lo = _split2(x[:, c0:c0 + MXU_DIM])
        cols.append(_dot(hi, ones_bd) + _dot(lo, ones_bd))
    return cols[0] if len(cols) == 1 else jnp.concatenate(cols, axis=1)


def _inproj_kernel(x_ref, xprev_ref, uprev_ref, w_in_ref, mu_ref, wA_ref, aA_ref, gA_ref, wB_ref, aB_ref, gB_ref,
                   w0_ref, a0_ref, murkv_ref, kk_ref, ka_ref, rk_ref,
                   ulast_ref, hglu_ref, dlog_ref, g_ref, r_ref, k_ref, v_ref, a_ref, b_ref, bonus_ref,
                   ucarry_scr, *, seq_tiles, halo):
    x = x_ref[...]
    tm = x.shape[0]
    first = (pl.program_id(0) % seq_tiles) == 0
    if halo:
        prev_row = jnp.where(first, 0.0, xprev_ref[SUBLANES - 1:SUBLANES, :])
        xprev = _shift_rows(x, prev_row)
    else:
        xprev = xprev_ref[...]
    xx = xprev - x
    xb = x.astype(BF16)
    d3 = murkv_ref.shape[1]
    dc = hglu_ref.shape[1]
    u = _dot(xb, w_in_ref[:, :d3])
    val = _dot(xb, w_in_ref[:, d3:d3 + dc])
    gate = _dot(xb, w_in_ref[:, d3 + dc:])
    hglu_ref[...] = val * jax.nn.sigmoid(gate)

    xw = (x + xx * mu_ref[0:1, :]).astype(BF16)
    tw = jnp.tanh(_dot(xw, wA_ref[...]))
    wlin = w0_ref[...] + _dot(tw.astype(BF16), wB_ref[...])
    wlog = -jax.nn.softplus(-wlin) - 0.5
    dlog_ref[...] = -jnp.exp(wlog)

    xa = (x + xx * mu_ref[1:2, :]).astype(BF16)
    ta = _dot(xa, aA_ref[...])
    asig = jax.nn.sigmoid(a0_ref[...] + _dot(ta.astype(BF16), aB_ref[...]))

    xg = (x + xx * mu_ref[2:3, :]).astype(BF16)
    tg = jax.nn.sigmoid(_dot(xg, gA_ref[...]))
    g_ref[...] = _dot(tg.astype(BF16), gB_ref[...])

    if halo:
        @pl.when(pl.program_id(0) == 0)
        def _():
            ucarry_scr[...] = jnp.zeros_like(ucarry_scr)

        uprev = _shift_rows(u, jnp.where(first, 0.0, ucarry_scr[0:1, :]))
        ucarry_scr[0:1, :] = u[tm - 1:tm, :]
        ulast_ref[...] = u[tm - SUBLANES:tm, :]
    else:
        uprev = uprev_ref[...]
        ulast_ref[...] = u
    rkv = u + (uprev - u) * murkv_ref[...]
    dr = r_ref.shape[1]
    r = rkv[:, :dr]
    k = rkv[:, dr:2 * dr]
    v = rkv[:, 2 * dr:]
    ones_bd = _head_mask(MXU_DIM, BF16)
    kk = k * kk_ref[...]
    kk = kk * lax.rsqrt(jnp.maximum(_head_sum(kk * kk, ones_bd), 1e-24))
    k = k * (1.0 + (asig - 1.0) * ka_ref[...])
    r_ref[...] = r
    k_ref[...] = k
    v_ref[...] = v
    a_ref[...] = -kk
    b_ref[...] = kk * asig
    bonus_ref[...] = _head_sum(r * k * rk_ref[...], ones_bd) * v


def _inproj(x, xprev, uprev, seq_len, lw):
    n, d = x.shape
    d_in = lw["w_in"].shape[1]
    dr = lw["w_B"].shape[1]
    d3 = 3 * dr
    dc = (d_in - d3) // 2
    halo = xprev is None
    tm = _row_tile(seq_len if halo else n, 256)
    seq_tiles = (seq_len // tm) if halo else 1
    row = lambda c: pl.BlockSpec((tm, c), lambda i: (i, 0))
    if halo:
        prev_specs = [pl.BlockSpec((SUBLANES, d), lambda i: (jnp.maximum(i * (tm // SUBLANES) - 1, 0), 0)),
                      pl.BlockSpec((SUBLANES, d3), lambda i: (0, 0))]
        prev_args = [x, jnp.zeros((SUBLANES, d3), F32)]
        ulast_spec = pl.BlockSpec((SUBLANES, d3), lambda i: (i, 0))
        ulast_shape = jax.ShapeDtypeStruct((n // tm * SUBLANES, d3), F32)
    else:
        prev_specs = [row(d), row(d3)]
        prev_args = [xprev, uprev]
        ulast_spec = row(d3)
        ulast_shape = jax.ShapeDtypeStruct((n, d3), F32)
    consts = [lw["w_in"], lw["mu_x"], lw["w_A"], lw["a_A"], lw["g_A"], lw["w_B"], lw["a_B"], lw["g_B"],
              lw["w0"], lw["a0"], lw["mu_rkv"], lw["k_k"], lw["k_a"], lw["r_k"]]
    widths = (dc, dr, dr) + (dr,) * 6
    return pl.pallas_call(
        functools.partial(_inproj_kernel, seq_tiles=seq_tiles, halo=halo),
        grid=(n // tm,),
        in_specs=[row(d)] + prev_specs + [_const_spec(c.shape) for c in consts],
        out_specs=[ulast_spec] + [row(c) for c in widths],
        out_shape=[ulast_shape] + [jax.ShapeDtypeStruct((n, c), F32) for c in widths],
        scratch_shapes=[pltpu.VMEM((SUBLANES, d3), F32)],
        compiler_params=_params("arbitrary"),
        name="inproj",
    )(x, *prev_args, *consts)


def _blockdiag(x, mask):
    reps = mask.shape[0] // x.shape[0]
    return jnp.concatenate([x] * reps, axis=0) * mask


def _dot_nt(a, b):
    return lax.dot_general(a, b, (((1,), (1,)), ((), ())), preferred_element_type=F32)


def _dot_tn(a, b):
    return lax.dot_general(a, b, (((0,), (0,)), ((), ())), preferred_element_type=F32)


def _scan_kernel(r_ref, dl_ref, k_ref, v_ref, a_ref, b_ref, y_ref, s_ref, h_scr):
    c = pl.program_id(0)
    nb = r_ref.shape[0]
    C = r_ref.shape[1]
    W = MXU_DIM
    ng = r_ref.shape[2] // W

    @pl.when(c == 0)
    def _():
        h_scr[...] = jnp.zeros_like(h_scr)

    ti = lax.broadcasted_iota(jnp.int32, (C, C), 0)
    tj = lax.broadcasted_iota(jnp.int32, (C, C), 1)
    tri = (ti >= tj).astype(BF16)
    t_row = lax.broadcasted_iota(jnp.int32, (C, W), 0)
    j_col = lax.broadcasted_iota(jnp.int32, (C, W), 1) % C
    strict = j_col < t_row
    incl = j_col <= t_row
    eye_cat = (j_col == t_row).astype(F32)
    rb = lax.broadcasted_iota(jnp.int32, (HEADS_PER_GROUP * C, W), 0) // C
    cb = lax.broadcasted_iota(jnp.int32, (HEADS_PER_GROUP * C, W), 1) // RWKV_HEAD
    bmask = (rb == cb).astype(BF16)
    hmask = _head_mask(W, F32)

    chains = [(bi, gi) for bi in range(nb) for gi in range(ng)]
    each = lambda f, *lists: [f(*xs) for xs in zip(*lists)]
    load = lambda ref: [ref[bi, :, gi * W:(gi + 1) * W] for bi, gi in chains]
    r, dl, k, v, a, b = (load(ref) for ref in (r_ref, dl_ref, k_ref, v_ref, a_ref, b_ref))
    sbd = [h_scr[bi, gi] for bi, gi in chains]
    bd = lambda x: _blockdiag(x, bmask)

    def cumsum(d):
        d_hi = d.astype(BF16)
        d_r1 = d - d_hi.astype(F32)
        d_mid = d_r1.astype(BF16)
        d_lo = (d_r1 - d_mid.astype(F32)).astype(BF16)
        return _dot(tri, d_hi) + (_dot(tri, d_mid) + _dot(tri, d_lo))

    cum = each(cumsum, dl)
    cum_last = each(lambda x: x[C - 1:C, :], cum)
    e_neg = each(lambda x: jnp.exp(-x), cum)
    e_end = each(lambda x, xl: jnp.exp(xl - x), cum, cum_last)
    at = each(lambda x, cu, d: (x * jnp.exp(cu - d)).astype(BF16), a, cum, dl)
    rt = each(lambda x, cu: (x * jnp.exp(cu)).astype(BF16), r, cum)
    bt = each(lambda x, e: (x * e).astype(BF16), b, e_neg)
    kt = each(lambda x, e: (x * e).astype(BF16), k, e_neg)
    bh = each(lambda x, e: (x * e).astype(BF16), b, e_end)
    kh = each(lambda x, e: (x * e).astype(BF16), k, e_end)
    vb = each(lambda x: x.astype(BF16), v)
    ar = each(lambda x, y: jnp.concatenate([x, y], axis=0), at, rt)

    p_b = each(lambda x, y: _dot_nt(x, bd(y)), ar, bt)
    p_k = each(lambda x, y: _dot_nt(x, bd(y)), ar, kt)
    arh = each(lambda x, s: _dot_nt(x, s.astype(BF16)), ar, sbd)
    p_ab = each(lambda p: jnp.where(strict, p[:C], 0.0), p_b)
    p_rb = each(lambda p: jnp.where(incl, p[C:], 0.0).astype(BF16), p_b)
    p_akrk = each(lambda p: jnp.concatenate([jnp.where(strict, p[:C], 0.0), jnp.where(incl, p[C:], 0.0)],
                                            axis=0).astype(BF16), p_k)

    pv = each(lambda p, x: _dot(p, bd(x)), p_akrk, vb)
    n_sq = int(math.log2(C))
    nn_b = each(lambda p: p.astype(BF16), p_ab)
    nn_b = each(lambda n: _dot(n, bd(n)).astype(BF16), nn_b)
    w = each(lambda x, y: (x[:C] + y[:C]).astype(BF16), arh, pv)
    tm = each(lambda p: eye_cat + p, p_ab)
    for i in range(1, n_sq):
        last = i == n_sq - 1
        lhs = each(lambda t, n: t.astype(BF16) if last else jnp.concatenate([t.astype(BF16), n], axis=0), tm, nn_b)
        prod = each(lambda l, n: _dot(l, bd(n)), lhs, nn_b)
        tm = each(lambda t, p: t + p[:C], tm, prod)
        if not last:
            nn_b = each(lambda p: p[C:].astype(BF16), prod)
    ub = each(lambda t, x: _dot(t.astype(BF16), bd(x)).astype(BF16), tm, w)

    yv = each(lambda x, y, p, u: x[C:] + y[C:] + _dot(p, bd(u)), arh, pv, p_rb, ub)
    upd = each(lambda u, x, y, z: _dot_tn(jnp.concatenate([u, x], axis=0), jnp.concatenate([y, z], axis=0)),
               ub, vb, bh, kh)
    for (bi, gi), y, s, xl, up in zip(chains, yv, sbd, cum_last, upd):
        y_ref[bi, :, gi * W:(gi + 1) * W] = y
        h_scr[bi, gi] = (s * jnp.exp(xl) + up) * hmask

    @pl.when(c == pl.num_programs(0) - 1)
    def _():
        s_ref[...] = h_scr[...]


def _scan(r, dl, k, v, a, b):
    nb, t, dr = r.shape
    ng = dr // MXU_DIM
    C = SCAN_CHUNK
    spec = pl.BlockSpec((nb, C, dr), lambda c: (0, c, 0))
    return pl.pallas_call(
        _scan_kernel,
        grid=(t // C,),
        in_specs=[spec] * 6,
        out_specs=[spec, pl.BlockSpec((nb, ng, MXU_DIM, MXU_DIM), lambda c: (0, 0, 0, 0))],
        out_shape=[jax.ShapeDtypeStruct((nb, t, dr), F32),
                   jax.ShapeDtypeStruct((nb, ng, MXU_DIM, MXU_DIM), F32)],
        scratch_shapes=[pltpu.VMEM((nb, ng, MXU_DIM, MXU_DIM), F32)],
        compiler_params=_params("arbitrary"),
        name="wkv_scan",
    )(r, dl, k, v, a, b)


def _step_kernel(s_ref, r_ref, dl_ref, k_ref, v_ref, a_ref, b_ref, sout_ref, y_ref):
    S = s_ref[0]
    per_key = lambda ref: ref[...][None, :, :]
    sa = jnp.sum(S * per_key(a_ref), axis=1, keepdims=True)
    v = v_ref[...][:, None, :]
    s_new = S * jnp.exp(per_key(dl_ref)) + sa * per_key(b_ref) + v * per_key(k_ref)
    sout_ref[0] = s_new
    y_ref[...] = jnp.sum(s_new * per_key(r_ref), axis=1)


def _step(S0, r, dl, k, v, a, b):
    nb, nh, n, _ = S0.shape
    st = jnp.transpose(S0, (1, 2, 3, 0))
    sspec = pl.BlockSpec((1, n, n, nb), lambda h: (h, 0, 0, 0))
    vspec = pl.BlockSpec((n, nb), lambda h: (h, 0))
    s_new, y = pl.pallas_call(
        _step_kernel,
        grid=(nh,),
        in_specs=[sspec] + [vspec] * 6,
        out_specs=[sspec, vspec],
        out_shape=[jax.ShapeDtypeStruct(st.shape, F32), jax.ShapeDtypeStruct((nh * n, nb), F32)],
        compiler_params=_params("arbitrary"),
        name="wkv_step",
    )(st, *(x.T for x in (r, dl, k, v, a, b)))
    return jnp.transpose(s_new, (3, 0, 1, 2)), y.T


def _post_kernel(y_ref, bonus_ref, g_ref, gng_ref, gnb_ref, o_ref):
    y = y_ref[...]
    ones_bd = _head_mask(MXU_DIM, BF16)
    inv_n = 1.0 / RWKV_HEAD
    mu = _head_sum(y, ones_bd) * inv_n
    yc = y - mu
    var = _head_sum(yc * yc, ones_bd) * inv_n
    yn = yc * lax.rsqrt(var + GN_EPS) * gng_ref[...] + gnb_ref[...]
    o_ref[...] = (yn + bonus_ref[...]) * g_ref[...]


def _post(y, bonus, g, lw):
    n, dr = y.shape
    tm = _row_tile(n, 512)
    row = pl.BlockSpec((tm, dr), lambda i: (i, 0))
    return pl.pallas_call(
        _post_kernel,
        grid=(n // tm,),
        in_specs=[row, row, row, _const_spec(lw["gn_g"].shape), _const_spec(lw["gn_b"].shape)],
        out_specs=row,
        out_shape=jax.ShapeDtypeStruct((n, dr), F32),
        compiler_params=_params("arbitrary"),
        name="wkv_post",
    )(y, bonus, g, lw["gn_g"], lw["gn_b"])


def _conv_seq_kernel(h_ref, halo_ref, w_ref, b_ref, lng_ref, lnb_ref, o_ref, hp_scr, *, seq_tiles):
    tt = h_ref.shape[0]
    first = (pl.program_id(0) % seq_tiles) == 0
    hp_scr[0:CONV_HALO, :] = jnp.where(first, 0.0, halo_ref[...])
    hp_scr[CONV_HALO:, :] = h_ref[...]
    off = CONV_HALO - (CONV_WIDTH - 1)
    rc = CONV_ROW_CHUNK if tt % CONV_ROW_CHUNK == 0 else tt
    for c0 in range(0, o_ref.shape[1], LANES):
        lanes = slice(c0, c0 + LANES)
        for t0 in range(0, tt, rc):
            acc = jnp.broadcast_to(b_ref[:, lanes], (rc, LANES))
            for p in range(SUBLANES):
                rows = rc if p == 0 else rc + SUBLANES
                part = None
                for m in range((CONV_HALO + SUBLANES) // SUBLANES):
                    j = SUBLANES * m + p - off
                    if 0 <= j < CONV_WIDTH:
                        term = hp_scr[t0 + SUBLANES * m:t0 + SUBLANES * m + rows, lanes] * w_ref[j:j + 1, lanes]
                        part = term if part is None else part + term
                acc = acc + part[p:p + rc]
            o_ref[t0:t0 + rc, lanes] = acc
    z = _layer_norm(o_ref[...], lng_ref[...], lnb_ref[...])
    o_ref[...] = z * jax.nn.sigmoid(z)


def _conv_seq(h, seq_len, lw):
    n, dc = h.shape
    tt = _row_tile(seq_len, 256)
    assert tt % CONV_HALO == 0
    seq_tiles = seq_len // tt
    row = pl.BlockSpec((tt, dc), lambda i: (i, 0))
    halo = pl.BlockSpec((CONV_HALO, dc), lambda i: (jnp.maximum(i * (tt // CONV_HALO) - 1, 0), 0))
    consts = [lw["conv_w"], lw["conv_b"], lw["conv_ln_g"], lw["conv_ln_b"]]
    return pl.pallas_call(
        functools.partial(_conv_seq_kernel, seq_tiles=seq_tiles),
        grid=(n // tt,),
        in_specs=[row, halo] + [_const_spec(c.shape) for c in consts],
        out_specs=row,
        out_shape=jax.ShapeDtypeStruct((n, dc), F32),
        scratch_shapes=[pltpu.VMEM((CONV_HALO + tt, dc), F32)],
        compiler_params=_params("arbitrary"),
        name="conv_seq",
    )(h, h, *consts)


def _conv_step_kernel(buf_ref, h_ref, w_ref, b_ref, lng_ref, lnb_ref, o_ref, nbuf_ref):
    h = h_ref[...]
    acc = b_ref[...] + h * w_ref[CONV_WIDTH - 1:CONV_WIDTH, :]
    for j in range(CONV_WIDTH - 1):
        tap = buf_ref[j]
        acc = acc + tap * w_ref[j:j + 1, :]
        if j > 0:
            nbuf_ref[j - 1] = tap
    nbuf_ref[CONV_WIDTH - 2] = h
    z = _layer_norm(acc, lng_ref[...], lnb_ref[...])
    o_ref[...] = z * jax.nn.sigmoid(z)


def _conv_step(buf, h, lw):
    nb, wm1, dc = buf.shape
    bb = _row_tile(nb, 32)
    bspec = pl.BlockSpec((wm1, bb, dc), lambda i: (0, i, 0))
    row = pl.BlockSpec((bb, dc), lambda i: (i, 0))
    consts = [lw["conv_w"], lw["conv_b"], lw["conv_ln_g"], lw["conv_ln_b"]]
    out, nbuf = pl.pallas_call(
        _conv_step_kernel,
        grid=(nb // bb,),
        in_specs=[bspec, row] + [_const_spec(c.shape) for c in consts],
        out_specs=[row, bspec],
        out_shape=[jax.ShapeDtypeStruct((nb, dc), F32), jax.ShapeDtypeStruct((wm1, nb, dc), F32)],
        compiler_params=_params("arbitrary"),
        name="conv_step",
    )(jnp.transpose(buf, (1, 0, 2)), h, *consts)
    return out, jnp.transpose(nbuf, (1, 0, 2))


def _outproj_kernel(*refs, alpha, n_dst):
    x_ref, yr_ref, yc_ref, wo_ref, g_ref, b_ref, wr_ref, br_ref, h_ref, logit_ref = refs[n_dst:]
    ymix = jnp.concatenate([yr_ref[...].astype(BF16), yc_ref[...].astype(BF16)], axis=1)
    h = _layer_norm(alpha * x_ref[...] + _dot(ymix, wo_ref[...]), g_ref[...], b_ref[...])
    h_ref[...] = h
    logit_ref[...] = _dot(h.astype(BF16), wr_ref[...]) + br_ref[...]


def _outproj(x, y_rwkv, y_conv, lw, alpha, n_total, row0, dst=()):
    n, d = x.shape
    dr = y_rwkv.shape[1]
    dc = y_conv.shape[1]
    ne = lw["w_router"].shape[1]
    tm = _row_tile(math.gcd(n, row0) if row0 else n, 256)
    blk0 = row0 // tm
    row = lambda c: pl.BlockSpec((tm, c), lambda i: (i, 0))
    out_row = lambda c: pl.BlockSpec((tm, c), lambda i: (blk0 + i, 0))
    consts = [lw["w_out"], lw["ln1_g"], lw["ln1_b"], lw["w_router"], lw["b_router"]]
    return pl.pallas_call(
        functools.partial(_outproj_kernel, alpha=alpha, n_dst=len(dst)),
        grid=(n // tm,),
        in_specs=[pl.BlockSpec(memory_space=pl.ANY)] * len(dst) + [row(d), row(dr), row(dc)]
        + [_const_spec(c.shape) for c in consts],
        out_specs=[out_row(d), out_row(ne)],
        out_shape=[jax.ShapeDtypeStruct((n_total, d), F32), jax.ShapeDtypeStruct((n_total, ne), F32)],
        input_output_aliases={i: i for i in range(len(dst))},
        compiler_params=_params("arbitrary"),
        name="outproj",
    )(*dst, x, y_rwkv, y_conv, *consts)


def _route(logits, moe_tm):
    n_tok, ne = logits.shape
    top_val, top_idx = lax.top_k(logits, TOP_K)
    gates = jax.nn.softmax(top_val, axis=-1)
    n_assign = n_tok * TOP_K
    e_flat = top_idx.reshape(n_assign).astype(jnp.int32)
    onehot = (e_flat[:, None] == jnp.arange(ne, dtype=jnp.int32)[None, :]).astype(jnp.int32)
    counts = onehot.sum(0)
    rank = jnp.take_along_axis(jnp.cumsum(onehot, axis=0), e_flat[:, None], axis=1)[:, 0] - 1
    nblk_e = (counts + moe_tm - 1) // moe_tm
    blk_end = jnp.cumsum(nblk_e)
    blk_start = blk_end - nblk_e
    n_active = blk_end[-1]
    nb_max = n_assign // moe_tm + ne
    pos = blk_start[e_flat] * moe_tm + rank
    sorted_tok = (jnp.argsort(e_flat, stable=True) // TOP_K).astype(jnp.int32)
    sorted_tok = jnp.concatenate([sorted_tok, jnp.zeros((GATHER_GROUP,), jnp.int32)])
    cstart = jnp.cumsum(counts) - counts
    q = jnp.arange(nb_max, dtype=jnp.int32)
    q_eff = jnp.minimum(q, n_active - 1)
    blk_e = jnp.minimum(jnp.searchsorted(blk_end, q_eff, side="right"), ne - 1).astype(jnp.int32)
    row_in_e = (q_eff - blk_start[blk_e]) * moe_tm
    valid = jnp.clip(counts[blk_e] - row_in_e, 0, moe_tm)
    valid = jnp.where(q < n_active, valid, 0).astype(jnp.int32)
    src_start = (cstart[blk_e] + row_in_e).astype(jnp.int32)
    return gates, pos.astype(jnp.int32), sorted_tok, src_start, blk_e, q_eff.astype(jnp.int32), valid


def _gather_kernel(qeff_ref, valid_ref, src_ref, tok_ref, h_hbm, o_ref, buf, sem):
    q = pl.program_id(0)
    nvalid = valid_ref[q]

    @pl.when(nvalid > 0)
    def _():
        base = src_ref[q]
        per = GATHER_GROUP // SUBLANES
        ngroups = (nvalid + GATHER_GROUP - 1) // GATHER_GROUP

        def issue(g, c):
            for i in range(GATHER_GROUP):
                tok = tok_ref[base + g * GATHER_GROUP + i]
                pltpu.make_async_copy(h_hbm.at[pl.ds(tok, 1)],
                                      buf.at[g * per + i // SUBLANES, pl.ds(i % SUBLANES, 1)], sem).start()
            return c

        lax.fori_loop(0, ngroups, issue, 0)

        def wait(g, c):
            pltpu.make_async_copy(h_hbm.at[pl.ds(0, SUBLANES)], buf.at[g], sem).wait()
            return c

        lax.fori_loop(0, ngroups * per, wait, 0)
        rows = lax.broadcasted_iota(jnp.int32, o_ref.shape, 0)
        o_ref[...] = jnp.where(rows < nvalid, buf[...].reshape(o_ref.shape), 0.0).astype(o_ref.dtype)


def _gather_rows(h, sorted_tok, src_start, q_eff, valid, moe_tm):
    n_tok, d = h.shape
    nb_max = q_eff.shape[0]
    grid_spec = pltpu.PrefetchScalarGridSpec(
        num_scalar_prefetch=4,
        grid=(nb_max,),
        in_specs=[pl.BlockSpec(memory_space=pl.ANY)],
        out_specs=pl.BlockSpec((moe_tm, d), lambda q, qe, va, sr, tk: (qe[q], 0)),
        scratch_shapes=[pltpu.VMEM((moe_tm // SUBLANES, SUBLANES, d), F32), pltpu.SemaphoreType.DMA(())],
    )
    return pl.pallas_call(
        _gather_kernel,
        grid_spec=grid_spec,
        out_shape=jax.ShapeDtypeStruct((nb_max * moe_tm, d), BF16),
        compiler_params=_params("arbitrary"),
        name="moe_gather",
    )(q_eff, valid, src_start, sorted_tok, h)


def _moe_kernel(be_ref, qeff_ref, valid_ref, x_ref, wg_ref, wl_ref, bg_ref, bl_ref, wd_ref, bd_ref, o_ref,
                wg_s, wl_s, wd_s, *, sub):
    q = pl.program_id(0)
    j = pl.program_id(1)
    nvalid = valid_ref[q]
    unit = sub // 2
    n_units = (nvalid + unit - 1) // unit
    n_all = o_ref.shape[0] // unit

    def fill(s, value):
        r0 = pl.multiple_of(s * unit, unit)
        o_ref[pl.ds(r0, unit), :] = jnp.broadcast_to(value, (unit, o_ref.shape[1]))

    def rows_block(r0, rows):
        x = x_ref[pl.ds(r0, rows), :]
        g = jnp.minimum(_dot(x, wg_s[...]) + bg_ref[0], SWIGLU_LIMIT)
        l = jnp.clip(_dot(x, wl_s[...]) + bl_ref[0], -SWIGLU_LIMIT, SWIGLU_LIMIT)
        act = g * jax.nn.sigmoid(SWIGLU_ALPHA * g) * (l + 1.0)
        o_ref[pl.ds(r0, rows), :] += _dot(act.astype(BF16), wd_s[...])

    @pl.when(nvalid > 0)
    def _():
        @pl.when(j == 0)
        def _():
            lax.fori_loop(0, n_units, lambda s, c: (fill(s, bd_ref[0]), c)[1], 0)
            lax.fori_loop(n_units, n_all, lambda s, c: (fill(s, jnp.zeros((1, 1), F32)), c)[1], 0)

        wg_s[...] = wg_ref[0].astype(BF16)
        wl_s[...] = wl_ref[0].astype(BF16)
        wd_s[...] = wd_ref[0].astype(BF16)
        rows_block(0, unit)
        rest = n_units - 1
        n_pairs = rest // 4
        tail = rest - n_pairs * 4

        def pair(i, c):
            r0 = pl.multiple_of(unit + i * 2 * sub, unit)
            rows_block(r0, sub)
            rows_block(r0 + sub, sub)
            return c

        lax.fori_loop(0, n_pairs, pair, 0)
        t0 = pl.multiple_of(unit + n_pairs * 2 * sub, unit)

        @pl.when(tail >= 2)
        def _():
            rows_block(t0, sub)

        @pl.when(tail % 2 == 1)
        def _():
            rows_block(pl.multiple_of(t0 + (tail // 2) * sub, unit), unit)


def _moe_experts(x_sorted, blk_e, q_eff, valid, lw, moe_tm, sub, tf):
    n_rows, d = x_sorted.shape
    ne, _, f2 = lw["w_gu"].shape
    f = f2 // 2
    nf = f // tf
    nb_max = q_eff.shape[0]
    b_gu = lw["b_gu"].reshape(ne, 1, f2)
    b_down = lw["b_down"].reshape(ne, 1, d)

    def jf(q, j, va):
        return jnp.where(va[q] > 0, j, nf - 1)

    grid_spec = pltpu.PrefetchScalarGridSpec(
        num_scalar_prefetch=3,
        grid=(nb_max, nf),
        in_specs=[
            pl.BlockSpec((moe_tm, d), lambda q, j, be, qe, va: (qe[q], 0)),
            pl.BlockSpec((1, d, tf), lambda q, j, be, qe, va: (be[q], 0, jf(q, j, va))),
            pl.BlockSpec((1, d, tf), lambda q, j, be, qe, va: (be[q], 0, nf + jf(q, j, va))),
            pl.BlockSpec((1, 1, tf), lambda q, j, be, qe, va: (be[q], 0, jf(q, j, va))),
            pl.BlockSpec((1, 1, tf), lambda q, j, be, qe, va: (be[q], 0, nf + jf(q, j, va))),
            pl.BlockSpec((1, tf, d), lambda q, j, be, qe, va: (be[q], jf(q, j, va), 0)),
            pl.BlockSpec((1, 1, d), lambda q, j, be, qe, va: (be[q], 0, 0)),
        ],
        out_specs=pl.BlockSpec((moe_tm, d), lambda q, j, be, qe, va: (qe[q], 0)),
        scratch_shapes=[pltpu.VMEM((d, tf), BF16), pltpu.VMEM((d, tf), BF16), pltpu.VMEM((tf, d), BF16)],
    )
    return pl.pallas_call(
        functools.partial(_moe_kernel, sub=sub),
        grid_spec=grid_spec,
        out_shape=jax.ShapeDtypeStruct((n_rows, d), F32),
        compiler_params=_params("arbitrary", "arbitrary", vmem_limit_bytes=MOE_VMEM_LIMIT_BYTES),
        name="moe_experts",
    )(blk_e, q_eff, valid, x_sorted, lw["w_gu"], lw["w_gu"], b_gu, b_gu, lw["w_down"], b_down)


def _combine_kernel(pos_ref, h_ref, gates_ref, yrows_hbm, g_ref, b_ref, op_ref, os_ref, buf0, buf1, sem0, sem1, *,
                    alpha, n_first, n_tiles):
    i = pl.program_id(0)
    tm = h_ref.shape[0]
    slots = ((buf0, sem0), (buf1, sem1))

    def fetch(tile, buf, sem):
        for r in range(tm):
            for k in range(TOP_K):
                p = pos_ref[(tile * tm + r) * TOP_K + k]
                pltpu.make_async_copy(yrows_hbm.at[pl.ds(p, 1)],
                                      buf.at[k, r // SUBLANES, pl.ds(r % SUBLANES, 1)], sem).start()

    def wait(buf, sem):
        for k in range(TOP_K):
            pltpu.make_async_copy(buf.at[k], buf.at[k], sem).wait()

    def finish(tile, buf):
        gates = gates_ref[...]
        slot = lambda k: buf[k].reshape(tm, buf.shape[-1])
        ffn = slot(0) * gates[:, 0:1]
        for k in range(1, TOP_K):
            ffn = ffn + slot(k) * gates[:, k:k + 1]
        y = _layer_norm(alpha * h_ref[...] + ffn, g_ref[...], b_ref[...])

        @pl.when(tile < n_first)
        def _():
            op_ref[...] = y

        @pl.when(tile >= n_first)
        def _():
            os_ref[...] = y

    @pl.when(i == 0)
    def _():
        fetch(0, *slots[0])

    for parity in range(2):
        @pl.when((i >= 1) & (i < n_tiles) & (i % 2 == parity))
        def _():
            wait(*slots[1 - parity])
            fetch(i, *slots[parity])
            finish(i - 1, slots[1 - parity][0])

    @pl.when(i == n_tiles)
    def _():
        last = slots[(n_tiles - 1) % 2]
        wait(*last)
        finish(i - 1, last[0])


def _combine(h, gates, pos, y_rows, lw, alpha, n_prompt):
    n_tok, d = h.shape
    tm = _row_tile(math.gcd(n_prompt, n_tok - n_prompt), 128)
    n_first = n_prompt // tm
    n_tiles = n_tok // tm
    done = lambda i: jnp.maximum(i - 1, 0)
    buf = pltpu.VMEM((TOP_K, tm // SUBLANES, SUBLANES, d), F32)
    grid_spec = pltpu.PrefetchScalarGridSpec(
        num_scalar_prefetch=1,
        grid=(n_tiles + 1,),
        in_specs=[
            pl.BlockSpec((tm, d), lambda i, p: (done(i), 0)),
            pl.BlockSpec((tm, TOP_K), lambda i, p: (done(i), 0)),
            pl.BlockSpec(memory_space=pl.ANY),
            pl.BlockSpec((1, d), lambda i, p: (0, 0)),
            pl.BlockSpec((1, d), lambda i, p: (0, 0)),
        ],
        out_specs=[pl.BlockSpec((tm, d), lambda i, p: (jnp.minimum(done(i), n_first - 1), 0)),
                   pl.BlockSpec((tm, d), lambda i, p: (jnp.maximum(done(i) - n_first, 0), 0))],
        scratch_shapes=[buf, buf, pltpu.SemaphoreType.DMA(()), pltpu.SemaphoreType.DMA(())],
    )
    return pl.pallas_call(
        functools.partial(_combine_kernel, alpha=alpha, n_first=n_first, n_tiles=n_tiles),
        grid_spec=grid_spec,
        out_shape=[jax.ShapeDtypeStruct((n_prompt, d), F32), jax.ShapeDtypeStruct((n_tok - n_prompt, d), F32)],
        compiler_params=_params("arbitrary"),
        name="moe_combine",
    )(pos, h, gates, y_rows, lw["ln2_g"], lw["ln2_b"])


def _moe_tiles(n_assign, ne, d, f):
    unit = MOE_SUB // 2
    mean = -(-n_assign // ne)
    tm = max(MOE_SUB, -(-(mean * 11 // 10) // unit) * unit)
    for tf in (512, 256):
        tf = min(tf, f)
        blocks = 2 * tm * d * 2 + 2 * tm * d * 4
        weights = 3 * d * tf * (2 * 4 + 2)
        temps = MOE_SUB * 2 * tf * 4
        if blocks + weights + temps <= MOE_VMEM_LIMIT_BYTES:
            return tm, tf
    return min(tm, 4 * MOE_SUB), min(256, f)


_VEC_PARAMS = ("mu_rkv", "w0", "a0", "k_k", "k_a", "r_k", "gn_g", "gn_b", "conv_b", "conv_ln_g", "conv_ln_b",
               "ln1_g", "ln1_b", "b_router", "ln2_g", "ln2_b")
_BF16_PARAMS = ("w_in", "w_A", "w_B", "a_A", "a_B", "g_A", "g_B", "w_out", "w_router")


def _diag_blocks(s):
    nb, ng = s.shape[:2]
    s6 = s.reshape(nb, ng, HEADS_PER_GROUP, RWKV_HEAD, HEADS_PER_GROUP, RWKV_HEAD)
    d = jnp.stack([s6[:, :, h, :, h, :] for h in range(HEADS_PER_GROUP)], axis=2)
    return d.reshape(nb, ng * HEADS_PER_GROUP, RWKV_HEAD, RWKV_HEAD)


def _layer(xp, xs, sx, srkv, swkv, sconv, lw, alpha):
    nbp, t, d = xp.shape
    nbs = xs.shape[0]
    xp2 = xp.reshape(nbp * t, d)
    xs2 = xs.reshape(nbs, d)

    up_last, hgp, dlp, gp, rp, kp, vp, ap, bp, bonp = _inproj(xp2, None, None, t, lw)
    us, hgs, dls, gs, rs, ks, vs, as_, bs, bons = _inproj(xs2, sx, srkv, 1, lw)
    dr = dlp.shape[1]

    seq = lambda z: z.reshape(nbp, t, dr)
    yp_raw, s_end = _scan(seq(rp), seq(dlp), seq(kp), seq(vp), seq(ap), seq(bp))
    s_new, ys_raw = _step(swkv, rs, dls, ks, vs, as_, bs)
    yrp = _post(yp_raw.reshape(nbp * t, dr), bonp, gp, lw)
    yrs = _post(ys_raw, bons, gs, lw)

    ycp = _conv_seq(hgp, t, lw)
    ycs, nbuf = _conv_step(sconv, hgs, lw)

    n_prompt = nbp * t
    n_tok = n_prompt + nbs
    dst = _outproj(xp2, yrp, ycp, lw, alpha, n_tok, 0)
    h_all, logits = _outproj(xs2, yrs, ycs, lw, alpha, n_tok, n_prompt, dst=tuple(dst))

    ne = logits.shape[1]
    moe_tm, moe_tf = _moe_tiles(n_tok * TOP_K, ne, d, lw["w_down"].shape[1])
    gates, pos, sorted_tok, src_start, blk_e, q_eff, valid = _route(logits, moe_tm)
    x_sorted = _gather_rows(h_all, sorted_tok, src_start, q_eff, valid, moe_tm)
    y_rows = _moe_experts(x_sorted, blk_e, q_eff, valid, lw, moe_tm, MOE_SUB, moe_tf)
    yp, ys = _combine(h_all, gates, pos, y_rows, lw, alpha, n_prompt)
    yp = yp.reshape(nbp, t, d)
    ys = ys.reshape(nbs, 1, d)
    p_state = (xp[:, -1], up_last.reshape(nbp, -1, SUBLANES, 3 * dr)[:, -1, -1], _diag_blocks(s_end),
               hgp.reshape(nbp, t, -1)[:, t - (CONV_WIDTH - 1):])
    s_state = (xs2, us, s_new, nbuf)
    return yp, ys, p_state, s_state


def kernel(x_prompt, x_sample, state_shift_x, state_shift_rkv, state_wkv, state_conv, w_in, mu_x, mu_rkv, w0, w_A,
           w_B, a0, a_A, a_B, g_A, g_B, k_k, k_a, r_k, gn_g, gn_b, conv_w, conv_b, conv_ln_g, conv_ln_b, w_out,
           ln1_g, ln1_b, w_router, b_router, w_gu, b_gu, w_down, b_down, ln2_g, ln2_b):
    params = dict(w_in=w_in, mu_x=mu_x, mu_rkv=mu_rkv, w0=w0, w_A=w_A, w_B=w_B, a0=a0, a_A=a_A, a_B=a_B, g_A=g_A,
                  g_B=g_B, k_k=k_k, k_a=k_a, r_k=r_k, gn_g=gn_g, gn_b=gn_b, conv_w=conv_w, conv_b=conv_b,
                  conv_ln_g=conv_ln_g, conv_ln_b=conv_ln_b, w_out=w_out, ln1_g=ln1_g, ln1_b=ln1_b,
                  w_router=w_router, b_router=b_router, w_gu=w_gu, b_gu=b_gu, w_down=w_down, b_down=b_down,
                  ln2_g=ln2_g, ln2_b=ln2_b)
    depth = w_in.shape[0]
    assert x_sample.shape[1] == 1, "the sample group advances one token per step"
    alpha = (2.0 * depth) ** 0.25
    xp, xs = x_prompt, x_sample
    p_states, s_states = [], []
    for l in range(depth):
        lw = {name: p[l] for name, p in params.items()}
        for name in _VEC_PARAMS:
            lw[name] = lw[name].reshape(1, -1)
        for name in _BF16_PARAMS:
            lw[name] = lw[name].astype(BF16)
        xp, xs, p_st, s_st = _layer(xp, xs, state_shift_x[l], state_shift_rkv[l], state_wkv[l], state_conv[l],
                                    lw, alpha)
        p_states.append(p_st)
        s_states.append(s_st)
    stack = lambda states, i: jnp.stack([st[i] for st in states])
    return (xp, xs,
            stack(p_states, 0), stack(p_states, 1), stack(p_states, 2), stack(p_states, 3),
            stack(s_states, 0), stack(s_states, 1), stack(s_states, 2), stack(s_states, 3))
```

```python
import functools
import math

import jax
import jax.numpy as jnp
from jax import lax
from jax.experimental import pallas as pl
from jax.experimental.pallas import tpu as pltpu

F32 = jnp.float32
BF16 = jnp.bfloat16

RWKV_HEAD = 64
CONV_WIDTH = 31
TOP_K = 4
SWIGLU_LIMIT = 7.0
SWIGLU_ALPHA = 1.702
LN_EPS = 1e-5
GN_EPS = 64e-5

LANES = 128
SUBLANES = 8
MXU_DIM = 256
VMEM_BYTES = 64 * 1024 * 1024
VMEM_LIMIT_BYTES = 56 * 1024 * 1024
MOE_VMEM_LIMIT_BYTES = VMEM_BYTES - 3 * 1024 * 1024

SCAN_CHUNK = 64
HEADS_PER_GROUP = MXU_DIM // RWKV_HEAD
CONV_HALO = 32
CONV_ROW_CHUNK = 128
GATHER_GROUP = 32
COMBINE_LAG = 2
MOE_SUB = 256


def _row_tile(n, target):
    best = None
    for t in range(SUBLANES, min(n, target) + 1, SUBLANES):
        if n % t == 0:
            best = t
    assert best is not None, (n, target)
    return best


def _params(*sem, vmem_limit_bytes=VMEM_LIMIT_BYTES):
    return pltpu.CompilerParams(dimension_semantics=sem, vmem_limit_bytes=vmem_limit_bytes)


def _const_spec(shape):
    nd = len(shape)
    return pl.BlockSpec(shape, lambda *_: (0,) * nd, pipeline_mode=pl.Buffered(1))


def _dot(a, b):
    return jnp.dot(a, b, preferred_element_type=F32)


def _split2(x):
    hi = x.astype(BF16)
    lo = (x - hi.astype(F32)).astype(BF16)
    return hi, lo


def _layer_norm(z, g, b):
    mu = jnp.mean(z, axis=-1, keepdims=True)
    zc = z - mu
    var = jnp.mean(zc * zc, axis=-1, keepdims=True)
    return zc * lax.rsqrt(var + LN_EPS) * g + b


def _shift_rows(x, first_row):
    xs = pltpu.roll(x, 1, axis=0)
    row = lax.broadcasted_iota(jnp.int32, x.shape, 0)
    return jnp.where(row == 0, first_row, xs)


def _head_mask(n, dtype):
    r = lax.broadcasted_iota(jnp.int32, (n, n), 0) // RWKV_HEAD
    c = lax.broadcasted_iota(jnp.int32, (n, n), 1) // RWKV_HEAD
    return (r == c).astype(dtype)


def _head_sum(x, ones_bd):
    cols = []
    for c0 in range(0, x.shape[1], MXU_DIM):
        hi, lo = _split2(x[:, c0:c0 + MXU_DIM])
        cols.append(_dot(hi, ones_bd) + _dot(lo, ones_bd))
    return cols[0] if len(cols) == 1 else jnp.concatenate(cols, axis=1)


def _inproj_kernel(x_ref, xprev_ref, uprev_ref, w_in_ref, mu_ref, wA_ref, aA_ref, gA_ref, wB_ref, aB_ref, gB_ref,
                   w0_ref, a0_ref, murkv_ref, kk_ref, ka_ref, rk_ref,
                   ulast_ref, hglu_ref, dlog_ref, g_ref, r_ref, k_ref, v_ref, a_ref, b_ref, bonus_ref,
                   ucarry_scr, *, seq_tiles, halo):
    x = x_ref[...]
    tm = x.shape[0]
    first = (pl.program_id(0) % seq_tiles) == 0
    if halo:
        prev_row = jnp.where(first, 0.0, xprev_ref[SUBLANES - 1:SUBLANES, :])
        xprev = _shift_rows(x, prev_row)
    else:
        xprev = xprev_ref[...]
    xx = xprev - x
    xb = x.astype(BF16)
    d3 = murkv_ref.shape[1]
    dc = hglu_ref.shape[1]
    u = _dot(xb, w_in_ref[:, :d3])
    val = _dot(xb, w_in_ref[:, d3:d3 + dc])
    gate = _dot(xb, w_in_ref[:, d3 + dc:])
    hglu_ref[...] = val * jax.nn.sigmoid(gate)

    xw = (x + xx * mu_ref[0:1, :]).astype(BF16)
    tw = jnp.tanh(_dot(xw, wA_ref[...]))
    wlin = w0_ref[...] + _dot(tw.astype(BF16), wB_ref[...])
    wlog = -jax.nn.softplus(-wlin) - 0.5
    dlog_ref[...] = -jnp.exp(wlog)

    xa = (x + xx * mu_ref[1:2, :]).astype(BF16)
    ta = _dot(xa, aA_ref[...])
    asig = jax.nn.sigmoid(a0_ref[...] + _dot(ta.astype(BF16), aB_ref[...]))

    xg = (x + xx * mu_ref[2:3, :]).astype(BF16)
    tg = jax.nn.sigmoid(_dot(xg, gA_ref[...]))
    g_ref[...] = _dot(tg.astype(BF16), gB_ref[...])

    if halo:
        @pl.when(pl.program_id(0) == 0)
        def _():
            ucarry_scr[...] = jnp.zeros_like(ucarry_scr)

        uprev = _shift_rows(u, jnp.where(first, 0.0, ucarry_scr[0:1, :]))
        ucarry_scr[0:1, :] = u[tm - 1:tm, :]
        ulast_ref[...] = u[tm - SUBLANES:tm, :]
    else:
        uprev = uprev_ref[...]
        ulast_ref[...] = u
    rkv = u + (uprev - u) * murkv_ref[...]
    dr = r_ref.shape[1]
    r = rkv[:, :dr]
    k = rkv[:, dr:2 * dr]
    v = rkv[:, 2 * dr:]
    ones_bd = _head_mask(MXU_DIM, BF16)
    kk = k * kk_ref[...]
    kk = kk * lax.rsqrt(jnp.maximum(_head_sum(kk * kk, ones_bd), 1e-24))
    k = k * (1.0 + (asig - 1.0) * ka_ref[...])
    r_ref[...] = r
    k_ref[...] = k
    v_ref[...] = v
    a_ref[...] = -kk
    b_ref[...] = kk * asig
    bonus_ref[...] = _head_sum(r * k * rk_ref[...], ones_bd) * v


def _inproj(x, xprev, uprev, seq_len, lw):
    n, d = x.shape
    d_in = lw["w_in"].shape[1]
    dr = lw["w_B"].shape[1]
    d3 = 3 * dr
    dc = (d_in - d3) // 2
    halo = xprev is None
    tm = _row_tile(seq_len if halo else n, 256)
    seq_tiles = (seq_len // tm) if halo else 1
    row = lambda c: pl.BlockSpec((tm, c), lambda i: (i, 0))
    if halo:
        prev_specs = [pl.BlockSpec((SUBLANES, d), lambda i: (jnp.maximum(i * (tm // SUBLANES) - 1, 0), 0)),
                      pl.BlockSpec((SUBLANES, d3), lambda i: (0, 0))]
        prev_args = [x, jnp.zeros((SUBLANES, d3), F32)]
        ulast_spec = pl.BlockSpec((SUBLANES, d3), lambda i: (i, 0))
        ulast_shape = jax.ShapeDtypeStruct((n // tm * SUBLANES, d3), F32)
    else:
        prev_specs = [row(d), row(d3)]
        prev_args = [xprev, uprev]
        ulast_spec = row(d3)
        ulast_shape = jax.ShapeDtypeStruct((n, d3), F32)
    consts = [lw["w_in"], lw["mu_x"], lw["w_A"], lw["a_A"], lw["g_A"], lw["w_B"], lw["a_B"], lw["g_B"],
              lw["w0"], lw["a0"], lw["mu_rkv"], lw["k_k"], lw["k_a"], lw["r_k"]]
    widths = (dc, dr, dr) + (dr,) * 6
    return pl.pallas_call(
        functools.partial(_inproj_kernel, seq_tiles=seq_tiles, halo=halo),
        grid=(n // tm,),
        in_specs=[row(d)] + prev_specs + [_const_spec(c.shape) for c in consts],
        out_specs=[ulast_spec] + [row(c) for c in widths],
        out_shape=[ulast_shape] + [jax.ShapeDtypeStruct((n, c), F32) for c in widths],
        scratch_shapes=[pltpu.VMEM((SUBLANES, d3), F32)],
        compiler_params=_params("arbitrary"),
        name="inproj",
    )(x, *prev_args, *consts)


def _blockdiag(x, mask):
    reps = mask.shape[0] // x.shape[0]
    return jnp.concatenate([x] * reps, axis=0) * mask


def _dot_nt(a, b):
    return lax.dot_general(a, b, (((1,), (1,)), ((), ())), preferred_element_type=F32)


def _dot_tn(a, b):
    return lax.dot_general(a, b, (((0,), (0,)), ((), ())), preferred_element_type=F32)


def _scan_kernel(r_ref, dl_ref, k_ref, v_ref, a_ref, b_ref, y_ref, s_ref, h_scr):
    c = pl.program_id(0)
    nb = r_ref.shape[0]
    C = r_ref.shape[1]
    W = MXU_DIM
    ng = r_ref.shape[2] // W

    @pl.when(c == 0)
    def _():
        h_scr[...] = jnp.zeros_like(h_scr)

    ti = lax.broadcasted_iota(jnp.int32, (C, C), 0)
    tj = lax.broadcasted_iota(jnp.int32, (C, C), 1)
    tri = (ti >= tj).astype(BF16)
    t_row = lax.broadcasted_iota(jnp.int32, (C, W), 0)
    j_col = lax.broadcasted_iota(jnp.int32, (C, W), 1) % C
    strict = j_col < t_row
    incl = j_col <= t_row
    eye_cat = (j_col == t_row).astype(F32)
    rb = lax.broadcasted_iota(jnp.int32, (HEADS_PER_GROUP * C, W), 0) // C
    cb = lax.broadcasted_iota(jnp.int32, (HEADS_PER_GROUP * C, W), 1) // RWKV_HEAD
    bmask = (rb == cb).astype(BF16)
    hmask = _head_mask(W, F32)

    chains = [(bi, gi) for bi in range(nb) for gi in range(ng)]
    each = lambda f, *lists: [f(*xs) for xs in zip(*lists)]
    load = lambda ref: [ref[bi, :, gi * W:(gi + 1) * W] for bi, gi in chains]
    r, dl, k, v, a, b = (load(ref) for ref in (r_ref, dl_ref, k_ref, v_ref, a_ref, b_ref))
    sbd = [h_scr[bi, gi] for bi, gi in chains]
    bd = lambda x: _blockdiag(x, bmask)

    def cumsum(d):
        d_hi = d.astype(BF16)
        d_r1 = d - d_hi.astype(F32)
        d_mid = d_r1.astype(BF16)
        d_lo = (d_r1 - d_mid.astype(F32)).astype(BF16)
        return _dot(tri, d_hi) + (_dot(tri, d_mid) + _dot(tri, d_lo))

    cum = each(cumsum, dl)
    cum_last = each(lambda x: x[C - 1:C, :], cum)
    e_neg = each(lambda x: jnp.exp(-x), cum)
    e_end = each(lambda x, xl: jnp.exp(xl - x), cum, cum_last)
    at = each(lambda x, cu, d: (x * jnp.exp(cu - d)).astype(BF16), a, cum, dl)
    rt = each(lambda x, cu: (x * jnp.exp(cu)).astype(BF16), r, cum)
    bt = each(lambda x, e: (x * e).astype(BF16), b, e_neg)
    kt = each(lambda x, e: (x * e).astype(BF16), k, e_neg)
    bh = each(lambda x, e: (x * e).astype(BF16), b, e_end)
    kh = each(lambda x, e: (x * e).astype(BF16), k, e_end)
    vb = each(lambda x: x.astype(BF16), v)
    ar = each(lambda x, y: jnp.concatenate([x, y], axis=0), at, rt)

    p_b = each(lambda x, y: _dot_nt(x, bd(y)), ar, bt)
    p_k = each(lambda x, y: _dot_nt(x, bd(y)), ar, kt)
    arh = each(lambda x, s: _dot_nt(x, s.astype(BF16)), ar, sbd)
    p_ab = each(lambda p: jnp.where(strict, p[:C], 0.0), p_b)
    p_rb = each(lambda p: jnp.where(incl, p[C:], 0.0).astype(BF16), p_b)
    p_akrk = each(lambda p: jnp.concatenate([jnp.where(strict, p[:C], 0.0), jnp.where(incl, p[C:], 0.0)],
                                            axis=0).astype(BF16), p_k)

    pv = each(lambda p, x: _dot(p, bd(x)), p_akrk, vb)
    n_sq = int(math.log2(C))
    nn_b = each(lambda p: p.astype(BF16), p_ab)
    nn_b = each(lambda n: _dot(n, bd(n)).astype(BF16), nn_b)
    w = each(lambda x, y: (x[:C] + y[:C]).astype(BF16), arh, pv)
    tm = each(lambda p: eye_cat + p, p_ab)
    for i in range(1, n_sq):
        last = i == n_sq - 1
        lhs = each(lambda t, n: t.astype(BF16) if last else jnp.concatenate([t.astype(BF16), n], axis=0), tm, nn_b)
        prod = each(lambda l, n: _dot(l, bd(n)), lhs, nn_b)
        tm = each(lambda t, p: t + p[:C], tm, prod)
        if not last:
            nn_b = each(lambda p: p[C:].astype(BF16), prod)
    ub = each(lambda t, x: _dot(t.astype(BF16), bd(x)).astype(BF16), tm, w)

    yv = each(lambda x, y, p, u: x[C:] + y[C:] + _dot(p, bd(u)), arh, pv, p_rb, ub)
    upd = each(lambda u, x, y, z: _dot_tn(jnp.concatenate([u, x], axis=0), jnp.concatenate([y, z], axis=0)),
               ub, vb, bh, kh)
    for (bi, gi), y, s, xl, up in zip(chains, yv, sbd, cum_last, upd):
        y_ref[bi, :, gi * W:(gi + 1) * W] = y
        h_scr[bi, gi] = (s * jnp.exp(xl) + up) * hmask

    @pl.when(c == pl.num_programs(0) - 1)
    def _():
        s_ref[...] = h_scr[...]


def _scan(r, dl, k, v, a, b):
    nb, t, dr = r.shape
    ng = dr // MXU_DIM
    C = SCAN_CHUNK
    spec = pl.BlockSpec((nb, C, dr), lambda c: (0, c, 0))
    return pl.pallas_call(
        _scan_kernel,
        grid=(t // C,),
        in_specs=[spec] * 6,
        out_specs=[spec, pl.BlockSpec((nb, ng, MXU_DIM, MXU_DIM), lambda c: (0, 0, 0, 0))],
        out_shape=[jax.ShapeDtypeStruct((nb, t, dr), F32),
                   jax.ShapeDtypeStruct((nb, ng, MXU_DIM, MXU_DIM), F32)],
        scratch_shapes=[pltpu.VMEM((nb, ng, MXU_DIM, MXU_DIM), F32)],
        compiler_params=_params("arbitrary"),
        name="wkv_scan",
    )(r, dl, k, v, a, b)


def _step_kernel(s_ref, r_ref, dl_ref, k_ref, v_ref, a_ref, b_ref, sout_ref, y_ref):
    S = s_ref[0]
    per_key = lambda ref: ref[...][None, :, :]
    sa = jnp.sum(S * per_key(a_ref), axis=1, keepdims=True)
    v = v_ref[...][:, None, :]
    s_new = S * jnp.exp(per_key(dl_ref)) + sa * per_key(b_ref) + v * per_key(k_ref)
    sout_ref[0] = s_new
    y_ref[...] = jnp.sum(s_new * per_key(r_ref), axis=1)


def _step(S0, r, dl, k, v, a, b):
    nb, nh, n, _ = S0.shape
    st = jnp.transpose(S0, (1, 2, 3, 0))
    sspec = pl.BlockSpec((1, n, n, nb), lambda h: (h, 0, 0, 0))
    vspec = pl.BlockSpec((n, nb), lambda h: (h, 0))
    s_new, y = pl.pallas_call(
        _step_kernel,
        grid=(nh,),
        in_specs=[sspec] + [vspec] * 6,
        out_specs=[sspec, vspec],
        out_shape=[jax.ShapeDtypeStruct(st.shape, F32), jax.ShapeDtypeStruct((nh * n, nb), F32)],
        compiler_params=_params("arbitrary"),
        name="wkv_step",
    )(st, *(x.T for x in (r, dl, k, v, a, b)))
    return jnp.transpose(s_new, (3, 0, 1, 2)), y.T


def _post_kernel(y_ref, bonus_ref, g_ref, gng_ref, gnb_ref, o_ref):
    y = y_ref[...]
    ones_bd = _head_mask(MXU_DIM, BF16)
    inv_n = 1.0 / RWKV_HEAD
    mu = _head_sum(y, ones_bd) * inv_n
    yc = y - mu
    var = _head_sum(yc * yc, ones_bd) * inv_n
    yn = yc * lax.rsqrt(var + GN_EPS) * gng_ref[...] + gnb_ref[...]
    o_ref[...] = (yn + bonus_ref[...]) * g_ref[...]


def _post(y, bonus, g, lw):
    n, dr = y.shape
    tm = _row_tile(n, 512)
    row = pl.BlockSpec((tm, dr), lambda i: (i, 0))
    return pl.pallas_call(
        _post_kernel,
        grid=(n // tm,),
        in_specs=[row, row, row, _const_spec(lw["gn_g"].shape), _const_spec(lw["gn_b"].shape)],
        out_specs=row,
        out_shape=jax.ShapeDtypeStruct((n, dr), F32),
        compiler_params=_params("arbitrary"),
        name="wkv_post",
    )(y, bonus, g, lw["gn_g"], lw["gn_b"])


def _conv_seq_kernel(h_ref, halo_ref, w_ref, b_ref, lng_ref, lnb_ref, o_ref, hp_scr, *, seq_tiles):
    tt = h_ref.shape[0]
    first = (pl.program_id(0) % seq_tiles) == 0
    hp_scr[0:CONV_HALO, :] = jnp.where(first, 0.0, halo_ref[...])
    hp_scr[CONV_HALO:, :] = h_ref[...]
    off = CONV_HALO - (CONV_WIDTH - 1)
    rc = CONV_ROW_CHUNK if tt % CONV_ROW_CHUNK == 0 else tt
    for c0 in range(0, o_ref.shape[1], LANES):
        lanes = slice(c0, c0 + LANES)
        for t0 in range(0, tt, rc):
            acc = jnp.broadcast_to(b_ref[:, lanes], (rc, LANES))
            for p in range(SUBLANES):
                rows = rc if p == 0 else rc + SUBLANES
                part = None
                for m in range((CONV_HALO + SUBLANES) // SUBLANES):
                    j = SUBLANES * m + p - off
                    if 0 <= j < CONV_WIDTH:
                        term = hp_scr[t0 + SUBLANES * m:t0 + SUBLANES * m + rows, lanes] * w_ref[j:j + 1, lanes]
                        part = term if part is None else part + term
                acc = acc + part[p:p + rc]
            o_ref[t0:t0 + rc, lanes] = acc
    z = _layer_norm(o_ref[...], lng_ref[...], lnb_ref[...])
    o_ref[...] = z * jax.nn.sigmoid(z)


def _conv_seq(h, seq_len, lw):
    n, dc = h.shape
    tt = _row_tile(seq_len, 256)
    assert tt % CONV_HALO == 0
    seq_tiles = seq_len // tt
    row = pl.BlockSpec((tt, dc), lambda i: (i, 0))
    halo = pl.BlockSpec((CONV_HALO, dc), lambda i: (jnp.maximum(i * (tt // CONV_HALO) - 1, 0), 0))
    consts = [lw["conv_w"], lw["conv_b"], lw["conv_ln_g"], lw["conv_ln_b"]]
    return pl.pallas_call(
        functools.partial(_conv_seq_kernel, seq_tiles=seq_tiles),
        grid=(n // tt,),
        in_specs=[row, halo] + [_const_spec(c.shape) for c in consts],
        out_specs=row,
        out_shape=jax.ShapeDtypeStruct((n, dc), F32),
        scratch_shapes=[pltpu.VMEM((CONV_HALO + tt, dc), F32)],
        compiler_params=_params("arbitrary"),
        name="conv_seq",
    )(h, h, *consts)


def _conv_step_kernel(buf_ref, h_ref, w_ref, b_ref, lng_ref, lnb_ref, o_ref, nbuf_ref):
    h = h_ref[...]
    acc = b_ref[...] + h * w_ref[CONV_WIDTH - 1:CONV_WIDTH, :]
    for j in range(CONV_WIDTH - 1):
        tap = buf_ref[j]
        acc = acc + tap * w_ref[j:j + 1, :]
        if j > 0:
            nbuf_ref[j - 1] = tap
    nbuf_ref[CONV_WIDTH - 2] = h
    z = _layer_norm(acc, lng_ref[...], lnb_ref[...])
    o_ref[...] = z * jax.nn.sigmoid(z)


def _conv_step(buf, h, lw):
    nb, wm1, dc = buf.shape
    bb = _row_tile(nb, 32)
    bspec = pl.BlockSpec((wm1, bb, dc), lambda i: (0, i, 0))
    row = pl.BlockSpec((bb, dc), lambda i: (i, 0))
    consts = [lw["conv_w"], lw["conv_b"], lw["conv_ln_g"], lw["conv_ln_b"]]
    out, nbuf = pl.pallas_call(
        _conv_step_kernel,
        grid=(nb // bb,),
        in_specs=[bspec, row] + [_const_spec(c.shape) for c in consts],
        out_specs=[row, bspec],
        out_shape=[jax.ShapeDtypeStruct((nb, dc), F32), jax.ShapeDtypeStruct((wm1, nb, dc), F32)],
        compiler_params=_params("arbitrary"),
        name="conv_step",
    )(jnp.transpose(buf, (1, 0, 2)), h, *consts)
    return out, jnp.transpose(nbuf, (1, 0, 2))


def _outproj_kernel(*refs, alpha, n_dst):
    x_ref, yr_ref, yc_ref, wo_ref, g_ref, b_ref, wr_ref, br_ref, h_ref, logit_ref = refs[n_dst:]
    ymix = jnp.concatenate([yr_ref[...].astype(BF16), yc_ref[...].astype(BF16)], axis=1)
    h = _layer_norm(alpha * x_ref[...] + _dot(ymix, wo_ref[...]), g_ref[...], b_ref[...])
    h_ref[...] = h
    logit_ref[...] = _dot(h.astype(BF16), wr_ref[...]) + br_ref[...]


def _outproj(x, y_rwkv, y_conv, lw, alpha, n_total, row0, dst=()):
    n, d = x.shape
    dr = y_rwkv.shape[1]
    dc = y_conv.shape[1]
    ne = lw["w_router"].shape[1]
    tm = _row_tile(math.gcd(n, row0) if row0 else n, 256)
    blk0 = row0 // tm
    row = lambda c: pl.BlockSpec((tm, c), lambda i: (i, 0))
    out_row = lambda c: pl.BlockSpec((tm, c), lambda i: (blk0 + i, 0))
    consts = [lw["w_out"], lw["ln1_g"], lw["ln1_b"], lw["w_router"], lw["b_router"]]
    return pl.pallas_call(
        functools.partial(_outproj_kernel, alpha=alpha, n_dst=len(dst)),
        grid=(n // tm,),
        in_specs=[pl.BlockSpec(memory_space=pl.ANY)] * len(dst) + [row(d), row(dr), row(dc)]
        + [_const_spec(c.shape) for c in consts],
        out_specs=[out_row(d), out_row(ne)],
        out_shape=[jax.ShapeDtypeStruct((n_total, d), F32), jax.ShapeDtypeStruct((n_total, ne), F32)],
        input_output_aliases={i: i for i in range(len(dst))},
        compiler_params=_params("arbitrary"),
        name="outproj",
    )(*dst, x, y_rwkv, y_conv, *consts)


def _route(logits, moe_tm):
    n_tok, ne = logits.shape
    top_val, top_idx = lax.top_k(logits, TOP_K)
    gates = jax.nn.softmax(top_val, axis=-1)
    n_assign = n_tok * TOP_K
    e_flat = top_idx.reshape(n_assign).astype(jnp.int32)
    onehot = (e_flat[:, None] == jnp.arange(ne, dtype=jnp.int32)[None, :]).astype(jnp.int32)
    counts = onehot.sum(0)
    rank = jnp.take_along_axis(jnp.cumsum(onehot, axis=0), e_flat[:, None], axis=1)[:, 0] - 1
    nblk_e = (counts + moe_tm - 1) // moe_tm
    blk_end = jnp.cumsum(nblk_e)
    blk_start = blk_end - nblk_e
    n_active = blk_end[-1]
    nb_max = n_assign // moe_tm + ne
    pos = blk_start[e_flat] * moe_tm + rank
    sorted_tok = (jnp.argsort(e_flat, stable=True) // TOP_K).astype(jnp.int32)
    sorted_tok = jnp.concatenate([sorted_tok, jnp.zeros((GATHER_GROUP,), jnp.int32)])
    cstart = jnp.cumsum(counts) - counts
    q = jnp.arange(nb_max, dtype=jnp.int32)
    q_eff = jnp.minimum(q, n_active - 1)
    blk_e = jnp.minimum(jnp.searchsorted(blk_end, q_eff, side="right"), ne - 1).astype(jnp.int32)
    row_in_e = (q_eff - blk_start[blk_e]) * moe_tm
    valid = jnp.clip(counts[blk_e] - row_in_e, 0, moe_tm)
    valid = jnp.where(q < n_active, valid, 0).astype(jnp.int32)
    src_start = (cstart[blk_e] + row_in_e).astype(jnp.int32)
    return gates, pos.astype(jnp.int32), sorted_tok, src_start, blk_e, q_eff.astype(jnp.int32), valid


def _gather_kernel(qeff_ref, valid_ref, src_ref, tok_ref, h_hbm, o_ref, buf, sem):
    s = pl.program_id(0)
    nb = pl.num_programs(0) - 1
    per = GATHER_GROUP // SUBLANES
    groups = lambda nvalid: (nvalid + GATHER_GROUP - 1) // GATHER_GROUP

    @pl.when(s < nb)
    def _():
        slot = s % 2
        base = src_ref[s]

        def issue(g, c):
            for i in range(GATHER_GROUP):
                tok = tok_ref[base + g * GATHER_GROUP + i]
                pltpu.make_async_copy(h_hbm.at[pl.ds(tok, 1)],
                                      buf.at[slot, g * per + i // SUBLANES, pl.ds(i % SUBLANES, 1)],
                                      sem.at[slot]).start()
            return c

        lax.fori_loop(0, groups(valid_ref[s]), issue, 0)

    @pl.when(s > 0)
    def _():
        slot = (s - 1) % 2
        nvalid = valid_ref[s - 1]

        @pl.when(nvalid > 0)
        def _():
            def wait(g, c):
                pltpu.make_async_copy(h_hbm.at[pl.ds(0, SUBLANES)], buf.at[slot, g], sem.at[slot]).wait()
                return c

            lax.fori_loop(0, groups(nvalid) * per, wait, 0)
            rows = lax.broadcasted_iota(jnp.int32, o_ref.shape, 0)
            o_ref[...] = jnp.where(rows < nvalid, buf[slot].reshape(o_ref.shape), 0.0).astype(o_ref.dtype)


def _gather_rows(h, sorted_tok, src_start, q_eff, valid, moe_tm):
    n_tok, d = h.shape
    nb_max = q_eff.shape[0]
    grid_spec = pltpu.PrefetchScalarGridSpec(
        num_scalar_prefetch=4,
        grid=(nb_max + 1,),
        in_specs=[pl.BlockSpec(memory_space=pl.ANY)],
        out_specs=pl.BlockSpec((moe_tm, d), lambda s, qe, va, sr, tk: (qe[jnp.maximum(s - 1, 0)], 0)),
        scratch_shapes=[pltpu.VMEM((2, moe_tm // SUBLANES, SUBLANES, d), F32), pltpu.SemaphoreType.DMA((2,))],
    )
    return pl.pallas_call(
        _gather_kernel,
        grid_spec=grid_spec,
        out_shape=jax.ShapeDtypeStruct((nb_max * moe_tm, d), BF16),
        compiler_params=_params("arbitrary"),
        name="moe_gather",
    )(q_eff, valid, src_start, sorted_tok, h)


def _moe_kernel(be_ref, qeff_ref, valid_ref, x_ref, wg_ref, wl_ref, bg_ref, bl_ref, wd_ref, bd_ref, o_ref,
                wg_s, wl_s, wd_s, *, sub):
    q = pl.program_id(0)
    j = pl.program_id(1)
    nvalid = valid_ref[q]
    unit = sub // 2
    n_units = (nvalid + unit - 1) // unit
    n_all = o_ref.shape[0] // unit

    def fill(s, value):
        r0 = pl.multiple_of(s * unit, unit)
        o_ref[pl.ds(r0, unit), :] = jnp.broadcast_to(value, (unit, o_ref.shape[1]))

    def rows_block(r0, rows):
        x = x_ref[pl.ds(r0, rows), :]
        g = jnp.minimum(_dot(x, wg_s[...]) + bg_ref[0], SWIGLU_LIMIT)
        l = jnp.clip(_dot(x, wl_s[...]) + bl_ref[0], -SWIGLU_LIMIT, SWIGLU_LIMIT)
        act = g * jax.nn.sigmoid(SWIGLU_ALPHA * g) * (l + 1.0)
        o_ref[pl.ds(r0, rows), :] += _dot(act.astype(BF16), wd_s[...])

    @pl.when(nvalid > 0)
    def _():
        @pl.when(j == 0)
        def _():
            lax.fori_loop(0, n_units, lambda s, c: (fill(s, bd_ref[0]), c)[1], 0)
            lax.fori_loop(n_units, n_all, lambda s, c: (fill(s, jnp.zeros((1, 1), F32)), c)[1], 0)

        wg_s[...] = wg_ref[0].astype(BF16)
        wl_s[...] = wl_ref[0].astype(BF16)
        wd_s[...] = wd_ref[0].astype(BF16)
        rows_block(0, unit)
        rest = n_units - 1
        n_pairs = rest // 4
        tail = rest - n_pairs * 4

        def pair(i, c):
            r0 = pl.multiple_of(unit + i * 2 * sub, unit)
            rows_block(r0, sub)
            rows_block(r0 + sub, sub)
            return c

        lax.fori_loop(0, n_pairs, pair, 0)
        t0 = pl.multiple_of(unit + n_pairs * 2 * sub, unit)

        @pl.when(tail >= 2)
        def _():
            rows_block(t0, sub)

        @pl.when(tail % 2 == 1)
        def _():
            rows_block(pl.multiple_of(t0 + (tail // 2) * sub, unit), unit)


def _moe_experts(x_sorted, blk_e, q_eff, valid, lw, moe_tm, sub, tf):
    n_rows, d = x_sorted.shape
    ne, _, f2 = lw["w_gu"].shape
    f = f2 // 2
    nf = f // tf
    nb_max = q_eff.shape[0]
    b_gu = lw["b_gu"].reshape(ne, 1, f2)
    b_down = lw["b_down"].reshape(ne, 1, d)

    def jf(q, j, va):
        return jnp.where(va[q] > 0, j, nf - 1)

    grid_spec = pltpu.PrefetchScalarGridSpec(
        num_scalar_prefetch=3,
        grid=(nb_max, nf),
        in_specs=[
            pl.BlockSpec((moe_tm, d), lambda q, j, be, qe, va: (qe[q], 0)),
            pl.BlockSpec((1, d, tf), lambda q, j, be, qe, va: (be[q], 0, jf(q, j, va))),
            pl.BlockSpec((1, d, tf), lambda q, j, be, qe, va: (be[q], 0, nf + jf(q, j, va))),
            pl.BlockSpec((1, 1, tf), lambda q, j, be, qe, va: (be[q], 0, jf(q, j, va))),
            pl.BlockSpec((1, 1, tf), lambda q, j, be, qe, va: (be[q], 0, nf + jf(q, j, va))),
            pl.BlockSpec((1, tf, d), lambda q, j, be, qe, va: (be[q], jf(q, j, va), 0)),
            pl.BlockSpec((1, 1, d), lambda q, j, be, qe, va: (be[q], 0, 0)),
        ],
        out_specs=pl.BlockSpec((moe_tm, d), lambda q, j, be, qe, va: (qe[q], 0)),
        scratch_shapes=[pltpu.VMEM((d, tf), BF16), pltpu.VMEM((d, tf), BF16), pltpu.VMEM((tf, d), BF16)],
    )
    return pl.pallas_call(
        functools.partial(_moe_kernel, sub=sub),
        grid_spec=grid_spec,
        out_shape=jax.ShapeDtypeStruct((n_rows, d), F32),
        compiler_params=_params("arbitrary", "arbitrary", vmem_limit_bytes=MOE_VMEM_LIMIT_BYTES),
        name="moe_experts",
    )(blk_e, q_eff, valid, x_sorted, lw["w_gu"], lw["w_gu"], b_gu, b_gu, lw["w_down"], b_down)


def _combine_kernel(pos_ref, h_ref, gates_ref, yrows_hbm, g_ref, b_ref, op_ref, os_ref, *scratch,
                    alpha, n_first, n_tiles):
    i = pl.program_id(0)
    tm = h_ref.shape[0]
    n_slots = COMBINE_LAG + 1
    slots = tuple(zip(scratch[:n_slots], scratch[n_slots:]))

    def fetch(tile, buf, sem):
        for r in range(tm):
            for k in range(TOP_K):
                p = pos_ref[(tile * tm + r) * TOP_K + k]
                pltpu.make_async_copy(yrows_hbm.at[pl.ds(p, 1)],
                                      buf.at[k, r // SUBLANES, pl.ds(r % SUBLANES, 1)], sem).start()

    def wait(buf, sem):
        for k in range(TOP_K):
            pltpu.make_async_copy(buf.at[k], buf.at[k], sem).wait()

    def finish(tile, buf):
        gates = gates_ref[...]
        slot = lambda k: buf[k].reshape(tm, buf.shape[-1])
        ffn = slot(0) * gates[:, 0:1]
        for k in range(1, TOP_K):
            ffn = ffn + slot(k) * gates[:, k:k + 1]
        y = _layer_norm(alpha * h_ref[...] + ffn, g_ref[...], b_ref[...])

        @pl.when(tile < n_first)
        def _():
            op_ref[...] = y

        @pl.when(tile >= n_first)
        def _():
            os_ref[...] = y

    for step in range(min(COMBINE_LAG, n_tiles)):
        @pl.when(i == step)
        def _():
            fetch(step, *slots[step % n_slots])

    for phase in range(n_slots):
        @pl.when((i >= COMBINE_LAG) & (i < n_tiles) & (i % n_slots == phase))
        def _():
            done = slots[(phase - COMBINE_LAG) % n_slots]
            wait(*done)
            fetch(i, *slots[phase])
            finish(i - COMBINE_LAG, done[0])

    for step in range(max(n_tiles, COMBINE_LAG), n_tiles + COMBINE_LAG):
        @pl.when(i == step)
        def _():
            done = slots[(step - COMBINE_LAG) % n_slots]
            wait(*done)
            finish(step - COMBINE_LAG, done[0])


def _combine(h, gates, pos, y_rows, lw, alpha, n_prompt):
    n_tok, d = h.shape
    tm = _row_tile(math.gcd(n_prompt, n_tok - n_prompt), 128)
    n_first = n_prompt // tm
    n_tiles = n_tok // tm
    done = lambda i: jnp.maximum(i - COMBINE_LAG, 0)
    n_slots = COMBINE_LAG + 1
    buf = pltpu.VMEM((TOP_K, tm // SUBLANES, SUBLANES, d), F32)
    grid_spec = pltpu.PrefetchScalarGridSpec(
        num_scalar_prefetch=1,
        grid=(n_tiles + COMBINE_LAG,),
        in_specs=[
            pl.BlockSpec((tm, d), lambda i, p: (done(i), 0)),
            pl.BlockSpec((tm, TOP_K), lambda i, p: (done(i), 0)),
            pl.BlockSpec(memory_space=pl.ANY),
            pl.BlockSpec((1, d), lambda i, p: (0, 0)),
            pl.BlockSpec((1, d), lambda i, p: (0, 0)),
        ],
        out_specs=[pl.BlockSpec((tm, d), lambda i, p: (jnp.minimum(done(i), n_first - 1), 0)),
                   pl.BlockSpec((tm, d), lambda i, p: (jnp.maximum(done(i) - n_first, 0), 0))],
        scratch_shapes=[buf] * n_slots + [pltpu.SemaphoreType.DMA(())] * n_slots,
    )
    return pl.pallas_call(
        functools.partial(_combine_kernel, alpha=alpha, n_first=n_first, n_tiles=n_tiles),
        grid_spec=grid_spec,
        out_shape=[jax.ShapeDtypeStruct((n_prompt, d), F32), jax.ShapeDtypeStruct((n_tok - n_prompt, d), F32)],
        compiler_params=_params("arbitrary"),
        name="moe_combine",
    )(pos, h, gates, y_rows, lw["ln2_g"], lw["ln2_b"])


def _moe_tiles(n_assign, ne, d, f):
    unit = MOE_SUB // 2
    mean = -(-n_assign // ne)
    tm = max(MOE_SUB, -(-(mean * 11 // 10) // unit) * unit)
    for tf in (512, 256):
        tf = min(tf, f)
        blocks = 2 * tm * d * 2 + 2 * tm * d * 4
        weights = 3 * d * tf * (2 * 4 + 2)
        temps = MOE_SUB * 2 * tf * 4
        if blocks + weights + temps <= MOE_VMEM_LIMIT_BYTES:
            return tm, tf
    return min(tm, 4 * MOE_SUB), min(256, f)


_VEC_PARAMS = ("mu_rkv", "w0", "a0", "k_k", "k_a", "r_k", "gn_g", "gn_b", "conv_b", "conv_ln_g", "conv_ln_b",
               "ln1_g", "ln1_b", "b_router", "ln2_g", "ln2_b")
_BF16_PARAMS = ("w_in", "w_A", "w_B", "a_A", "a_B", "g_A", "g_B", "w_out", "w_router")


def _diag_blocks(s):
    nb, ng = s.shape[:2]
    s6 = s.reshape(nb, ng, HEADS_PER_GROUP, RWKV_HEAD, HEADS_PER_GROUP, RWKV_HEAD)
    d = jnp.stack([s6[:, :, h, :, h, :] for h in range(HEADS_PER_GROUP)], axis=2)
    return d.reshape(nb, ng * HEADS_PER_GROUP, RWKV_HEAD, RWKV_HEAD)


def _layer(xp, xs, sx, srkv, swkv, sconv, lw, alpha):
    nbp, t, d = xp.shape
    nbs = xs.shape[0]
    xp2 = xp.reshape(nbp * t, d)
    xs2 = xs.reshape(nbs, d)

    up_last, hgp, dlp, gp, rp, kp, vp, ap, bp, bonp = _inproj(xp2, None, None, t, lw)
    us, hgs, dls, gs, rs, ks, vs, as_, bs, bons = _inproj(xs2, sx, srkv, 1, lw)
    dr = dlp.shape[1]

    seq = lambda z: z.reshape(nbp, t, dr)
    yp_raw, s_end = _scan(seq(rp), seq(dlp), seq(kp), seq(vp), seq(ap), seq(bp))
    s_new, ys_raw = _step(swkv, rs, dls, ks, vs, as_, bs)
    yrp = _post(yp_raw.reshape(nbp * t, dr), bonp, gp, lw)
    yrs = _post(ys_raw, bons, gs, lw)

    ycp = _conv_seq(hgp, t, lw)
    ycs, nbuf = _conv_step(sconv, hgs, lw)

    n_prompt = nbp * t
    n_tok = n_prompt + nbs
    dst = _outproj(xp2, yrp, ycp, lw, alpha, n_tok, 0)
    h_all, logits = _outproj(xs2, yrs, ycs, lw, alpha, n_tok, n_prompt, dst=tuple(dst))

    ne = logits.shape[1]
    moe_tm, moe_tf = _moe_tiles(n_tok * TOP_K, ne, d, lw["w_down"].shape[1])
    gates, pos, sorted_tok, src_start, blk_e, q_eff, valid = _route(logits, moe_tm)
    x_sorted = _gather_rows(h_all, sorted_tok, src_start, q_eff, valid, moe_tm)
    y_rows = _moe_experts(x_sorted, blk_e, q_eff, valid, lw, moe_tm, MOE_SUB, moe_tf)
    yp, ys = _combine(h_all, gates, pos, y_rows, lw, alpha, n_prompt)
    yp = yp.reshape(nbp, t, d)
    ys = ys.reshape(nbs, 1, d)
    p_state = (xp[:, -1], up_last.reshape(nbp, -1, SUBLANES, 3 * dr)[:, -1, -1], _diag_blocks(s_end),
               hgp.reshape(nbp, t, -1)[:, t - (CONV_WIDTH - 1):])
    s_state = (xs2, us, s_new, nbuf)
    return yp, ys, p_state, s_state


def kernel(x_prompt, x_sample, state_shift_x, state_shift_rkv, state_wkv, state_conv, w_in, mu_x, mu_rkv, w0, w_A,
           w_B, a0, a_A, a_B, g_A, g_B, k_k, k_a, r_k, gn_g, gn_b, conv_w, conv_b, conv_ln_g, conv_ln_b, w_out,
           ln1_g, ln1_b, w_router, b_router, w_gu, b_gu, w_down, b_down, ln2_g, ln2_b):
    params = dict(w_in=w_in, mu_x=mu_x, mu_rkv=mu_rkv, w0=w0, w_A=w_A, w_B=w_B, a0=a0, a_A=a_A, a_B=a_B, g_A=g_A,
                  g_B=g_B, k_k=k_k, k_a=k_a, r_k=r_k, gn_g=gn_g, gn_b=gn_b, conv_w=conv_w, conv_b=conv_b,
                  conv_ln_g=conv_ln_g, conv_ln_b=conv_ln_b, w_out=w_out, ln1_g=ln1_g, ln1_b=ln1_b,
                  w_router=w_router, b_router=b_router, w_gu=w_gu, b_gu=b_gu, w_down=w_down, b_down=b_down,
                  ln2_g=ln2_g, ln2_b=ln2_b)
    depth = w_in.shape[0]
    assert x_sample.shape[1] == 1, "the sample group advances one token per step"
    alpha = (2.0 * depth) ** 0.25
    xp, xs = x_prompt, x_sample
    p_states, s_states = [], []
    for l in range(depth):
        lw = {name: p[l] for name, p in params.items()}
        for name in _VEC_PARAMS:
            lw[name] = lw[name].reshape(1, -1)
        for name in _BF16_PARAMS:
            lw[name] = lw[name].astype(BF16)
        xp, xs, p_st, s_st = _layer(xp, xs, state_shift_x[l], state_shift_rkv[l], state_wkv[l], state_conv[l],
                                    lw, alpha)
        p_states.append(p_st)
        s_states.append(s_st)
    stack = lambda states, i: jnp.stack([st[i] for st in states])
    return (xp, xs,
            stack(p_states, 0), stack(p_states, 1), stack(p_states, 2), stack(p_states, 3),
            stack(s_states, 0), stack(s_states, 1), stack(s_states, 2), stack(s_states, 3))
```

```python
import functools
import math

import jax
import jax.numpy as jnp
from jax import lax
from jax.experimental import pallas as pl
from jax.experimental.pallas import tpu as pltpu

F32 = jnp.float32
BF16 = jnp.bfloat16

RWKV_HEAD = 64
CONV_WIDTH = 31
TOP_K = 4
SWIGLU_LIMIT = 7.0
SWIGLU_ALPHA = 1.702
LN_EPS = 1e-5
GN_EPS = 64e-5

LANES = 128
SUBLANES = 8
MXU_DIM = 256
VMEM_BYTES = 64 * 1024 * 1024
VMEM_LIMIT_BYTES = 56 * 1024 * 1024
MOE_VMEM_LIMIT_BYTES = VMEM_BYTES - 3 * 1024 * 1024

SCAN_CHUNK = 64
HEADS_PER_GROUP = MXU_DIM // RWKV_HEAD
CONV_HALO = 32
CONV_ROW_CHUNK = 128
GATHER_GROUP = 32
COMBINE_LAG = 2
MOE_SUB = 256


def _row_tile(n, target):
    best = None
    for t in range(SUBLANES, min(n, target) + 1, SUBLANES):
        if n % t == 0:
            best = t
    assert best is not None, (n, target)
    return best


def _params(*sem, vmem_limit_bytes=VMEM_LIMIT_BYTES):
    return pltpu.CompilerParams(dimension_semantics=sem, vmem_limit_bytes=vmem_limit_bytes)


def _const_spec(shape):
    nd = len(shape)
    return pl.BlockSpec(shape, lambda *_: (0,) * nd, pipeline_mode=pl.Buffered(1))


def _dot(a, b):
    return jnp.dot(a, b, preferred_element_type=F32)


def _split2(x):
    hi = x.astype(BF16)
    lo = (x - hi.astype(F32)).astype(BF16)
    return hi, lo


def _layer_norm(z, g, b):
    mu = jnp.mean(z, axis=-1, keepdims=True)
    zc = z - mu
    var = jnp.mean(zc * zc, axis=-1, keepdims=True)
    return zc * lax.rsqrt(var + LN_EPS) * g + b


def _shift_rows(x, first_row):
    xs = pltpu.roll(x, 1, axis=0)
    row = lax.broadcasted_iota(jnp.int32, x.shape, 0)
    return jnp.where(row == 0, first_row, xs)


def _head_mask(n, dtype):
    r = lax.broadcasted_iota(jnp.int32, (n, n), 0) // RWKV_HEAD
    c = lax.broadcasted_iota(jnp.int32, (n, n), 1) // RWKV_HEAD
    return (r == c).astype(dtype)


def _head_sum(x, ones_bd):
    cols = []
    for c0 in range(0, x.shape[1], MXU_DIM):
        hi, lo = _split2(x[:, c0:c0 + MXU_DIM])
        cols.append(_dot(hi, ones_bd) + _dot(lo, ones_bd))
    return cols[0] if len(cols) == 1 else jnp.concatenate(cols, axis=1)


def _inproj_kernel(x_ref, xprev_ref, uprev_ref, w_in_ref, mu_ref, wA_ref, aA_ref, gA_ref, wB_ref, aB_ref, gB_ref,
                   w0_ref, a0_ref, murkv_ref, kk_ref, ka_ref, rk_ref,
                   ulast_ref, hglu_ref, dlog_ref, g_ref, r_ref, k_ref, v_ref, a_ref, b_ref, bonus_ref,
                   ucarry_scr, *, seq_tiles, halo):
    x = x_ref[...]
    tm = x.shape[0]
    first = (pl.program_id(0) % seq_tiles) == 0
    if halo:
        prev_row = jnp.where(first, 0.0, xprev_ref[SUBLANES - 1:SUBLANES, :])
        xprev = _shift_rows(x, prev_row)
    else:
        xprev = xprev_ref[...]
    xx = xprev - x
    xb = x.astype(BF16)
    d3 = murkv_ref.shape[1]
    dc = hglu_ref.shape[1]
    u = _dot(xb, w_in_ref[:, :d3])
    val = _dot(xb, w_in_ref[:, d3:d3 + dc])
    gate = _dot(xb, w_in_ref[:, d3 + dc:])
    hglu_ref[...] = val * jax.nn.sigmoid(gate)

    xw = (x + xx * mu_ref[0:1, :]).astype(BF16)
    tw = jnp.tanh(_dot(xw, wA_ref[...]))
    wlin = w0_ref[...] + _dot(tw.astype(BF16), wB_ref[...])
    wlog = -jax.nn.softplus(-wlin) - 0.5
    dlog_ref[...] = -jnp.exp(wlog)

    xa = (x + xx * mu_ref[1:2, :]).astype(BF16)
    ta = _dot(xa, aA_ref[...])
    asig = jax.nn.sigmoid(a0_ref[...] + _dot(ta.astype(BF16), aB_ref[...]))

    xg = (x + xx * mu_ref[2:3, :]).astype(BF16)
    tg = jax.nn.sigmoid(_dot(xg, gA_ref[...]))
    g_ref[...] = _dot(tg.astype(BF16), gB_ref[...])

    if halo:
        @pl.when(pl.program_id(0) == 0)
        def _():
            ucarry_scr[...] = jnp.zeros_like(ucarry_scr)

        uprev = _shift_rows(u, jnp.where(first, 0.0, ucarry_scr[0:1, :]))
        ucarry_scr[0:1, :] = u[tm - 1:tm, :]
        ulast_ref[...] = u[tm - SUBLANES:tm, :]
    else:
        uprev = uprev_ref[...]
        ulast_ref[...] = u
    rkv = u + (uprev - u) * murkv_ref[...]
    dr = r_ref.shape[1]
    r = rkv[:, :dr]
    k = rkv[:, dr:2 * dr]
    v = rkv[:, 2 * dr:]
    ones_bd = _head_mask(MXU_DIM, BF16)
    kk = k * kk_ref[...]
    kk = kk * lax.rsqrt(jnp.maximum(_head_sum(kk * kk, ones_bd), 1e-24))
    k = k * (1.0 + (asig - 1.0) * ka_ref[...])
    r_ref[...] = r
    k_ref[...] = k
    v_ref[...] = v
    a_ref[...] = -kk
    b_ref[...] = kk * asig
    bonus_ref[...] = _head_sum(r * k * rk_ref[...], ones_bd) * v


def _inproj(x, xprev, uprev, seq_len, lw):
    n, d = x.shape
    d_in = lw["w_in"].shape[1]
    dr = lw["w_B"].shape[1]
    d3 = 3 * dr
    dc = (d_in - d3) // 2
    halo = xprev is None
    tm = _row_tile(seq_len if halo else n, 256)
    seq_tiles = (seq_len // tm) if halo else 1
    row = lambda c: pl.BlockSpec((tm, c), lambda i: (i, 0))
    if halo:
        prev_specs = [pl.BlockSpec((SUBLANES, d), lambda i: (jnp.maximum(i * (tm // SUBLANES) - 1, 0), 0)),
                      pl.BlockSpec((SUBLANES, d3), lambda i: (0, 0))]
        prev_args = [x, jnp.zeros((SUBLANES, d3), F32)]
        ulast_spec = pl.BlockSpec((SUBLANES, d3), lambda i: (i, 0))
        ulast_shape = jax.ShapeDtypeStruct((n // tm * SUBLANES, d3), F32)
    else:
        prev_specs = [row(d), row(d3)]
        prev_args = [xprev, uprev]
        ulast_spec = row(d3)
        ulast_shape = jax.ShapeDtypeStruct((n, d3), F32)
    consts = [lw["w_in"], lw["mu_x"], lw["w_A"], lw["a_A"], lw["g_A"], lw["w_B"], lw["a_B"], lw["g_B"],
              lw["w0"], lw["a0"], lw["mu_rkv"], lw["k_k"], lw["k_a"], lw["r_k"]]
    widths = (dc, dr, dr) + (dr,) * 6
    return pl.pallas_call(
        functools.partial(_inproj_kernel, seq_tiles=seq_tiles, halo=halo),
        grid=(n // tm,),
        in_specs=[row(d)] + prev_specs + [_const_spec(c.shape) for c in consts],
        out_specs=[ulast_spec] + [row(c) for c in widths],
        out_shape=[ulast_shape] + [jax.ShapeDtypeStruct((n, c), F32) for c in widths],
        scratch_shapes=[pltpu.VMEM((SUBLANES, d3), F32)],
        compiler_params=_params("arbitrary"),
        name="inproj",
    )(x, *prev_args, *consts)


def _blockdiag(x, mask):
    reps = mask.shape[0] // x.shape[0]
    return jnp.concatenate([x] * reps, axis=0) * mask


def _dot_nt(a, b):
    return lax.dot_general(a, b, (((1,), (1,)), ((), ())), preferred_element_type=F32)


def _dot_tn(a, b):
    return lax.dot_general(a, b, (((0,), (0,)), ((), ())), preferred_element_type=F32)


def _scan_kernel(r_ref, dl_ref, k_ref, v_ref, a_ref, b_ref, y_ref, s_ref, h_scr):
    c = pl.program_id(0)
    nb = r_ref.shape[0]
    C = r_ref.shape[1]
    W = MXU_DIM
    ng = r_ref.shape[2] // W

    @pl.when(c == 0)
    def _():
        h_scr[...] = jnp.zeros_like(h_scr)

    ti = lax.broadcasted_iota(jnp.int32, (C, C), 0)
    tj = lax.broadcasted_iota(jnp.int32, (C, C), 1)
    tri = (ti >= tj).astype(BF16)
    t_row = lax.broadcasted_iota(jnp.int32, (C, W), 0)
    j_col = lax.broadcasted_iota(jnp.int32, (C, W), 1) % C
    strict = j_col < t_row
    incl = j_col <= t_row
    eye_cat = (j_col == t_row).astype(F32)
    rb = lax.broadcasted_iota(jnp.int32, (HEADS_PER_GROUP * C, W), 0) // C
    cb = lax.broadcasted_iota(jnp.int32, (HEADS_PER_GROUP * C, W), 1) // RWKV_HEAD
    bmask = (rb == cb).astype(BF16)
    hmask = _head_mask(W, F32)

    chains = [(bi, gi) for bi in range(nb) for gi in range(ng)]
    each = lambda f, *lists: [f(*xs) for xs in zip(*lists)]
    load = lambda ref: [ref[bi, :, gi * W:(gi + 1) * W] for bi, gi in chains]
    r, dl, k, v, a, b = (load(ref) for ref in (r_ref, dl_ref, k_ref, v_ref, a_ref, b_ref))
    sbd = [h_scr[bi, gi] for bi, gi in chains]
    bd = lambda x: _blockdiag(x, bmask)

    def cumsum(d):
        d_hi = d.astype(BF16)
        d_r1 = d - d_hi.astype(F32)
        d_mid = d_r1.astype(BF16)
        d_lo = (d_r1 - d_mid.astype(F32)).astype(BF16)
        return _dot(tri, d_hi) + (_dot(tri, d_mid) + _dot(tri, d_lo))

    cum = each(cumsum, dl)
    cum_last = each(lambda x: x[C - 1:C, :], cum)
    e_neg = each(lambda x: jnp.exp(-x), cum)
    e_end = each(lambda x, xl: jnp.exp(xl - x), cum, cum_last)
    at = each(lambda x, cu, d: (x * jnp.exp(cu - d)).astype(BF16), a, cum, dl)
    rt = each(lambda x, cu: (x * jnp.exp(cu)).astype(BF16), r, cum)
    bt = each(lambda x, e: (x * e).astype(BF16), b, e_neg)
    kt = each(lambda x, e: (x * e).astype(BF16), k, e_neg)
    bh = each(lambda x, e: (x * e).astype(BF16), b, e_end)
    kh = each(lambda x, e: (x * e).astype(BF16), k, e_end)
    vb = each(lambda x: x.astype(BF16), v)
    ar = each(lambda x, y: jnp.concatenate([x, y], axis=0), at, rt)

    p_b = each(lambda x, y: _dot_nt(x, bd(y)), ar, bt)
    p_k = each(lambda x, y: _dot_nt(x, bd(y)), ar, kt)
    arh = each(lambda x, s: _dot_nt(x, s.astype(BF16)), ar, sbd)
    p_ab = each(lambda p: jnp.where(strict, p[:C], 0.0), p_b)
    p_rb = each(lambda p: jnp.where(incl, p[C:], 0.0).astype(BF16), p_b)
    p_akrk = each(lambda p: jnp.concatenate([jnp.where(strict, p[:C], 0.0), jnp.where(incl, p[C:], 0.0)],
                                            axis=0).astype(BF16), p_k)

    pv = each(lambda p, x: _dot(p, bd(x)), p_akrk, vb)
    n_sq = int(math.log2(C))
    nn_b = each(lambda p: p.astype(BF16), p_ab)
    nn_b = each(lambda n: _dot(n, bd(n)).astype(BF16), nn_b)
    w = each(lambda x, y: (x[:C] + y[:C]).astype(BF16), arh, pv)
    tm = each(lambda p: eye_cat + p, p_ab)
    for i in range(1, n_sq):
        last = i == n_sq - 1
        lhs = each(lambda t, n: t.astype(BF16) if last else jnp.concatenate([t.astype(BF16), n], axis=0), tm, nn_b)
        prod = each(lambda l, n: _dot(l, bd(n)), lhs, nn_b)
        tm = each(lambda t, p: t + p[:C], tm, prod)
        if not last:
            nn_b = each(lambda p: p[C:].astype(BF16), prod)
    ub = each(lambda t, x: _dot(t.astype(BF16), bd(x)).astype(BF16), tm, w)

    yv = each(lambda x, y, p, u: x[C:] + y[C:] + _dot(p, bd(u)), arh, pv, p_rb, ub)
    upd = each(lambda u, x, y, z: _dot_tn(jnp.concatenate([u, x], axis=0), jnp.concatenate([y, z], axis=0)),
               ub, vb, bh, kh)
    for (bi, gi), y, s, xl, up in zip(chains, yv, sbd, cum_last, upd):
        y_ref[bi, :, gi * W:(gi + 1) * W] = y
        h_scr[bi, gi] = (s * jnp.exp(xl) + up) * hmask

    @pl.when(c == pl.num_programs(0) - 1)
    def _():
        s_ref[...] = h_scr[...]


def _scan(r, dl, k, v, a, b):
    nb, t, dr = r.shape
    ng = dr // MXU_DIM
    C = SCAN_CHUNK
    spec = pl.BlockSpec((nb, C, dr), lambda c: (0, c, 0))
    return pl.pallas_call(
        _scan_kernel,
        grid=(t // C,),
        in_specs=[spec] * 6,
        out_specs=[spec, pl.BlockSpec((nb, ng, MXU_DIM, MXU_DIM), lambda c: (0, 0, 0, 0))],
        out_shape=[jax.ShapeDtypeStruct((nb, t, dr), F32),
                   jax.ShapeDtypeStruct((nb, ng, MXU_DIM, MXU_DIM), F32)],
        scratch_shapes=[pltpu.VMEM((nb, ng, MXU_DIM, MXU_DIM), F32)],
        compiler_params=_params("arbitrary"),
        name="wkv_scan",
    )(r, dl, k, v, a, b)


def _step_kernel(s_ref, r_ref, dl_ref, k_ref, v_ref, a_ref, b_ref, sout_ref, y_ref):
    S = s_ref[0]
    per_key = lambda ref: ref[...][None, :, :]
    sa = jnp.sum(S * per_key(a_ref), axis=1, keepdims=True)
    v = v_ref[...][:, None, :]
    s_new = S * jnp.exp(per_key(dl_ref)) + sa * per_key(b_ref) + v * per_key(k_ref)
    sout_ref[0] = s_new
    y_ref[...] = jnp.sum(s_new * per_key(r_ref), axis=1)


def _step(S0, r, dl, k, v, a, b):
    nb, nh, n, _ = S0.shape
    st = jnp.transpose(S0, (1, 2, 3, 0))
    sspec = pl.BlockSpec((1, n, n, nb), lambda h: (h, 0, 0, 0))
    vspec = pl.BlockSpec((n, nb), lambda h: (h, 0))
    s_new, y = pl.pallas_call(
        _step_kernel,
        grid=(nh,),
        in_specs=[sspec] + [vspec] * 6,
        out_specs=[sspec, vspec],
        out_shape=[jax.ShapeDtypeStruct(st.shape, F32), jax.ShapeDtypeStruct((nh * n, nb), F32)],
        compiler_params=_params("arbitrary"),
        name="wkv_step",
    )(st, *(x.T for x in (r, dl, k, v, a, b)))
    return jnp.transpose(s_new, (3, 0, 1, 2)), y.T


def _post_kernel(y_ref, bonus_ref, g_ref, gng_ref, gnb_ref, o_ref):
    y = y_ref[...]
    ones_bd = _head_mask(MXU_DIM, BF16)
    inv_n = 1.0 / RWKV_HEAD
    mu = _head_sum(y, ones_bd) * inv_n
    yc = y - mu
    var = _head_sum(yc * yc, ones_bd) * inv_n
    yn = yc * lax.rsqrt(var + GN_EPS) * gng_ref[...] + gnb_ref[...]
    o_ref[...] = (yn + bonus_ref[...]) * g_ref[...]


def _post(y, bonus, g, lw):
    n, dr = y.shape
    tm = _row_tile(n, 512)
    row = pl.BlockSpec((tm, dr), lambda i: (i, 0))
    return pl.pallas_call(
        _post_kernel,
        grid=(n // tm,),
        in_specs=[row, row, row, _const_spec(lw["gn_g"].shape), _const_spec(lw["gn_b"].shape)],
        out_specs=row,
        out_shape=jax.ShapeDtypeStruct((n, dr), F32),
        compiler_params=_params("arbitrary"),
        name="wkv_post",
    )(y, bonus, g, lw["gn_g"], lw["gn_b"])


def _conv_seq_kernel(h_ref, halo_ref, w_ref, b_ref, lng_ref, lnb_ref, o_ref, hp_scr, *, seq_tiles):
    tt = h_ref.shape[0]
    first = (pl.program_id(0) % seq_tiles) == 0
    hp_scr[0:CONV_HALO, :] = jnp.where(first, 0.0, halo_ref[...])
    hp_scr[CONV_HALO:, :] = h_ref[...]
    off = CONV_HALO - (CONV_WIDTH - 1)
    rc = CONV_ROW_CHUNK if tt % CONV_ROW_CHUNK == 0 else tt
    for c0 in range(0, o_ref.shape[1], LANES):
        lanes = slice(c0, c0 + LANES)
        for t0 in range(0, tt, rc):
            acc = jnp.broadcast_to(b_ref[:, lanes], (rc, LANES))
            for p in range(SUBLANES):
                rows = rc if p == 0 else rc + SUBLANES
                part = None
                for m in range((CONV_HALO + SUBLANES) // SUBLANES):
                    j = SUBLANES * m + p - off
                    if 0 <= j < CONV_WIDTH:
                        term = hp_scr[t0 + SUBLANES * m:t0 + SUBLANES * m + rows, lanes] * w_ref[j:j + 1, lanes]
                        part = term if part is None else part + term
                acc = acc + part[p:p + rc]
            o_ref[t0:t0 + rc, lanes] = acc
    z = _layer_norm(o_ref[...], lng_ref[...], lnb_ref[...])
    o_ref[...] = z * jax.nn.sigmoid(z)


def _conv_seq(h, seq_len, lw):
    n, dc = h.shape
    tt = _row_tile(seq_len, 256)
    assert tt % CONV_HALO == 0
    seq_tiles = seq_len // tt
    row = pl.BlockSpec((tt, dc), lambda i: (i, 0))
    halo = pl.BlockSpec((CONV_HALO, dc), lambda i: (jnp.maximum(i * (tt // CONV_HALO) - 1, 0), 0))
    consts = [lw["conv_w"], lw["conv_b"], lw["conv_ln_g"], lw["conv_ln_b"]]
    return pl.pallas_call(
        functools.partial(_conv_seq_kernel, seq_tiles=seq_tiles),
        grid=(n // tt,),
        in_specs=[row, halo] + [_const_spec(c.shape) for c in consts],
        out_specs=row,
        out_shape=jax.ShapeDtypeStruct((n, dc), F32),
        scratch_shapes=[pltpu.VMEM((CONV_HALO + tt, dc), F32)],
        compiler_params=_params("arbitrary"),
        name="conv_seq",
    )(h, h, *consts)


def _conv_step_kernel(buf_ref, h_ref, w_ref, b_ref, lng_ref, lnb_ref, o_ref, nbuf_ref):
    h = h_ref[...]
    acc = b_ref[...] + h * w_ref[CONV_WIDTH - 1:CONV_WIDTH, :]
    for j in range(CONV_WIDTH - 1):
        tap = buf_ref[j]
        acc = acc + tap * w_ref[j:j + 1, :]
        if j > 0:
            nbuf_ref[j - 1] = tap
    nbuf_ref[CONV_WIDTH - 2] = h
    z = _layer_norm(acc, lng_ref[...], lnb_ref[...])
    o_ref[...] = z * jax.nn.sigmoid(z)


def _conv_step(buf, h, lw):
    nb, wm1, dc = buf.shape
    bb = _row_tile(nb, 32)
    bspec = pl.BlockSpec((wm1, bb, dc), lambda i: (0, i, 0))
    row = pl.BlockSpec((bb, dc), lambda i: (i, 0))
    consts = [lw["conv_w"], lw["conv_b"], lw["conv_ln_g"], lw["conv_ln_b"]]
    out, nbuf = pl.pallas_call(
        _conv_step_kernel,
        grid=(nb // bb,),
        in_specs=[bspec, row] + [_const_spec(c.shape) for c in consts],
        out_specs=[row, bspec],
        out_shape=[jax.ShapeDtypeStruct((nb, dc), F32), jax.ShapeDtypeStruct((wm1, nb, dc), F32)],
        compiler_params=_params("arbitrary"),
        name="conv_step",
    )(jnp.transpose(buf, (1, 0, 2)), h, *consts)
    return out, jnp.transpose(nbuf, (1, 0, 2))


def _outproj_kernel(*refs, alpha, n_dst):
    x_ref, yr_ref, yc_ref, wo_ref, g_ref, b_ref, wr_ref, br_ref, h_ref, logit_ref = refs[n_dst:]
    ymix = jnp.concatenate([yr_ref[...].astype(BF16), yc_ref[...].astype(BF16)], axis=1)
    h = _layer_norm(alpha * x_ref[...] + _dot(ymix, wo_ref[...]), g_ref[...], b_ref[...])
    h_ref[...] = h
    logit_ref[...] = _dot(h.astype(BF16), wr_ref[...]) + br_ref[...]


def _outproj(x, y_rwkv, y_conv, lw, alpha, n_total, row0, dst=()):
    n, d = x.shape
    dr = y_rwkv.shape[1]
    dc = y_conv.shape[1]
    ne = lw["w_router"].shape[1]
    tm = _row_tile(math.gcd(n, row0) if row0 else n, 256)
    blk0 = row0 // tm
    row = lambda c: pl.BlockSpec((tm, c), lambda i: (i, 0))
    out_row = lambda c: pl.BlockSpec((tm, c), lambda i: (blk0 + i, 0))
    consts = [lw["w_out"], lw["ln1_g"], lw["ln1_b"], lw["w_router"], lw["b_router"]]
    return pl.pallas_call(
        functools.partial(_outproj_kernel, alpha=alpha, n_dst=len(dst)),
        grid=(n // tm,),
        in_specs=[pl.BlockSpec(memory_space=pl.ANY)] * len(dst) + [row(d), row(dr), row(dc)]
        + [_const_spec(c.shape) for c in consts],
        out_specs=[out_row(d), out_row(ne)],
        out_shape=[jax.ShapeDtypeStruct((n_total, d), F32), jax.ShapeDtypeStruct((n_total, ne), F32)],
        input_output_aliases={i: i for i in range(len(dst))},
        compiler_params=_params("arbitrary"),
        name="outproj",
    )(*dst, x, y_rwkv, y_conv, *consts)


def _route(logits, moe_tm):
    n_tok, ne = logits.shape
    top_val, top_idx = lax.top_k(logits, TOP_K)
    gates = jax.nn.softmax(top_val, axis=-1)
    n_assign = n_tok * TOP_K
    e_flat = top_idx.reshape(n_assign).astype(jnp.int32)
    onehot = (e_flat[:, None] == jnp.arange(ne, dtype=jnp.int32)[None, :]).astype(jnp.int32)
    counts = onehot.sum(0)
    rank = jnp.take_along_axis(jnp.cumsum(onehot, axis=0), e_flat[:, None], axis=1)[:, 0] - 1
    nblk_e = (counts + moe_tm - 1) // moe_tm
    blk_end = jnp.cumsum(nblk_e)
    blk_start = blk_end - nblk_e
    n_active = blk_end[-1]
    nb_max = n_assign // moe_tm + ne
    pos = blk_start[e_flat] * moe_tm + rank
    sorted_tok = (jnp.argsort(e_flat, stable=True) // TOP_K).astype(jnp.int32)
    sorted_tok = jnp.concatenate([sorted_tok, jnp.zeros((GATHER_GROUP,), jnp.int32)])
    cstart = jnp.cumsum(counts) - counts
    q = jnp.arange(nb_max, dtype=jnp.int32)
    q_eff = jnp.minimum(q, n_active - 1)
    blk_e = jnp.minimum(jnp.searchsorted(blk_end, q_eff, side="right"), ne - 1).astype(jnp.int32)
    row_in_e = (q_eff - blk_start[blk_e]) * moe_tm
    valid = jnp.clip(counts[blk_e] - row_in_e, 0, moe_tm)
    valid = jnp.where(q < n_active, valid, 0).astype(jnp.int32)
    src_start = (cstart[blk_e] + row_in_e).astype(jnp.int32)
    return gates, pos.astype(jnp.int32), sorted_tok, src_start, blk_e, q_eff.astype(jnp.int32), valid


def _gather_kernel(qeff_ref, valid_ref, src_ref, tok_ref, h_hbm, o_ref, buf, sem):
    s = pl.program_id(0)
    nb = pl.num_programs(0) - 1
    per = GATHER_GROUP // SUBLANES
    groups = lambda nvalid: (nvalid + GATHER_GROUP - 1) // GATHER_GROUP

    @pl.when(s < nb)
    def _():
        slot = s % 2
        base = src_ref[s]

        def issue(g, c):
            for i in range(GATHER_GROUP):
                tok = tok_ref[base + g * GATHER_GROUP + i]
                pltpu.make_async_copy(h_hbm.at[pl.ds(tok, 1)],
                                      buf.at[slot, g * per + i // SUBLANES, pl.ds(i % SUBLANES, 1)],
                                      sem.at[slot]).start()
            return c

        lax.fori_loop(0, groups(valid_ref[s]), issue, 0)

    @pl.when(s > 0)
    def _():
        slot = (s - 1) % 2
        nvalid = valid_ref[s - 1]

        @pl.when(nvalid > 0)
        def _():
            def wait(g, c):
                pltpu.make_async_copy(h_hbm.at[pl.ds(0, SUBLANES)], buf.at[slot, g], sem.at[slot]).wait()
                return c

            lax.fori_loop(0, groups(nvalid) * per, wait, 0)
            rows = lax.broadcasted_iota(jnp.int32, o_ref.shape, 0)
            o_ref[...] = jnp.where(rows < nvalid, buf[slot].reshape(o_ref.shape), 0.0).astype(o_ref.dtype)


def _gather_rows(h, sorted_tok, src_start, q_eff, valid, moe_tm):
    n_tok, d = h.shape
    nb_max = q_eff.shape[0]
    grid_spec = pltpu.PrefetchScalarGridSpec(
        num_scalar_prefetch=4,
        grid=(nb_max + 1,),
        in_specs=[pl.BlockSpec(memory_space=pl.ANY)],
        out_specs=pl.BlockSpec((moe_tm, d), lambda s, qe, va, sr, tk: (qe[jnp.maximum(s - 1, 0)], 0)),
        scratch_shapes=[pltpu.VMEM((2, moe_tm // SUBLANES, SUBLANES, d), F32), pltpu.SemaphoreType.DMA((2,))],
    )
    return pl.pallas_call(
        _gather_kernel,
        grid_spec=grid_spec,
        out_shape=jax.ShapeDtypeStruct((nb_max * moe_tm, d), BF16),
        compiler_params=_params("arbitrary"),
        name="moe_gather",
    )(q_eff, valid, src_start, sorted_tok, h)


def _moe_kernel(be_ref, qeff_ref, valid_ref, x_ref, wg_ref, wl_ref, bg_ref, bl_ref, wd_ref, bd_ref, o_ref,
                wg_s, wl_s, wd_s, *, sub):
    q = pl.program_id(0)
    j = pl.program_id(1)
    nvalid = valid_ref[q]
    unit = sub // 4
    n_units = (nvalid + unit - 1) // unit
    n_all = o_ref.shape[0] // unit

    def fill(s, value):
        r0 = pl.multiple_of(s * unit, unit)
        o_ref[pl.ds(r0, unit), :] = jnp.broadcast_to(value, (unit, o_ref.shape[1]))

    def rows_block(r0, rows):
        x = x_ref[pl.ds(r0, rows), :]
        g = jnp.minimum(_dot(x, wg_s[...]) + bg_ref[0], SWIGLU_LIMIT)
        l = jnp.clip(_dot(x, wl_s[...]) + bl_ref[0], -SWIGLU_LIMIT, SWIGLU_LIMIT)
        act = g * jax.nn.sigmoid(SWIGLU_ALPHA * g) * (l + 1.0)
        o_ref[pl.ds(r0, rows), :] += _dot(act.astype(BF16), wd_s[...])

    @pl.when(nvalid > 0)
    def _():
        @pl.when(j == 0)
        def _():
            lax.fori_loop(0, n_units, lambda s, c: (fill(s, bd_ref[0]), c)[1], 0)
            lax.fori_loop(n_units, n_all, lambda s, c: (fill(s, jnp.zeros((1, 1), F32)), c)[1], 0)

        wg_s[...] = wg_ref[0].astype(BF16)
        wl_s[...] = wl_ref[0].astype(BF16)
        wd_s[...] = wd_ref[0].astype(BF16)
        rows_block(0, unit)
        rest = n_units - 1
        n_pairs = rest // 8
        tail = rest - n_pairs * 8

        def pair(i, c):
            r0 = pl.multiple_of(unit + i * 2 * sub, unit)
            rows_block(r0, sub)
            rows_block(r0 + sub, sub)
            return c

        lax.fori_loop(0, n_pairs, pair, 0)
        t0 = unit + n_pairs * 2 * sub
        for units in (4, 2, 1):
            @pl.when((tail & units) != 0)
            def _():
                rows_block(pl.multiple_of(t0 + (tail & ~(2 * units - 1)) * unit, unit), units * unit)


def _moe_experts(x_sorted, blk_e, q_eff, valid, lw, moe_tm, sub, tf):
    n_rows, d = x_sorted.shape
    ne, _, f2 = lw["w_gu"].shape
    f = f2 // 2
    nf = f // tf
    nb_max = q_eff.shape[0]
    b_gu = lw["b_gu"].reshape(ne, 1, f2)
    b_down = lw["b_down"].reshape(ne, 1, d)

    def jf(q, j, va):
        return jnp.where(va[q] > 0, j, nf - 1)

    grid_spec = pltpu.PrefetchScalarGridSpec(
        num_scalar_prefetch=3,
        grid=(nb_max, nf),
        in_specs=[
            pl.BlockSpec((moe_tm, d), lambda q, j, be, qe, va: (qe[q], 0)),
            pl.BlockSpec((1, d, tf), lambda q, j, be, qe, va: (be[q], 0, jf(q, j, va))),
            pl.BlockSpec((1, d, tf), lambda q, j, be, qe, va: (be[q], 0, nf + jf(q, j, va))),
            pl.BlockSpec((1, 1, tf), lambda q, j, be, qe, va: (be[q], 0, jf(q, j, va))),
            pl.BlockSpec((1, 1, tf), lambda q, j, be, qe, va: (be[q], 0, nf + jf(q, j, va))),
            pl.BlockSpec((1, tf, d), lambda q, j, be, qe, va: (be[q], jf(q, j, va), 0)),
            pl.BlockSpec((1, 1, d), lambda q, j, be, qe, va: (be[q], 0, 0)),
        ],
        out_specs=pl.BlockSpec((moe_tm, d), lambda q, j, be, qe, va: (qe[q], 0)),
        scratch_shapes=[pltpu.VMEM((d, tf), BF16), pltpu.VMEM((d, tf), BF16), pltpu.VMEM((tf, d), BF16)],
    )
    return pl.pallas_call(
        functools.partial(_moe_kernel, sub=sub),
        grid_spec=grid_spec,
        out_shape=jax.ShapeDtypeStruct((n_rows, d), F32),
        compiler_params=_params("arbitrary", "arbitrary", vmem_limit_bytes=MOE_VMEM_LIMIT_BYTES),
        name="moe_experts",
    )(blk_e, q_eff, valid, x_sorted, lw["w_gu"], lw["w_gu"], b_gu, b_gu, lw["w_down"], b_down)


def _combine_kernel(pos_ref, h_ref, gates_ref, yrows_hbm, g_ref, b_ref, op_ref, os_ref, *scratch,
                    alpha, n_first, n_tiles):
    i = pl.program_id(0)
    tm = h_ref.shape[0]
    n_slots = COMBINE_LAG + 1
    slots = tuple(zip(scratch[:n_slots], scratch[n_slots:]))

    def fetch(tile, buf, sem):
        for r in range(tm):
            for k in range(TOP_K):
                p = pos_ref[(tile * tm + r) * TOP_K + k]
                pltpu.make_async_copy(yrows_hbm.at[pl.ds(p, 1)],
                                      buf.at[k, r // SUBLANES, pl.ds(r % SUBLANES, 1)], sem).start()

    def wait(buf, sem):
        for k in range(TOP_K):
            pltpu.make_async_copy(buf.at[k], buf.at[k], sem).wait()

    def finish(tile, buf):
        gates = gates_ref[...]
        slot = lambda k: buf[k].reshape(tm, buf.shape[-1])
        ffn = slot(0) * gates[:, 0:1]
        for k in range(1, TOP_K):
            ffn = ffn + slot(k) * gates[:, k:k + 1]
        y = _layer_norm(alpha * h_ref[...] + ffn, g_ref[...], b_ref[...])

        @pl.when(tile < n_first)
        def _():
            op_ref[...] = y

        @pl.when(tile >= n_first)
        def _():
            os_ref[...] = y

    for step in range(min(COMBINE_LAG, n_tiles)):
        @pl.when(i == step)
        def _():
            fetch(step, *slots[step % n_slots])

    for phase in range(n_slots):
        @pl.when((i >= COMBINE_LAG) & (i < n_tiles) & (i % n_slots == phase))
        def _():
            done = slots[(phase - COMBINE_LAG) % n_slots]
            wait(*done)
            fetch(i, *slots[phase])
            finish(i - COMBINE_LAG, done[0])

    for step in range(max(n_tiles, COMBINE_LAG), n_tiles + COMBINE_LAG):
        @pl.when(i == step)
        def _():
            done = slots[(step - COMBINE_LAG) % n_slots]
            wait(*done)
            finish(step - COMBINE_LAG, done[0])


def _combine(h, gates, pos, y_rows, lw, alpha, n_prompt):
    n_tok, d = h.shape
    tm = _row_tile(math.gcd(n_prompt, n_tok - n_prompt), 128)
    n_first = n_prompt // tm
    n_tiles = n_tok // tm
    done = lambda i: jnp.maximum(i - COMBINE_LAG, 0)
    n_slots = COMBINE_LAG + 1
    buf = pltpu.VMEM((TOP_K, tm // SUBLANES, SUBLANES, d), F32)
    grid_spec = pltpu.PrefetchScalarGridSpec(
        num_scalar_prefetch=1,
        grid=(n_tiles + COMBINE_LAG,),
        in_specs=[
            pl.BlockSpec((tm, d), lambda i, p: (done(i), 0)),
            pl.BlockSpec((tm, TOP_K), lambda i, p: (done(i), 0)),
            pl.BlockSpec(memory_space=pl.ANY),
            pl.BlockSpec((1, d), lambda i, p: (0, 0)),
            pl.BlockSpec((1, d), lambda i, p: (0, 0)),
        ],
        out_specs=[pl.BlockSpec((tm, d), lambda i, p: (jnp.minimum(done(i), n_first - 1), 0)),
                   pl.BlockSpec((tm, d), lambda i, p: (jnp.maximum(done(i) - n_first, 0), 0))],
        scratch_shapes=[buf] * n_slots + [pltpu.SemaphoreType.DMA(())] * n_slots,
    )
    return pl.pallas_call(
        functools.partial(_combine_kernel, alpha=alpha, n_first=n_first, n_tiles=n_tiles),
        grid_spec=grid_spec,
        out_shape=[jax.ShapeDtypeStruct((n_prompt, d), F32), jax.ShapeDtypeStruct((n_tok - n_prompt, d), F32)],
        compiler_params=_params("arbitrary"),
        name="moe_combine",
    )(pos, h, gates, y_rows, lw["ln2_g"], lw["ln2_b"])


def _moe_tiles(n_assign, ne, d, f):
    unit = MOE_SUB // 2
    mean = -(-n_assign // ne)
    tm = max(MOE_SUB, -(-(mean * 11 // 10) // unit) * unit)
    for tf in (512, 256):
        tf = min(tf, f)
        blocks = 2 * tm * d * 2 + 2 * tm * d * 4
        weights = 3 * d * tf * (2 * 4 + 2)
        temps = MOE_SUB * 2 * tf * 4
        if blocks + weights + temps <= MOE_VMEM_LIMIT_BYTES:
            return tm, tf
    return min(tm, 4 * MOE_SUB), min(256, f)


_VEC_PARAMS = ("mu_rkv", "w0", "a0", "k_k", "k_a", "r_k", "gn_g", "gn_b", "conv_b", "conv_ln_g", "conv_ln_b",
               "ln1_g", "ln1_b", "b_router", "ln2_g", "ln2_b")
_BF16_PARAMS = ("w_in", "w_A", "w_B", "a_A", "a_B", "g_A", "g_B", "w_out", "w_router")


def _diag_blocks(s):
    nb, ng = s.shape[:2]
    s6 = s.reshape(nb, ng, HEADS_PER_GROUP, RWKV_HEAD, HEADS_PER_GROUP, RWKV_HEAD)
    d = jnp.stack([s6[:, :, h, :, h, :] for h in range(HEADS_PER_GROUP)], axis=2)
    return d.reshape(nb, ng * HEADS_PER_GROUP, RWKV_HEAD, RWKV_HEAD)


def _layer(xp, xs, sx, srkv, swkv, sconv, lw, alpha):
    nbp, t, d = xp.shape
    nbs = xs.shape[0]
    xp2 = xp.reshape(nbp * t, d)
    xs2 = xs.reshape(nbs, d)

    up_last, hgp, dlp, gp, rp, kp, vp, ap, bp, bonp = _inproj(xp2, None, None, t, lw)
    us, hgs, dls, gs, rs, ks, vs, as_, bs, bons = _inproj(xs2, sx, srkv, 1, lw)
    dr = dlp.shape[1]

    seq = lambda z: z.reshape(nbp, t, dr)
    yp_raw, s_end = _scan(seq(rp), seq(dlp), seq(kp), seq(vp), seq(ap), seq(bp))
    s_new, ys_raw = _step(swkv, rs, dls, ks, vs, as_, bs)
    yrp = _post(yp_raw.reshape(nbp * t, dr), bonp, gp, lw)
    yrs = _post(ys_raw, bons, gs, lw)

    ycp = _conv_seq(hgp, t, lw)
    ycs, nbuf = _conv_step(sconv, hgs, lw)

    n_prompt = nbp * t
    n_tok = n_prompt + nbs
    dst = _outproj(xp2, yrp, ycp, lw, alpha, n_tok, 0)
    h_all, logits = _outproj(xs2, yrs, ycs, lw, alpha, n_tok, n_prompt, dst=tuple(dst))

    ne = logits.shape[1]
    moe_tm, moe_tf = _moe_tiles(n_tok * TOP_K, ne, d, lw["w_down"].shape[1])
    gates, pos, sorted_tok, src_start, blk_e, q_eff, valid = _route(logits, moe_tm)
    x_sorted = _gather_rows(h_all, sorted_tok, src_start, q_eff, valid, moe_tm)
    y_rows = _moe_experts(x_sorted, blk_e, q_eff, valid, lw, moe_tm, MOE_SUB, moe_tf)
    yp, ys = _combine(h_all, gates, pos, y_rows, lw, alpha, n_prompt)
    yp = yp.reshape(nbp, t, d)
    ys = ys.reshape(nbs, 1, d)
    p_state = (xp[:, -1], up_last.reshape(nbp, -1, SUBLANES, 3 * dr)[:, -1, -1], _diag_blocks(s_end),
               hgp.reshape(nbp, t, -1)[:, t - (CONV_WIDTH - 1):])
    s_state = (xs2, us, s_new, nbuf)
    return yp, ys, p_state, s_state


def kernel(x_prompt, x_sample, state_shift_x, state_shift_rkv, state_wkv, state_conv, w_in, mu_x, mu_rkv, w0, w_A,
           w_B, a0, a_A, a_B, g_A, g_B, k_k, k_a, r_k, gn_g, gn_b, conv_w, conv_b, conv_ln_g, conv_ln_b, w_out,
           ln1_g, ln1_b, w_router, b_router, w_gu, b_gu, w_down, b_down, ln2_g, ln2_b):
    params = dict(w_in=w_in, mu_x=mu_x, mu_rkv=mu_rkv, w0=w0, w_A=w_A, w_B=w_B, a0=a0, a_A=a_A, a_B=a_B, g_A=g_A,
                  g_B=g_B, k_k=k_k, k_a=k_a, r_k=r_k, gn_g=gn_g, gn_b=gn_b, conv_w=conv_w, conv_b=conv_b,
                  conv_ln_g=conv_ln_g, conv_ln_b=conv_ln_b, w_out=w_out, ln1_g=ln1_g, ln1_b=ln1_b,
                  w_router=w_router, b_router=b_router, w_gu=w_gu, b_gu=b_gu, w_down=w_down, b_down=b_down,
                  ln2_g=ln2_g, ln2_b=ln2_b)
    depth = w_in.shape[0]
    assert x_sample.shape[1] == 1, "the sample group advances one token per step"
    alpha = (2.0 * depth) ** 0.25
    xp, xs = x_prompt, x_sample
    p_states, s_states = [], []
    for l in range(depth):
        lw = {name: p[l] for name, p in params.items()}
        for name in _VEC_PARAMS:
            lw[name] = lw[name].reshape(1, -1)
        for name in _BF16_PARAMS:
            lw[name] = lw[name].astype(BF16)
        xp, xs, p_st, s_st = _layer(xp, xs, state_shift_x[l], state_shift_rkv[l], state_wkv[l], state_conv[l],
                                    lw, alpha)
        p_states.append(p_st)
        s_states.append(s_st)
    stack = lambda states, i: jnp.stack([st[i] for st in states])
    return (xp, xs,
            stack(p_states, 0), stack(p_states, 1), stack(p_states, 2), stack(p_states, 3),
            stack(s_states, 0), stack(s_states, 1), stack(s_states, 2), stack(s_states, 3))
```

```python
import functools
import math

import jax
import jax.numpy as jnp
from jax import lax
from jax.experimental import pallas as pl
from jax.experimental.pallas import tpu as pltpu

F32 = jnp.float32
BF16 = jnp.bfloat16

RWKV_HEAD = 64
CONV_WIDTH = 31
TOP_K = 4
SWIGLU_LIMIT = 7.0
SWIGLU_ALPHA = 1.702
LN_EPS = 1e-5
GN_EPS = 64e-5

LANES = 128
SUBLANES = 8
MXU_DIM = 256
VMEM_BYTES = 64 * 1024 * 1024
VMEM_LIMIT_BYTES = 56 * 1024 * 1024
MOE_VMEM_LIMIT_BYTES = VMEM_BYTES - 3 * 1024 * 1024

SCAN_CHUNK = 64
HEADS_PER_GROUP = MXU_DIM // RWKV_HEAD
CONV_HALO = 32
CONV_ROW_CHUNK = 128
GATHER_GROUP = 32
COMBINE_LAG = 2
MOE_SUB = 256


def _row_tile(n, target):
    best = None
    for t in range(SUBLANES, min(n, target) + 1, SUBLANES):
        if n % t == 0:
            best = t
    assert best is not None, (n, target)
    return best


def _params(*sem, vmem_limit_bytes=VMEM_LIMIT_BYTES):
    return pltpu.CompilerParams(dimension_semantics=sem, vmem_limit_bytes=vmem_limit_bytes)


def _const_spec(shape):
    nd = len(shape)
    return pl.BlockSpec(shape, lambda *_: (0,) * nd, pipeline_mode=pl.Buffered(1))


def _dot(a, b):
    return jnp.dot(a, b, preferred_element_type=F32)


def _split2(x):
    hi = x.astype(BF16)
    lo = (x - hi.astype(F32)).astype(BF16)
    return hi, lo


def _layer_norm(z, g, b):
    mu = jnp.mean(z, axis=-1, keepdims=True)
    zc = z - mu
    var = jnp.mean(zc * zc, axis=-1, keepdims=True)
    return zc * lax.rsqrt(var + LN_EPS) * g + b


def _shift_rows(x, first_row):
    xs = pltpu.roll(x, 1, axis=0)
    row = lax.broadcasted_iota(jnp.int32, x.shape, 0)
    return jnp.where(row == 0, first_row, xs)


def _head_mask(n, dtype):
    r = lax.broadcasted_iota(jnp.int32, (n, n), 0) // RWKV_HEAD
    c = lax.broadcasted_iota(jnp.int32, (n, n), 1) // RWKV_HEAD
    return (r == c).astype(dtype)


def _head_sum(x, ones_bd):
    cols = []
    for c0 in range(0, x.shape[1], MXU_DIM):
        hi, lo = _split2(x[:, c0:c0 + MXU_DIM])
        cols.append(_dot(hi, ones_bd) + _dot(lo, ones_bd))
    return cols[0] if len(cols) == 1 else jnp.concatenate(cols, axis=1)


def _inproj_kernel(x_ref, xprev_ref, uprev_ref, w_in_ref, mu_ref, wA_ref, aA_ref, gA_ref, wB_ref, aB_ref, gB_ref,
                   w0_ref, a0_ref, murkv_ref, kk_ref, ka_ref, rk_ref,
                   ulast_ref, hglu_ref, dlog_ref, g_ref, r_ref, k_ref, v_ref, a_ref, b_ref, bonus_ref,
                   ucarry_scr, *, seq_tiles, halo):
    x = x_ref[...]
    tm = x.shape[0]
    first = (pl.program_id(0) % seq_tiles) == 0
    if halo:
        prev_row = jnp.where(first, 0.0, xprev_ref[SUBLANES - 1:SUBLANES, :])
        xprev = _shift_rows(x, prev_row)
    else:
        xprev = xprev_ref[...]
    xx = xprev - x
    xb = x.astype(BF16)
    d3 = murkv_ref.shape[1]
    dc = hglu_ref.shape[1]
    u = _dot(xb, w_in_ref[:, :d3])
    val = _dot(xb, w_in_ref[:, d3:d3 + dc])
    gate = _dot(xb, w_in_ref[:, d3 + dc:])
    hglu_ref[...] = val * jax.nn.sigmoid(gate)

    xw = (x + xx * mu_ref[0:1, :]).astype(BF16)
    tw = jnp.tanh(_dot(xw, wA_ref[...]))
    wlin = w0_ref[...] + _dot(tw.astype(BF16), wB_ref[...])
    wlog = -jax.nn.softplus(-wlin) - 0.5
    dlog_ref[...] = -jnp.exp(wlog)

    xa = (x + xx * mu_ref[1:2, :]).astype(BF16)
    ta = _dot(xa, aA_ref[...])
    asig = jax.nn.sigmoid(a0_ref[...] + _dot(ta.astype(BF16), aB_ref[...]))

    xg = (x + xx * mu_ref[2:3, :]).astype(BF16)
    tg = jax.nn.sigmoid(_dot(xg, gA_ref[...]))
    g_ref[...] = _dot(tg.astype(BF16), gB_ref[...])

    if halo:
        @pl.when(pl.program_id(0) == 0)
        def _():
            ucarry_scr[...] = jnp.zeros_like(ucarry_scr)

        uprev = _shift_rows(u, jnp.where(first, 0.0, ucarry_scr[0:1, :]))
        ucarry_scr[0:1, :] = u[tm - 1:tm, :]
        ulast_ref[...] = u[tm - SUBLANES:tm, :]
    else:
        uprev = uprev_ref[...]
        ulast_ref[...] = u
    rkv = u + (uprev - u) * murkv_ref[...]
    dr = r_ref.shape[1]
    r = rkv[:, :dr]
    k = rkv[:, dr:2 * dr]
    v = rkv[:, 2 * dr:]
    ones_bd = _head_mask(MXU_DIM, BF16)
    kk = k * kk_ref[...]
    kk = kk * lax.rsqrt(jnp.maximum(_head_sum(kk * kk, ones_bd), 1e-24))
    k = k * (1.0 + (asig - 1.0) * ka_ref[...])
    r_ref[...] = r
    k_ref[...] = k
    v_ref[...] = v
    a_ref[...] = -kk
    b_ref[...] = kk * asig
    bonus_ref[...] = _head_sum(r * k * rk_ref[...], ones_bd) * v


def _inproj(x, xprev, uprev, seq_len, lw):
    n, d = x.shape
    d_in = lw["w_in"].shape[1]
    dr = lw["w_B"].shape[1]
    d3 = 3 * dr
    dc = (d_in - d3) // 2
    halo = xprev is None
    tm = _row_tile(seq_len if halo else n, 256)
    seq_tiles = (seq_len // tm) if halo else 1
    row = lambda c: pl.BlockSpec((tm, c), lambda i: (i, 0))
    if halo:
        prev_specs = [pl.BlockSpec((SUBLANES, d), lambda i: (jnp.maximum(i * (tm // SUBLANES) - 1, 0), 0)),
                      pl.BlockSpec((SUBLANES, d3), lambda i: (0, 0))]
        prev_args = [x, jnp.zeros((SUBLANES, d3), F32)]
        ulast_spec = pl.BlockSpec((SUBLANES, d3), lambda i: (i, 0))
        ulast_shape = jax.ShapeDtypeStruct((n // tm * SUBLANES, d3), F32)
    else:
        prev_specs = [row(d), row(d3)]
        prev_args = [xprev, uprev]
        ulast_spec = row(d3)
        ulast_shape = jax.ShapeDtypeStruct((n, d3), F32)
    consts = [lw["w_in"], lw["mu_x"], lw["w_A"], lw["a_A"], lw["g_A"], lw["w_B"], lw["a_B"], lw["g_B"],
              lw["w0"], lw["a0"], lw["mu_rkv"], lw["k_k"], lw["k_a"], lw["r_k"]]
    widths = (dc, dr, dr) + (dr,) * 6
    return pl.pallas_call(
        functools.partial(_inproj_kernel, seq_tiles=seq_tiles, halo=halo),
        grid=(n // tm,),
        in_specs=[row(d)] + prev_specs + [_const_spec(c.shape) for c in consts],
        out_specs=[ulast_spec] + [row(c) for c in widths],
        out_shape=[ulast_shape] + [jax.ShapeDtypeStruct((n, c), F32) for c in widths],
        scratch_shapes=[pltpu.VMEM((SUBLANES, d3), F32)],
        compiler_params=_params("arbitrary"),
        name="inproj",
    )(x, *prev_args, *consts)


def _blockdiag(x, mask):
    reps = mask.shape[0] // x.shape[0]
    return jnp.concatenate([x] * reps, axis=0) * mask


def _dot_nt(a, b):
    return lax.dot_general(a, b, (((1,), (1,)), ((), ())), preferred_element_type=F32)


def _dot_tn(a, b):
    return lax.dot_general(a, b, (((0,), (0,)), ((), ())), preferred_element_type=F32)


def _scan_kernel(r_ref, dl_ref, k_ref, v_ref, a_ref, b_ref, y_ref, s_ref, h_scr):
    c = pl.program_id(0)
    nb = r_ref.shape[0]
    C = r_ref.shape[1]
    W = MXU_DIM
    ng = r_ref.shape[2] // W

    @pl.when(c == 0)
    def _():
        h_scr[...] = jnp.zeros_like(h_scr)

    ti = lax.broadcasted_iota(jnp.int32, (C, C), 0)
    tj = lax.broadcasted_iota(jnp.int32, (C, C), 1)
    tri = (ti >= tj).astype(BF16)
    t_row = lax.broadcasted_iota(jnp.int32, (C, W), 0)
    j_col = lax.broadcasted_iota(jnp.int32, (C, W), 1) % C
    strict = j_col < t_row
    incl = j_col <= t_row
    eye_cat = (j_col == t_row).astype(F32)
    rb = lax.broadcasted_iota(jnp.int32, (HEADS_PER_GROUP * C, W), 0) // C
    cb = lax.broadcasted_iota(jnp.int32, (HEADS_PER_GROUP * C, W), 1) // RWKV_HEAD
    bmask = (rb == cb).astype(BF16)
    hmask = _head_mask(W, F32)

    chains = [(bi, gi) for bi in range(nb) for gi in range(ng)]
    each = lambda f, *lists: [f(*xs) for xs in zip(*lists)]
    load = lambda ref: [ref[bi, :, gi * W:(gi + 1) * W] for bi, gi in chains]
    r, dl, k, v, a, b = (load(ref) for ref in (r_ref, dl_ref, k_ref, v_ref, a_ref, b_ref))
    sbd = [h_scr[bi, gi] for bi, gi in chains]
    bd = lambda x: _blockdiag(x, bmask)

    def cumsum(d):
        d_hi = d.astype(BF16)
        d_r1 = d - d_hi.astype(F32)
        d_mid = d_r1.astype(BF16)
        d_lo = (d_r1 - d_mid.astype(F32)).astype(BF16)
        return _dot(tri, d_hi) + (_dot(tri, d_mid) + _dot(tri, d_lo))

    cum = each(cumsum, dl)
    cum_last = each(lambda x: x[C - 1:C, :], cum)
    e_neg = each(lambda x: jnp.exp(-x), cum)
    e_end = each(lambda x, xl: jnp.exp(xl - x), cum, cum_last)
    at = each(lambda x, cu, d: (x * jnp.exp(cu - d)).astype(BF16), a, cum, dl)
    rt = each(lambda x, cu: (x * jnp.exp(cu)).astype(BF16), r, cum)
    bt = each(lambda x, e: (x * e).astype(BF16), b, e_neg)
    kt = each(lambda x, e: (x * e).astype(BF16), k, e_neg)
    bh = each(lambda x, e: (x * e).astype(BF16), b, e_end)
    kh = each(lambda x, e: (x * e).astype(BF16), k, e_end)
    vb = each(lambda x: x.astype(BF16), v)
    ar = each(lambda x, y: jnp.concatenate([x, y], axis=0), at, rt)

    p_b = each(lambda x, y: _dot_nt(x, bd(y)), ar, bt)
    p_k = each(lambda x, y: _dot_nt(x, bd(y)), ar, kt)
    arh = each(lambda x, s: _dot_nt(x, s.astype(BF16)), ar, sbd)
    p_ab = each(lambda p: jnp.where(strict, p[:C], 0.0), p_b)
    p_rb = each(lambda p: jnp.where(incl, p[C:], 0.0).astype(BF16), p_b)
    p_akrk = each(lambda p: jnp.concatenate([jnp.where(strict, p[:C], 0.0), jnp.where(incl, p[C:], 0.0)],
                                            axis=0).astype(BF16), p_k)

    pv = each(lambda p, x: _dot(p, bd(x)), p_akrk, vb)
    n_sq = int(math.log2(C))
    nn_b = each(lambda p: p.astype(BF16), p_ab)
    nn_b = each(lambda n: _dot(n, bd(n)).astype(BF16), nn_b)
    w = each(lambda x, y: (x[:C] + y[:C]).astype(BF16), arh, pv)
    tm = each(lambda p: eye_cat + p, p_ab)
    for i in range(1, n_sq):
        last = i == n_sq - 1
        lhs = each(lambda t, n: t.astype(BF16) if last else jnp.concatenate([t.astype(BF16), n], axis=0), tm, nn_b)
        prod = each(lambda l, n: _dot(l, bd(n)), lhs, nn_b)
        tm = each(lambda t, p: t + p[:C], tm, prod)
        if not last:
            nn_b = each(lambda p: p[C:].astype(BF16), prod)
    ub = each(lambda t, x: _dot(t.astype(BF16), bd(x)).astype(BF16), tm, w)

    yv = each(lambda x, y, p, u: x[C:] + y[C:] + _dot(p, bd(u)), arh, pv, p_rb, ub)
    upd = each(lambda u, x, y, z: _dot_tn(jnp.concatenate([u, x], axis=0), jnp.concatenate([y, z], axis=0)),
               ub, vb, bh, kh)
    for (bi, gi), y, s, xl, up in zip(chains, yv, sbd, cum_last, upd):
        y_ref[bi, :, gi * W:(gi + 1) * W] = y
        h_scr[bi, gi] = (s * jnp.exp(xl) + up) * hmask

    @pl.when(c == pl.num_programs(0) - 1)
    def _():
        s_ref[...] = h_scr[...]


def _scan(r, dl, k, v, a, b):
    nb, t, dr = r.shape
    ng = dr // MXU_DIM
    C = SCAN_CHUNK
    spec = pl.BlockSpec((nb, C, dr), lambda c: (0, c, 0))
    return pl.pallas_call(
        _scan_kernel,
        grid=(t // C,),
        in_specs=[spec] * 6,
        out_specs=[spec, pl.BlockSpec((nb, ng, MXU_DIM, MXU_DIM), lambda c: (0, 0, 0, 0))],
        out_shape=[jax.ShapeDtypeStruct((nb, t, dr), F32),
                   jax.ShapeDtypeStruct((nb, ng, MXU_DIM, MXU_DIM), F32)],
        scratch_shapes=[pltpu.VMEM((nb, ng, MXU_DIM, MXU_DIM), F32)],
        compiler_params=_params("arbitrary"),
        name="wkv_scan",
    )(r, dl, k, v, a, b)


def _step_kernel(s_ref, r_ref, dl_ref, k_ref, v_ref, a_ref, b_ref, sout_ref, y_ref):
    S = s_ref[0]
    per_key = lambda ref: ref[...][None, :, :]
    sa = jnp.sum(S * per_key(a_ref), axis=1, keepdims=True)
    v = v_ref[...][:, None, :]
    s_new = S * jnp.exp(per_key(dl_ref)) + sa * per_key(b_ref) + v * per_key(k_ref)
    sout_ref[0] = s_new
    y_ref[...] = jnp.sum(s_new * per_key(r_ref), axis=1)


def _step(S0, r, dl, k, v, a, b):
    nb, nh, n, _ = S0.shape
    st = jnp.transpose(S0, (1, 2, 3, 0))
    sspec = pl.BlockSpec((1, n, n, nb), lambda h: (h, 0, 0, 0))
    vspec = pl.BlockSpec((n, nb), lambda h: (h, 0))
    s_new, y = pl.pallas_call(
        _step_kernel,
        grid=(nh,),
        in_specs=[sspec] + [vspec] * 6,
        out_specs=[sspec, vspec],
        out_shape=[jax.ShapeDtypeStruct(st.shape, F32), jax.ShapeDtypeStruct((nh * n, nb), F32)],
        compiler_params=_params("arbitrary"),
        name="wkv_step",
    )(st, *(x.T for x in (r, dl, k, v, a, b)))
    return jnp.transpose(s_new, (3, 0, 1, 2)), y.T


def _post_kernel(y_ref, bonus_ref, g_ref, gng_ref, gnb_ref, o_ref):
    y = y_ref[...]
    ones_bd = _head_mask(MXU_DIM, BF16)
    inv_n = 1.0 / RWKV_HEAD
    mu = _head_sum(y, ones_bd) * inv_n
    yc = y - mu
    var = _head_sum(yc * yc, ones_bd) * inv_n
    yn = yc * lax.rsqrt(var + GN_EPS) * gng_ref[...] + gnb_ref[...]
    o_ref[...] = (yn + bonus_ref[...]) * g_ref[...]


def _post(y, bonus, g, lw):
    n, dr = y.shape
    tm = _row_tile(n, 512)
    row = pl.BlockSpec((tm, dr), lambda i: (i, 0))
    return pl.pallas_call(
        _post_kernel,
        grid=(n // tm,),
        in_specs=[row, row, row, _const_spec(lw["gn_g"].shape), _const_spec(lw["gn_b"].shape)],
        out_specs=row,
        out_shape=jax.ShapeDtypeStruct((n, dr), F32),
        compiler_params=_params("arbitrary"),
        name="wkv_post",
    )(y, bonus, g, lw["gn_g"], lw["gn_b"])


def _conv_seq_kernel(h_ref, halo_ref, w_ref, b_ref, lng_ref, lnb_ref, o_ref, hp_scr, *, seq_tiles):
    tt = h_ref.shape[0]
    first = (pl.program_id(0) % seq_tiles) == 0
    hp_scr[0:CONV_HALO, :] = jnp.where(first, 0.0, halo_ref[...])
    hp_scr[CONV_HALO:, :] = h_ref[...]
    off = CONV_HALO - (CONV_WIDTH - 1)
    rc = CONV_ROW_CHUNK if tt % CONV_ROW_CHUNK == 0 else tt
    for c0 in range(0, o_ref.shape[1], LANES):
        lanes = slice(c0, c0 + LANES)
        for t0 in range(0, tt, rc):
            acc = jnp.broadcast_to(b_ref[:, lanes], (rc, LANES))
            for p in range(SUBLANES):
                rows = rc if p == 0 else rc + SUBLANES
                part = None
                for m in range((CONV_HALO + SUBLANES) // SUBLANES):
                    j = SUBLANES * m + p - off
                    if 0 <= j < CONV_WIDTH:
                        term = hp_scr[t0 + SUBLANES * m:t0 + SUBLANES * m + rows, lanes] * w_ref[j:j + 1, lanes]
                        part = term if part is None else part + term
                acc = acc + part[p:p + rc]
            o_ref[t0:t0 + rc, lanes] = acc
    z = _layer_norm(o_ref[...], lng_ref[...], lnb_ref[...])
    o_ref[...] = z * jax.nn.sigmoid(z)


def _conv_seq(h, seq_len, lw):
    n, dc = h.shape
    tt = _row_tile(seq_len, 256)
    assert tt % CONV_HALO == 0
    seq_tiles = seq_len // tt
    row = pl.BlockSpec((tt, dc), lambda i: (i, 0))
    halo = pl.BlockSpec((CONV_HALO, dc), lambda i: (jnp.maximum(i * (tt // CONV_HALO) - 1, 0), 0))
    consts = [lw["conv_w"], lw["conv_b"], lw["conv_ln_g"], lw["conv_ln_b"]]
    return pl.pallas_call(
        functools.partial(_conv_seq_kernel, seq_tiles=seq_tiles),
        grid=(n // tt,),
        in_specs=[row, halo] + [_const_spec(c.shape) for c in consts],
        out_specs=row,
        out_shape=jax.ShapeDtypeStruct((n, dc), F32),
        scratch_shapes=[pltpu.VMEM((CONV_HALO + tt, dc), F32)],
        compiler_params=_params("arbitrary"),
        name="conv_seq",
    )(h, h, *consts)


def _conv_step_kernel(buf_ref, h_ref, w_ref, b_ref, lng_ref, lnb_ref, o_ref, nbuf_ref):
    h = h_ref[...]
    acc = b_ref[...] + h * w_ref[CONV_WIDTH - 1:CONV_WIDTH, :]
    for j in range(CONV_WIDTH - 1):
        tap = buf_ref[j]
        acc = acc + tap * w_ref[j:j + 1, :]
        if j > 0:
            nbuf_ref[j - 1] = tap
    nbuf_ref[CONV_WIDTH - 2] = h
    z = _layer_norm(acc, lng_ref[...], lnb_ref[...])
    o_ref[...] = z * jax.nn.sigmoid(z)


def _conv_step(buf, h, lw):
    nb, wm1, dc = buf.shape
    bb = _row_tile(nb, 32)
    bspec = pl.BlockSpec((wm1, bb, dc), lambda i: (0, i, 0))
    row = pl.BlockSpec((bb, dc), lambda i: (i, 0))
    consts = [lw["conv_w"], lw["conv_b"], lw["conv_ln_g"], lw["conv_ln_b"]]
    out, nbuf = pl.pallas_call(
        _conv_step_kernel,
        grid=(nb // bb,),
        in_specs=[bspec, row] + [_const_spec(c.shape) for c in consts],
        out_specs=[row, bspec],
        out_shape=[jax.ShapeDtypeStruct((nb, dc), F32), jax.ShapeDtypeStruct((wm1, nb, dc), F32)],
        compiler_params=_params("arbitrary"),
        name="conv_step",
    )(jnp.transpose(buf, (1, 0, 2)), h, *consts)
    return out, jnp.transpose(nbuf, (1, 0, 2))


def _outproj_kernel(*refs, alpha, n_dst):
    x_ref, yr_ref, yc_ref, wo_ref, g_ref, b_ref, wr_ref, br_ref, h_ref, idx_ref, gates_ref = refs[n_dst:]
    ymix = jnp.concatenate([yr_ref[...].astype(BF16), yc_ref[...].astype(BF16)], axis=1)
    h = _layer_norm(alpha * x_ref[...] + _dot(ymix, wo_ref[...]), g_ref[...], b_ref[...])
    h_ref[...] = h
    logits = _dot(h.astype(BF16), wr_ref[...]) + br_ref[...]

    ne = logits.shape[1]
    lane = lax.broadcasted_iota(jnp.int32, logits.shape, 1)
    slot = lax.broadcasted_iota(jnp.int32, idx_ref.shape, 1)
    vals = logits
    top_v, idx = [], jnp.zeros(idx_ref.shape, jnp.int32)
    for k in range(TOP_K):
        m = jnp.max(vals, axis=-1, keepdims=True)
        e = jnp.min(jnp.where(vals == m, lane, ne), axis=-1, keepdims=True)
        top_v.append(m)
        idx = jnp.where(slot == k, e, idx)
        vals = jnp.where(lane == e, -jnp.inf, vals)
    expv = [jnp.exp(v - top_v[0]) for v in top_v]
    den = sum(expv[1:], expv[0])
    gates = jnp.zeros(gates_ref.shape, F32)
    for k in range(TOP_K):
        gates = jnp.where(slot == k, expv[k] / den, gates)
    idx_ref[...] = idx
    gates_ref[...] = gates


def _outproj(x, y_rwkv, y_conv, lw, alpha, n_total, row0, dst=()):
    n, d = x.shape
    dr = y_rwkv.shape[1]
    dc = y_conv.shape[1]
    tm = _row_tile(math.gcd(n, row0) if row0 else n, 256)
    blk0 = row0 // tm
    row = lambda c: pl.BlockSpec((tm, c), lambda i: (i, 0))
    out_row = lambda c: pl.BlockSpec((tm, c), lambda i: (blk0 + i, 0))
    consts = [lw["w_out"], lw["ln1_g"], lw["ln1_b"], lw["w_router"], lw["b_router"]]
    return pl.pallas_call(
        functools.partial(_outproj_kernel, alpha=alpha, n_dst=len(dst)),
        grid=(n // tm,),
        in_specs=[pl.BlockSpec(memory_space=pl.ANY)] * len(dst) + [row(d), row(dr), row(dc)]
        + [_const_spec(c.shape) for c in consts],
        out_specs=[out_row(d), out_row(TOP_K), out_row(TOP_K)],
        out_shape=[jax.ShapeDtypeStruct((n_total, d), F32), jax.ShapeDtypeStruct((n_total, TOP_K), jnp.int32),
                   jax.ShapeDtypeStruct((n_total, TOP_K), F32)],
        input_output_aliases={i: i for i in range(len(dst))},
        compiler_params=_params("arbitrary"),
        name="outproj",
    )(*dst, x, y_rwkv, y_conv, *consts)


def _route(top_idx, ne, moe_tm):
    n_tok = top_idx.shape[0]
    n_assign = n_tok * TOP_K
    e_flat = top_idx.reshape(n_assign).astype(jnp.int32)
    onehot = (e_flat[:, None] == jnp.arange(ne, dtype=jnp.int32)[None, :]).astype(jnp.int32)
    counts = onehot.sum(0)
    rank = jnp.take_along_axis(jnp.cumsum(onehot, axis=0), e_flat[:, None], axis=1)[:, 0] - 1
    nblk_e = (counts + moe_tm - 1) // moe_tm
    blk_end = jnp.cumsum(nblk_e)
    blk_start = blk_end - nblk_e
    n_active = blk_end[-1]
    nb_max = n_assign // moe_tm + ne
    pos = blk_start[e_flat] * moe_tm + rank
    sorted_tok = (jnp.argsort(e_flat, stable=True) // TOP_K).astype(jnp.int32)
    sorted_tok = jnp.concatenate([sorted_tok, jnp.zeros((GATHER_GROUP,), jnp.int32)])
    cstart = jnp.cumsum(counts) - counts
    q = jnp.arange(nb_max, dtype=jnp.int32)
    q_eff = jnp.minimum(q, n_active - 1)
    blk_e = jnp.minimum(jnp.searchsorted(blk_end, q_eff, side="right"), ne - 1).astype(jnp.int32)
    row_in_e = (q_eff - blk_start[blk_e]) * moe_tm
    valid = jnp.clip(counts[blk_e] - row_in_e, 0, moe_tm)
    valid = jnp.where(q < n_active, valid, 0).astype(jnp.int32)
    src_start = (cstart[blk_e] + row_in_e).astype(jnp.int32)
    return pos.astype(jnp.int32), sorted_tok, src_start, blk_e, q_eff.astype(jnp.int32), valid


def _gather_kernel(qeff_ref, valid_ref, src_ref, tok_ref, h_hbm, o_ref, buf, sem):
    s = pl.program_id(0)
    nb = pl.num_programs(0) - 1
    per = GATHER_GROUP // SUBLANES
    groups = lambda nvalid: (nvalid + GATHER_GROUP - 1) // GATHER_GROUP

    @pl.when(s < nb)
    def _():
        slot = s % 2
        base = src_ref[s]

        def issue(g, c):
            for i in range(GATHER_GROUP):
                tok = tok_ref[base + g * GATHER_GROUP + i]
                pltpu.make_async_copy(h_hbm.at[pl.ds(tok, 1)],
                                      buf.at[slot, g * per + i // SUBLANES, pl.ds(i % SUBLANES, 1)],
                                      sem.at[slot]).start()
            return c

        lax.fori_loop(0, groups(valid_ref[s]), issue, 0)

    @pl.when(s > 0)
    def _():
        slot = (s - 1) % 2
        nvalid = valid_ref[s - 1]

        @pl.when(nvalid > 0)
        def _():
            def wait(g, c):
                pltpu.make_async_copy(h_hbm.at[pl.ds(0, SUBLANES)], buf.at[slot, g], sem.at[slot]).wait()
                return c

            lax.fori_loop(0, groups(nvalid) * per, wait, 0)
            rows = lax.broadcasted_iota(jnp.int32, o_ref.shape, 0)
            o_ref[...] = jnp.where(rows < nvalid, buf[slot].reshape(o_ref.shape), 0.0).astype(o_ref.dtype)


def _gather_rows(h, sorted_tok, src_start, q_eff, valid, moe_tm):
    n_tok, d = h.shape
    nb_max = q_eff.shape[0]
    grid_spec = pltpu.PrefetchScalarGridSpec(
        num_scalar_prefetch=4,
        grid=(nb_max + 1,),
        in_specs=[pl.BlockSpec(memory_space=pl.ANY)],
        out_specs=pl.BlockSpec((moe_tm, d), lambda s, qe, va, sr, tk: (qe[jnp.maximum(s - 1, 0)], 0)),
        scratch_shapes=[pltpu.VMEM((2, moe_tm // SUBLANES, SUBLANES, d), F32), pltpu.SemaphoreType.DMA((2,))],
    )
    return pl.pallas_call(
        _gather_kernel,
        grid_spec=grid_spec,
        out_shape=jax.ShapeDtypeStruct((nb_max * moe_tm, d), BF16),
        compiler_params=_params("arbitrary"),
        name="moe_gather",
    )(q_eff, valid, src_start, sorted_tok, h)


def _moe_kernel(be_ref, qeff_ref, valid_ref, x_ref, wg_ref, wl_ref, bg_ref, bl_ref, wd_ref, bd_ref, o_ref,
                wg_s, wl_s, wd_s, *, sub):
    q = pl.program_id(0)
    j = pl.program_id(1)
    nvalid = valid_ref[q]
    unit = sub // 2
    n_units = (nvalid + unit - 1) // unit
    n_all = o_ref.shape[0] // unit

    def fill(s, value):
        r0 = pl.multiple_of(s * unit, unit)
        o_ref[pl.ds(r0, unit), :] = jnp.broadcast_to(value, (unit, o_ref.shape[1]))

    def rows_block(r0, rows):
        x = x_ref[pl.ds(r0, rows), :]
        g = jnp.minimum(_dot(x, wg_s[...]) + bg_ref[0], SWIGLU_LIMIT)
        l = jnp.clip(_dot(x, wl_s[...]) + bl_ref[0], -SWIGLU_LIMIT, SWIGLU_LIMIT)
        act = g * jax.nn.sigmoid(SWIGLU_ALPHA * g) * (l + 1.0)
        o_ref[pl.ds(r0, rows), :] += _dot(act.astype(BF16), wd_s[...])

    @pl.when(nvalid > 0)
    def _():
        @pl.when(j == 0)
        def _():
            lax.fori_loop(0, n_units, lambda s, c: (fill(s, bd_ref[0]), c)[1], 0)
            lax.fori_loop(n_units, n_all, lambda s, c: (fill(s, jnp.zeros((1, 1), F32)), c)[1], 0)

        def cast_weights():
            wg_s[...] = wg_ref[0].astype(BF16)
            wl_s[...] = wl_ref[0].astype(BF16)
            wd_s[...] = wd_ref[0].astype(BF16)

        def pair(i, c):
            r0 = pl.multiple_of(i * 2 * sub, unit)
            rows_block(r0, sub)
            rows_block(r0 + sub, sub)
            return c

        big = n_units >= 4

        @pl.when(big)
        def _():
            cast_weights()
            pair(0, 0)

        @pl.when(jnp.logical_not(big))
        def _():
            cast_weights()
            rows_block(0, unit)

        first = jnp.where(big, 4, 1)
        rest = n_units - first
        n_pairs = rest // 4
        tail = rest - n_pairs * 4
        lax.fori_loop(0, n_pairs, lambda i, c: pair(i + 1, c), 0)
        t0 = (first + n_pairs * 4) * unit
        for units in (2, 1):
            @pl.when((tail & units) != 0)
            def _():
                rows_block(pl.multiple_of(t0 + (tail & ~(2 * units - 1)) * unit, unit), units * unit)


def _moe_experts(x_sorted, blk_e, q_eff, valid, lw, moe_tm, sub, tf):
    n_rows, d = x_sorted.shape
    ne, _, f2 = lw["w_gu"].shape
    f = f2 // 2
    nf = f // tf
    nb_max = q_eff.shape[0]
    b_gu = lw["b_gu"].reshape(ne, 1, f2)
    b_down = lw["b_down"].reshape(ne, 1, d)

    def jf(q, j, va):
        return jnp.where(va[q] > 0, j, nf - 1)

    grid_spec = pltpu.PrefetchScalarGridSpec(
        num_scalar_prefetch=3,
        grid=(nb_max, nf),
        in_specs=[
            pl.BlockSpec((moe_tm, d), lambda q, j, be, qe, va: (qe[q], 0)),
            pl.BlockSpec((1, d, tf), lambda q, j, be, qe, va: (be[q], 0, jf(q, j, va))),
            pl.BlockSpec((1, d, tf), lambda q, j, be, qe, va: (be[q], 0, nf + jf(q, j, va))),
            pl.BlockSpec((1, 1, tf), lambda q, j, be, qe, va: (be[q], 0, jf(q, j, va))),
            pl.BlockSpec((1, 1, tf), lambda q, j, be, qe, va: (be[q], 0, nf + jf(q, j, va))),
            pl.BlockSpec((1, tf, d), lambda q, j, be, qe, va: (be[q], jf(q, j, va), 0)),
            pl.BlockSpec((1, 1, d), lambda q, j, be, qe, va: (be[q], 0, 0)),
        ],
        out_specs=pl.BlockSpec((moe_tm, d), lambda q, j, be, qe, va: (qe[q], 0)),
        scratch_shapes=[pltpu.VMEM((d, tf), BF16), pltpu.VMEM((d, tf), BF16), pltpu.VMEM((tf, d), BF16)],
    )
    return pl.pallas_call(
        functools.partial(_moe_kernel, sub=sub),
        grid_spec=grid_spec,
        out_shape=jax.ShapeDtypeStruct((n_rows, d), F32),
        compiler_params=_params("arbitrary", "arbitrary", vmem_limit_bytes=MOE_VMEM_LIMIT_BYTES),
        name="moe_experts",
    )(blk_e, q_eff, valid, x_sorted, lw["w_gu"], lw["w_gu"], b_gu, b_gu, lw["w_down"], b_down)


def _combine_kernel(pos_ref, h_ref, gates_ref, yrows_hbm, g_ref, b_ref, op_ref, os_ref, *scratch,
                    alpha, n_first, n_tiles):
    i = pl.program_id(0)
    tm = h_ref.shape[0]
    n_slots = COMBINE_LAG + 1
    slots = tuple(zip(scratch[:n_slots], scratch[n_slots:]))

    def fetch(tile, buf, sem):
        for r in range(tm):
            for k in range(TOP_K):
                p = pos_ref[(tile * tm + r) * TOP_K + k]
                pltpu.make_async_copy(yrows_hbm.at[pl.ds(p, 1)],
                                      buf.at[k, r // SUBLANES, pl.ds(r % SUBLANES, 1)], sem).start()

    def wait(buf, sem):
        for k in range(TOP_K):
            pltpu.make_async_copy(buf.at[k], buf.at[k], sem).wait()

    def finish(tile, buf):
        gates = gates_ref[...]
        slot = lambda k: buf[k].reshape(tm, buf.shape[-1])
        ffn = slot(0) * gates[:, 0:1]
        for k in range(1, TOP_K):
            ffn = ffn + slot(k) * gates[:, k:k + 1]
        y = _layer_norm(alpha * h_ref[...] + ffn, g_ref[...], b_ref[...])

        @pl.when(tile < n_first)
        def _():
            op_ref[...] = y

        @pl.when(tile >= n_first)
        def _():
            os_ref[...] = y

    for step in range(min(COMBINE_LAG, n_tiles)):
        @pl.when(i == step)
        def _():
            fetch(step, *slots[step % n_slots])

    for phase in range(n_slots):
        @pl.when((i >= COMBINE_LAG) & (i < n_tiles) & (i % n_slots == phase))
        def _():
            done = slots[(phase - COMBINE_LAG) % n_slots]
            wait(*done)
            fetch(i, *slots[phase])
            finish(i - COMBINE_LAG, done[0])

    for step in range(max(n_tiles, COMBINE_LAG), n_tiles + COMBINE_LAG):
        @pl.when(i == step)
        def _():
            done = slots[(step - COMBINE_LAG) % n_slots]
            wait(*done)
            finish(step - COMBINE_LAG, done[0])


def _combine(h, gates, pos, y_rows, lw, alpha, n_prompt):
    n_tok, d = h.shape
    tm = _row_tile(math.gcd(n_prompt, n_tok - n_prompt), 128)
    n_first = n_prompt // tm
    n_tiles = n_tok // tm
    done = lambda i: jnp.maximum(i - COMBINE_LAG, 0)
    n_slots = COMBINE_LAG + 1
    buf = pltpu.VMEM((TOP_K, tm // SUBLANES, SUBLANES, d), F32)
    grid_spec = pltpu.PrefetchScalarGridSpec(
        num_scalar_prefetch=1,
        grid=(n_tiles + COMBINE_LAG,),
        in_specs=[
            pl.BlockSpec((tm, d), lambda i, p: (done(i), 0)),
            pl.BlockSpec((tm, TOP_K), lambda i, p: (done(i), 0)),
            pl.BlockSpec(memory_space=pl.ANY),
            pl.BlockSpec((1, d), lambda i, p: (0, 0)),
            pl.BlockSpec((1, d), lambda i, p: (0, 0)),
        ],
        out_specs=[pl.BlockSpec((tm, d), lambda i, p: (jnp.minimum(done(i), n_first - 1), 0)),
                   pl.BlockSpec((tm, d), lambda i, p: (jnp.maximum(done(i) - n_first, 0), 0))],
        scratch_shapes=[buf] * n_slots + [pltpu.SemaphoreType.DMA(())] * n_slots,
    )
    return pl.pallas_call(
        functools.partial(_combine_kernel, alpha=alpha, n_first=n_first, n_tiles=n_tiles),
        grid_spec=grid_spec,
        out_shape=[jax.ShapeDtypeStruct((n_prompt, d), F32), jax.ShapeDtypeStruct((n_tok - n_prompt, d), F32)],
        compiler_params=_params("arbitrary"),
        name="moe_combine",
    )(pos, h, gates, y_rows, lw["ln2_g"], lw["ln2_b"])


def _moe_tiles(n_assign, ne, d, f):
    unit = MOE_SUB // 2
    mean = -(-n_assign // ne)
    tm = max(MOE_SUB, -(-(mean * 11 // 10) // unit) * unit)
    for tf in (512, 256):
        tf = min(tf, f)
        blocks = 2 * tm * d * 2 + 2 * tm * d * 4
        weights = 3 * d * tf * (2 * 4 + 2)
        temps = MOE_SUB * 2 * tf * 4
        if blocks + weights + temps <= MOE_VMEM_LIMIT_BYTES:
            return tm, tf
    return min(tm, 4 * MOE_SUB), min(256, f)


_VEC_PARAMS = ("mu_rkv", "w0", "a0", "k_k", "k_a", "r_k", "gn_g", "gn_b", "conv_b", "conv_ln_g", "conv_ln_b",
               "ln1_g", "ln1_b", "b_router", "ln2_g", "ln2_b")
_BF16_PARAMS = ("w_in", "w_A", "w_B", "a_A", "a_B", "g_A", "g_B", "w_out", "w_router")


def _diag_blocks(s):
    nb, ng = s.shape[:2]
    s6 = s.reshape(nb, ng, HEADS_PER_GROUP, RWKV_HEAD, HEADS_PER_GROUP, RWKV_HEAD)
    d = jnp.stack([s6[:, :, h, :, h, :] for h in range(HEADS_PER_GROUP)], axis=2)
    return d.reshape(nb, ng * HEADS_PER_GROUP, RWKV_HEAD, RWKV_HEAD)


def _layer(xp, xs, sx, srkv, swkv, sconv, lw, alpha):
    nbp, t, d = xp.shape
    nbs = xs.shape[0]
    xp2 = xp.reshape(nbp * t, d)
    xs2 = xs.reshape(nbs, d)

    up_last, hgp, dlp, gp, rp, kp, vp, ap, bp, bonp = _inproj(xp2, None, None, t, lw)
    us, hgs, dls, gs, rs, ks, vs, as_, bs, bons = _inproj(xs2, sx, srkv, 1, lw)
    dr = dlp.shape[1]

    seq = lambda z: z.reshape(nbp, t, dr)
    yp_raw, s_end = _scan(seq(rp), seq(dlp), seq(kp), seq(vp), seq(ap), seq(bp))
    s_new, ys_raw = _step(swkv, rs, dls, ks, vs, as_, bs)
    yrp = _post(yp_raw.reshape(nbp * t, dr), bonp, gp, lw)
    yrs = _post(ys_raw, bons, gs, lw)

    ycp = _conv_seq(hgp, t, lw)
    ycs, nbuf = _conv_step(sconv, hgs, lw)

    n_prompt = nbp * t
    n_tok = n_prompt + nbs
    dst = _outproj(xp2, yrp, ycp, lw, alpha, n_tok, 0)
    h_all, top_idx, gates = _outproj(xs2, yrs, ycs, lw, alpha, n_tok, n_prompt, dst=tuple(dst))

    ne = lw["w_router"].shape[1]
    moe_tm, moe_tf = _moe_tiles(n_tok * TOP_K, ne, d, lw["w_down"].shape[1])
    pos, sorted_tok, src_start, blk_e, q_eff, valid = _route(top_idx, ne, moe_tm)
    x_sorted = _gather_rows(h_all, sorted_tok, src_start, q_eff, valid, moe_tm)
    y_rows = _moe_experts(x_sorted, blk_e, q_eff, valid, lw, moe_tm, MOE_SUB, moe_tf)
    yp, ys = _combine(h_all, gates, pos, y_rows, lw, alpha, n_prompt)
    yp = yp.reshape(nbp, t, d)
    ys = ys.reshape(nbs, 1, d)
    p_state = (xp[:, -1], up_last.reshape(nbp, -1, SUBLANES, 3 * dr)[:, -1, -1], _diag_blocks(s_end),
               hgp.reshape(nbp, t, -1)[:, t - (CONV_WIDTH - 1):])
    s_state = (xs2, us, s_new, nbuf)
    return yp, ys, p_state, s_state


def kernel(x_prompt, x_sample, state_shift_x, state_shift_rkv, state_wkv, state_conv, w_in, mu_x, mu_rkv, w0, w_A,
           w_B, a0, a_A, a_B, g_A, g_B, k_k, k_a, r_k, gn_g, gn_b, conv_w, conv_b, conv_ln_g, conv_ln_b, w_out,
           ln1_g, ln1_b, w_router, b_router, w_gu, b_gu, w_down, b_down, ln2_g, ln2_b):
    params = dict(w_in=w_in, mu_x=mu_x, mu_rkv=mu_rkv, w0=w0, w_A=w_A, w_B=w_B, a0=a0, a_A=a_A, a_B=a_B, g_A=g_A,
                  g_B=g_B, k_k=k_k, k_a=k_a, r_k=r_k, gn_g=gn_g, gn_b=gn_b, conv_w=conv_w, conv_b=conv_b,
                  conv_ln_g=conv_ln_g, conv_ln_b=conv_ln_b, w_out=w_out, ln1_g=ln1_g, ln1_b=ln1_b,
                  w_router=w_router, b_router=b_router, w_gu=w_gu, b_gu=b_gu, w_down=w_down, b_down=b_down,
                  ln2_g=ln2_g, ln2_b=ln2_b)
    depth = w_in.shape[0]
    assert x_sample.shape[1] == 1, "the sample group advances one token per step"
    alpha = (2.0 * depth) ** 0.25
    xp, xs = x_prompt, x_sample
    p_states, s_states = [], []
    for l in range(depth):
        lw = {name: p[l] for name, p in params.items()}
        for name in _VEC_PARAMS:
            lw[name] = lw[name].reshape(1, -1)
        for name in _BF16_PARAMS:
            lw[name] = lw[name].astype(BF16)
        xp, xs, p_st, s_st = _layer(xp, xs, state_shift_x[l], state_shift_rkv[l], state_wkv[l], state_conv[l],
                                    lw, alpha)
        p_states.append(p_st)
        s_states.append(s_st)
    stack = lambda states, i: jnp.stack([st[i] for st in states])
    return (xp, xs,
            stack(p_states, 0), stack(p_states, 1), stack(p_states, 2), stack(p_states, 3),
            stack(s_states, 0), stack(s_states, 1), stack(s_states, 2), stack(s_states, 3))
```

```python
import functools
import math

import jax
import jax.numpy as jnp
from jax import lax
from jax.experimental import pallas as pl
from jax.experimental.pallas import tpu as pltpu

F32 = jnp.float32
BF16 = jnp.bfloat16

RWKV_HEAD = 64
CONV_WIDTH = 31
TOP_K = 4
SWIGLU_LIMIT = 7.0
SWIGLU_ALPHA = 1.702
LN_EPS = 1e-5
GN_EPS = 64e-5

LANES = 128
SUBLANES = 8
MXU_DIM = 256
VMEM_BYTES = 64 * 1024 * 1024
VMEM_LIMIT_BYTES = 56 * 1024 * 1024
MOE_VMEM_LIMIT_BYTES = VMEM_BYTES - 2 * 1024 * 1024

SCAN_CHUNK = 64
HEADS_PER_GROUP = MXU_DIM // RWKV_HEAD
CONV_HALO = 32
CONV_ROW_CHUNK = 128
GATHER_GROUP = 32
COMBINE_LAG = 2
MOE_SUB = 256


def _row_tile(n, target):
    best = None
    for t in range(SUBLANES, min(n, target) + 1, SUBLANES):
        if n % t == 0:
            best = t
    assert best is not None, (n, target)
    return best


def _params(*sem, vmem_limit_bytes=VMEM_LIMIT_BYTES):
    return pltpu.CompilerParams(dimension_semantics=sem, vmem_limit_bytes=vmem_limit_bytes)


def _const_spec(shape):
    nd = len(shape)
    return pl.BlockSpec(shape, lambda *_: (0,) * nd, pipeline_mode=pl.Buffered(1))


def _dot(a, b):
    return jnp.dot(a, b, preferred_element_type=F32)


def _split2(x):
    hi = x.astype(BF16)
    lo = (x - hi.astype(F32)).astype(BF16)
    return hi, lo


def _layer_norm(z, g, b):
    mu = jnp.mean(z, axis=-1, keepdims=True)
    zc = z - mu
    var = jnp.mean(zc * zc, axis=-1, keepdims=True)
    return zc * lax.rsqrt(var + LN_EPS) * g + b


def _shift_rows(x, first_row):
    xs = pltpu.roll(x, 1, axis=0)
    row = lax.broadcasted_iota(jnp.int32, x.shape, 0)
    return jnp.where(row == 0, first_row, xs)


def _head_mask(n, dtype):
    r = lax.broadcasted_iota(jnp.int32, (n, n), 0) // RWKV_HEAD
    c = lax.broadcasted_iota(jnp.int32, (n, n), 1) // RWKV_HEAD
    return (r == c).astype(dtype)


def _head_sum(x, ones_bd):
    cols = []
    for c0 in range(0, x.shape[1], MXU_DIM):
        hi, lo = _split2(x[:, c0:c0 + MXU_DIM])
        cols.append(_dot(hi, ones_bd) + _dot(lo, ones_bd))
    return cols[0] if len(cols) == 1 else jnp.concatenate(cols, axis=1)


def _inproj_kernel(x_ref, xprev_ref, uprev_ref, w_in_ref, mu_ref, wA_ref, aA_ref, gA_ref, wB_ref, aB_ref, gB_ref,
                   w0_ref, a0_ref, murkv_ref, kk_ref, ka_ref, rk_ref,
                   ulast_ref, hglu_ref, dlog_ref, g_ref, r_ref, k_ref, v_ref, a_ref, b_ref, bonus_ref,
                   ucarry_scr, *, seq_tiles, halo):
    if halo:
        @pl.when(pl.program_id(0) == 0)
        def _():
            ucarry_scr[...] = jnp.zeros_like(ucarry_scr)

    x = x_ref[...]
    tm = x.shape[0]
    first = (pl.program_id(0) % seq_tiles) == 0
    if halo:
        prev_row = jnp.where(first, 0.0, xprev_ref[SUBLANES - 1:SUBLANES, :])
        xprev = _shift_rows(x, prev_row)
    else:
        xprev = xprev_ref[...]
    xx = xprev - x
    xb = x.astype(BF16)
    d3 = murkv_ref.shape[1]
    dc = hglu_ref.shape[1]
    dr = r_ref.shape[1]
    ones_bd = _head_mask(MXU_DIM, BF16)

    tw1 = _dot((x + xx * mu_ref[0:1, :]).astype(BF16), wA_ref[...])
    ta1 = _dot((x + xx * mu_ref[1:2, :]).astype(BF16), aA_ref[...])
    tg1 = _dot((x + xx * mu_ref[2:3, :]).astype(BF16), gA_ref[...])
    u = _dot(xb, w_in_ref[:, :d3])

    wlin = w0_ref[...] + _dot(jnp.tanh(tw1).astype(BF16), wB_ref[...])
    asig = jax.nn.sigmoid(a0_ref[...] + _dot(ta1.astype(BF16), aB_ref[...]))
    g_ref[...] = _dot(jax.nn.sigmoid(tg1).astype(BF16), gB_ref[...])
    wlog = -jax.nn.softplus(-wlin) - 0.5
    dlog_ref[...] = -jnp.exp(wlog)

    if halo:
        uprev = _shift_rows(u, jnp.where(first, 0.0, ucarry_scr[0:1, :]))
        ucarry_scr[0:1, :] = u[tm - 1:tm, :]
        ulast_ref[...] = u[tm - SUBLANES:tm, :]
    else:
        uprev = uprev_ref[...]
        ulast_ref[...] = u
    rkv = u + (uprev - u) * murkv_ref[...]
    r = rkv[:, :dr]
    k = rkv[:, dr:2 * dr]
    v = rkv[:, 2 * dr:]
    kk = k * kk_ref[...]
    kk_norm2 = _head_sum(kk * kk, ones_bd)

    val = _dot(xb, w_in_ref[:, d3:d3 + dc])
    gate = _dot(xb, w_in_ref[:, d3 + dc:])
    hglu_ref[...] = val * jax.nn.sigmoid(gate)

    kk = kk * lax.rsqrt(jnp.maximum(kk_norm2, 1e-24))
    k = k * (1.0 + (asig - 1.0) * ka_ref[...])
    r_ref[...] = r
    k_ref[...] = k
    v_ref[...] = v
    a_ref[...] = -kk
    b_ref[...] = kk * asig
    bonus_ref[...] = _head_sum(r * k * rk_ref[...], ones_bd) * v


def _inproj(x, xprev, uprev, seq_len, lw):
    n, d = x.shape
    d_in = lw["w_in"].shape[1]
    dr = lw["w_B"].shape[1]
    d3 = 3 * dr
    dc = (d_in - d3) // 2
    halo = xprev is None
    tm = _row_tile(seq_len if halo else n, 256)
    seq_tiles = (seq_len // tm) if halo else 1
    row = lambda c: pl.BlockSpec((tm, c), lambda i: (i, 0))
    if halo:
        prev_specs = [pl.BlockSpec((SUBLANES, d), lambda i: (jnp.maximum(i * (tm // SUBLANES) - 1, 0), 0)),
                      pl.BlockSpec((SUBLANES, d3), lambda i: (0, 0))]
        prev_args = [x, jnp.zeros((SUBLANES, d3), F32)]
        ulast_spec = pl.BlockSpec((SUBLANES, d3), lambda i: (i, 0))
        ulast_shape = jax.ShapeDtypeStruct((n // tm * SUBLANES, d3), F32)
    else:
        prev_specs = [row(d), row(d3)]
        prev_args = [xprev, uprev]
        ulast_spec = row(d3)
        ulast_shape = jax.ShapeDtypeStruct((n, d3), F32)
    consts = [lw["w_in"], lw["mu_x"], lw["w_A"], lw["a_A"], lw["g_A"], lw["w_B"], lw["a_B"], lw["g_B"],
              lw["w0"], lw["a0"], lw["mu_rkv"], lw["k_k"], lw["k_a"], lw["r_k"]]
    widths = (dc, dr, dr) + (dr,) * 6
    return pl.pallas_call(
        functools.partial(_inproj_kernel, seq_tiles=seq_tiles, halo=halo),
        grid=(n // tm,),
        in_specs=[row(d)] + prev_specs + [_const_spec(c.shape) for c in consts],
        out_specs=[ulast_spec] + [row(c) for c in widths],
        out_shape=[ulast_shape] + [jax.ShapeDtypeStruct((n, c), F32) for c in widths],
        scratch_shapes=[pltpu.VMEM((SUBLANES, d3), F32)],
        compiler_params=_params("arbitrary"),
        name="inproj",
    )(x, *prev_args, *consts)


def _blockdiag(x, mask):
    reps = mask.shape[0] // x.shape[0]
    return jnp.concatenate([x] * reps, axis=0) * mask


def _dot_nt(a, b):
    return lax.dot_general(a, b, (((1,), (1,)), ((), ())), preferred_element_type=F32)


def _dot_tn(a, b):
    return lax.dot_general(a, b, (((0,), (0,)), ((), ())), preferred_element_type=F32)


def _scan_kernel(r_ref, dl_ref, k_ref, v_ref, a_ref, b_ref, y_ref, s_ref, h_scr):
    c = pl.program_id(0)
    nb = r_ref.shape[0]
    C = r_ref.shape[1]
    W = MXU_DIM
    ng = r_ref.shape[2] // W

    @pl.when(c == 0)
    def _():
        h_scr[...] = jnp.zeros_like(h_scr)

    ti = lax.broadcasted_iota(jnp.int32, (C, C), 0)
    tj = lax.broadcasted_iota(jnp.int32, (C, C), 1)
    tri = (ti >= tj).astype(BF16)
    t_row = lax.broadcasted_iota(jnp.int32, (C, W), 0)
    j_col = lax.broadcasted_iota(jnp.int32, (C, W), 1) % C
    strict = j_col < t_row
    incl = j_col <= t_row
    eye_cat = (j_col == t_row).astype(F32)
    rb = lax.broadcasted_iota(jnp.int32, (HEADS_PER_GROUP * C, W), 0) // C
    cb = lax.broadcasted_iota(jnp.int32, (HEADS_PER_GROUP * C, W), 1) // RWKV_HEAD
    bmask = (rb == cb).astype(BF16)
    hmask = _head_mask(W, F32)

    chains = [(bi, gi) for bi in range(nb) for gi in range(ng)]
    each = lambda f, *lists: [f(*xs) for xs in zip(*lists)]
    load = lambda ref: [ref[bi, :, gi * W:(gi + 1) * W] for bi, gi in chains]
    r, dl, k, v, a, b = (load(ref) for ref in (r_ref, dl_ref, k_ref, v_ref, a_ref, b_ref))
    sbd = [h_scr[bi, gi] for bi, gi in chains]
    bd = lambda x: _blockdiag(x, bmask)

    def cumsum(d):
        d_hi = d.astype(BF16)
        d_r1 = d - d_hi.astype(F32)
        d_mid = d_r1.astype(BF16)
        d_lo = (d_r1 - d_mid.astype(F32)).astype(BF16)
        return _dot(tri, d_hi) + (_dot(tri, d_mid) + _dot(tri, d_lo))

    cum = each(cumsum, dl)
    cum_last = each(lambda x: x[C - 1:C, :], cum)
    e_neg = each(lambda x: jnp.exp(-x), cum)
    e_end = each(lambda x, xl: jnp.exp(xl - x), cum, cum_last)
    at = each(lambda x, cu, d: (x * jnp.exp(cu - d)).astype(BF16), a, cum, dl)
    rt = each(lambda x, cu: (x * jnp.exp(cu)).astype(BF16), r, cum)
    bt = each(lambda x, e: (x * e).astype(BF16), b, e_neg)
    kt = each(lambda x, e: (x * e).astype(BF16), k, e_neg)
    bh = each(lambda x, e: (x * e).astype(BF16), b, e_end)
    kh = each(lambda x, e: (x * e).astype(BF16), k, e_end)
    vb = each(lambda x: x.astype(BF16), v)
    ar = each(lambda x, y: jnp.concatenate([x, y], axis=0), at, rt)

    p_b = each(lambda x, y: _dot_nt(x, bd(y)), ar, bt)
    p_k = each(lambda x, y: _dot_nt(x, bd(y)), ar, kt)
    arh = each(lambda x, s: _dot_nt(x, s.astype(BF16)), ar, sbd)
    p_ab = each(lambda p: jnp.where(strict, p[:C], 0.0), p_b)
    p_rb = each(lambda p: jnp.where(incl, p[C:], 0.0).astype(BF16), p_b)
    p_akrk = each(lambda p: jnp.concatenate([jnp.where(strict, p[:C], 0.0), jnp.where(incl, p[C:], 0.0)],
                                            axis=0).astype(BF16), p_k)

    pv = each(lambda p, x: _dot(p, bd(x)), p_akrk, vb)
    n_sq = int(math.log2(C))
    nn_b = each(lambda p: p.astype(BF16), p_ab)
    nn_b = each(lambda n: _dot(n, bd(n)).astype(BF16), nn_b)
    w = each(lambda x, y: (x[:C] + y[:C]).astype(BF16), arh, pv)
    tm = each(lambda p: eye_cat + p, p_ab)
    for i in range(1, n_sq):
        last = i == n_sq - 1
        lhs = each(lambda t, n: t.astype(BF16) if last else jnp.concatenate([t.astype(BF16), n], axis=0), tm, nn_b)
        prod = each(lambda l, n: _dot(l, bd(n)), lhs, nn_b)
        tm = each(lambda t, p: t + p[:C], tm, prod)
        if not last:
            nn_b = each(lambda p: p[C:].astype(BF16), prod)
    ub = each(lambda t, x: _dot(t.astype(BF16), bd(x)).astype(BF16), tm, w)

    yv = each(lambda x, y, p, u: x[C:] + y[C:] + _dot(p, bd(u)), arh, pv, p_rb, ub)
    upd = each(lambda u, x, y, z: _dot_tn(jnp.concatenate([u, x], axis=0), jnp.concatenate([y, z], axis=0)),
               ub, vb, bh, kh)
    for (bi, gi), y, s, xl, up in zip(chains, yv, sbd, cum_last, upd):
        y_ref[bi, :, gi * W:(gi + 1) * W] = y
        h_scr[bi, gi] = (s * jnp.exp(xl) + up) * hmask

    @pl.when(c == pl.num_programs(0) - 1)
    def _():
        s_ref[...] = h_scr[...]


def _scan(r, dl, k, v, a, b):
    nb, t, dr = r.shape
    ng = dr // MXU_DIM
    C = SCAN_CHUNK
    spec = pl.BlockSpec((nb, C, dr), lambda c: (0, c, 0))
    return pl.pallas_call(
        _scan_kernel,
        grid=(t // C,),
        in_specs=[spec] * 6,
        out_specs=[spec, pl.BlockSpec((nb, ng, MXU_DIM, MXU_DIM), lambda c: (0, 0, 0, 0))],
        out_shape=[jax.ShapeDtypeStruct((nb, t, dr), F32),
                   jax.ShapeDtypeStruct((nb, ng, MXU_DIM, MXU_DIM), F32)],
        scratch_shapes=[pltpu.VMEM((nb, ng, MXU_DIM, MXU_DIM), F32)],
        compiler_params=_params("arbitrary"),
        name="wkv_scan",
    )(r, dl, k, v, a, b)


def _step_kernel(s_ref, r_ref, dl_ref, k_ref, v_ref, a_ref, b_ref, sout_ref, y_ref):
    S = s_ref[0]
    per_key = lambda ref: ref[...][None, :, :]
    sa = jnp.sum(S * per_key(a_ref), axis=1, keepdims=True)
    v = v_ref[...][:, None, :]
    s_new = S * jnp.exp(per_key(dl_ref)) + sa * per_key(b_ref) + v * per_key(k_ref)
    sout_ref[0] = s_new
    y_ref[...] = jnp.sum(s_new * per_key(r_ref), axis=1)


def _step(S0, r, dl, k, v, a, b):
    nb, nh, n, _ = S0.shape
    st = jnp.transpose(S0, (1, 2, 3, 0))
    sspec = pl.BlockSpec((1, n, n, nb), lambda h: (h, 0, 0, 0))
    vspec = pl.BlockSpec((n, nb), lambda h: (h, 0))
    s_new, y = pl.pallas_call(
        _step_kernel,
        grid=(nh,),
        in_specs=[sspec] + [vspec] * 6,
        out_specs=[sspec, vspec],
        out_shape=[jax.ShapeDtypeStruct(st.shape, F32), jax.ShapeDtypeStruct((nh * n, nb), F32)],
        compiler_params=_params("arbitrary"),
        name="wkv_step",
    )(st, *(x.T for x in (r, dl, k, v, a, b)))
    return jnp.transpose(s_new, (3, 0, 1, 2)), y.T


def _post_kernel(y_ref, bonus_ref, g_ref, gng_ref, gnb_ref, o_ref):
    y = y_ref[...]
    ones_bd = _head_mask(MXU_DIM, BF16)
    inv_n = 1.0 / RWKV_HEAD
    mu = _head_sum(y, ones_bd) * inv_n
    yc = y - mu
    var = _head_sum(yc * yc, ones_bd) * inv_n
    yn = yc * lax.rsqrt(var + GN_EPS) * gng_ref[...] + gnb_ref[...]
    o_ref[...] = (yn + bonus_ref[...]) * g_ref[...]


def _post(y, bonus, g, lw):
    n, dr = y.shape
    tm = _row_tile(n, 512)
    row = pl.BlockSpec((tm, dr), lambda i: (i, 0))
    return pl.pallas_call(
        _post_kernel,
        grid=(n // tm,),
        in_specs=[row, row, row, _const_spec(lw["gn_g"].shape), _const_spec(lw["gn_b"].shape)],
        out_specs=row,
        out_shape=jax.ShapeDtypeStruct((n, dr), F32),
        compiler_params=_params("arbitrary"),
        name="wkv_post",
    )(y, bonus, g, lw["gn_g"], lw["gn_b"])


def _conv_seq_kernel(h_ref, halo_ref, w_ref, b_ref, lng_ref, lnb_ref, o_ref, hp_scr, *, seq_tiles):
    tt = h_ref.shape[0]
    first = (pl.program_id(0) % seq_tiles) == 0
    hp_scr[0:CONV_HALO, :] = jnp.where(first, 0.0, halo_ref[...])
    hp_scr[CONV_HALO:, :] = h_ref[...]
    off = CONV_HALO - (CONV_WIDTH - 1)
    rc = CONV_ROW_CHUNK if tt % CONV_ROW_CHUNK == 0 else tt
    for c0 in range(0, o_ref.shape[1], LANES):
        lanes = slice(c0, c0 + LANES)
        for t0 in range(0, tt, rc):
            acc = jnp.broadcast_to(b_ref[:, lanes], (rc, LANES))
            for p in range(SUBLANES):
                rows = rc if p == 0 else rc + SUBLANES
                part = None
                for m in range((CONV_HALO + SUBLANES) // SUBLANES):
                    j = SUBLANES * m + p - off
                    if 0 <= j < CONV_WIDTH:
                        term = hp_scr[t0 + SUBLANES * m:t0 + SUBLANES * m + rows, lanes] * w_ref[j:j + 1, lanes]
                        part = term if part is None else part + term
                acc = acc + part[p:p + rc]
            o_ref[t0:t0 + rc, lanes] = acc
    z = _layer_norm(o_ref[...], lng_ref[...], lnb_ref[...])
    o_ref[...] = z * jax.nn.sigmoid(z)


def _conv_seq(h, seq_len, lw):
    n, dc = h.shape
    tt = _row_tile(seq_len, 256)
    assert tt % CONV_HALO == 0
    seq_tiles = seq_len // tt
    row = pl.BlockSpec((tt, dc), lambda i: (i, 0))
    halo = pl.BlockSpec((CONV_HALO, dc), lambda i: (jnp.maximum(i * (tt // CONV_HALO) - 1, 0), 0))
    consts = [lw["conv_w"], lw["conv_b"], lw["conv_ln_g"], lw["conv_ln_b"]]
    return pl.pallas_call(
        functools.partial(_conv_seq_kernel, seq_tiles=seq_tiles),
        grid=(n // tt,),
        in_specs=[row, halo] + [_const_spec(c.shape) for c in consts],
        out_specs=row,
        out_shape=jax.ShapeDtypeStruct((n, dc), F32),
        scratch_shapes=[pltpu.VMEM((CONV_HALO + tt, dc), F32)],
        compiler_params=_params("arbitrary"),
        name="conv_seq",
    )(h, h, *consts)


def _conv_step_kernel(buf_ref, h_ref, w_ref, b_ref, lng_ref, lnb_ref, o_ref, nbuf_ref):
    h = h_ref[...]
    acc = b_ref[...] + h * w_ref[CONV_WIDTH - 1:CONV_WIDTH, :]
    for j in range(CONV_WIDTH - 1):
        tap = buf_ref[j]
        acc = acc + tap * w_ref[j:j + 1, :]
        if j > 0:
            nbuf_ref[j - 1] = tap
    nbuf_ref[CONV_WIDTH - 2] = h
    z = _layer_norm(acc, lng_ref[...], lnb_ref[...])
    o_ref[...] = z * jax.nn.sigmoid(z)


def _conv_step(buf, h, lw):
    nb, wm1, dc = buf.shape
    bb = _row_tile(nb, 32)
    bspec = pl.BlockSpec((wm1, bb, dc), lambda i: (0, i, 0))
    row = pl.BlockSpec((bb, dc), lambda i: (i, 0))
    consts = [lw["conv_w"], lw["conv_b"], lw["conv_ln_g"], lw["conv_ln_b"]]
    out, nbuf = pl.pallas_call(
        _conv_step_kernel,
        grid=(nb // bb,),
        in_specs=[bspec, row] + [_const_spec(c.shape) for c in consts],
        out_specs=[row, bspec],
        out_shape=[jax.ShapeDtypeStruct((nb, dc), F32), jax.ShapeDtypeStruct((wm1, nb, dc), F32)],
        compiler_params=_params("arbitrary"),
        name="conv_step",
    )(jnp.transpose(buf, (1, 0, 2)), h, *consts)
    return out, jnp.transpose(nbuf, (1, 0, 2))


def _outproj_kernel(*refs, alpha, n_dst):
    x_ref, yr_ref, yc_ref, wo_ref, g_ref, b_ref, wr_ref, br_ref, h_ref, idx_ref, gates_ref = refs[n_dst:]
    tm = x_ref.shape[0]
    n_parts = 2 if tm % (2 * MXU_DIM) == 0 else 1
    rows = [slice(p * (tm // n_parts), (p + 1) * (tm // n_parts)) for p in range(n_parts)]
    mix = [_dot(jnp.concatenate([yr_ref[rs, :].astype(BF16), yc_ref[rs, :].astype(BF16)], axis=1), wo_ref[...])
           for rs in rows]
    hs = [_layer_norm(alpha * x_ref[rs, :] + m, g_ref[...], b_ref[...]) for rs, m in zip(rows, mix)]
    logit_parts = [_dot(h.astype(BF16), wr_ref[...]) + br_ref[...] for h in hs]
    for rs, h, logits in zip(rows, hs, logit_parts):
        h_ref[rs, :] = h
        ne = logits.shape[1]
        lane = lax.broadcasted_iota(jnp.int32, logits.shape, 1)
        slot = lax.broadcasted_iota(jnp.int32, (logits.shape[0], TOP_K), 1)
        vals = logits
        top_v, idx = [], jnp.zeros(slot.shape, jnp.int32)
        for k in range(TOP_K):
            m = jnp.max(vals, axis=-1, keepdims=True)
            e = jnp.min(jnp.where(vals == m, lane, ne), axis=-1, keepdims=True)
            top_v.append(m)
            idx = jnp.where(slot == k, e, idx)
            vals = jnp.where(lane == e, -jnp.inf, vals)
        expv = [jnp.exp(v - top_v[0]) for v in top_v]
        den = sum(expv[1:], expv[0])
        gates = jnp.zeros(slot.shape, F32)
        for k in range(TOP_K):
            gates = jnp.where(slot == k, expv[k] / den, gates)
        idx_ref[rs, :] = idx
        gates_ref[rs, :] = gates


def _outproj(x, y_rwkv, y_conv, lw, alpha, n_total, row0, dst=()):
    n, d = x.shape
    dr = y_rwkv.shape[1]
    dc = y_conv.shape[1]
    tm = _row_tile(math.gcd(n, row0) if row0 else n, 2 * MXU_DIM)
    blk0 = row0 // tm
    row = lambda c: pl.BlockSpec((tm, c), lambda i: (i, 0))
    out_row = lambda c: pl.BlockSpec((tm, c), lambda i: (blk0 + i, 0))
    consts = [lw["w_out"], lw["ln1_g"], lw["ln1_b"], lw["w_router"], lw["b_router"]]
    return pl.pallas_call(
        functools.partial(_outproj_kernel, alpha=alpha, n_dst=len(dst)),
        grid=(n // tm,),
        in_specs=[pl.BlockSpec(memory_space=pl.ANY)] * len(dst) + [row(d), row(dr), row(dc)]
        + [_const_spec(c.shape) for c in consts],
        out_specs=[out_row(d), out_row(TOP_K), out_row(TOP_K)],
        out_shape=[jax.ShapeDtypeStruct((n_total, d), F32), jax.ShapeDtypeStruct((n_total, TOP_K), jnp.int32),
                   jax.ShapeDtypeStruct((n_total, TOP_K), F32)],
        input_output_aliases={i: i for i in range(len(dst))},
        compiler_params=_params("arbitrary"),
        name="outproj",
    )(*dst, x, y_rwkv, y_conv, *consts)


def _route(top_idx, ne, moe_tm):
    n_tok = top_idx.shape[0]
    n_assign = n_tok * TOP_K
    e_flat = top_idx.reshape(n_assign).astype(jnp.int32)
    onehot = (e_flat[:, None] == jnp.arange(ne, dtype=jnp.int32)[None, :]).astype(jnp.int32)
    counts = onehot.sum(0)
    rank = jnp.take_along_axis(jnp.cumsum(onehot, axis=0), e_flat[:, None], axis=1)[:, 0] - 1
    nblk_e = (counts + moe_tm - 1) // moe_tm
    blk_end = jnp.cumsum(nblk_e)
    blk_start = blk_end - nblk_e
    n_active = blk_end[-1]
    nb_max = n_assign // moe_tm + ne
    pos = blk_start[e_flat] * moe_tm + rank
    sorted_tok = (jnp.argsort(e_flat, stable=True) // TOP_K).astype(jnp.int32)
    sorted_tok = jnp.concatenate([sorted_tok, jnp.zeros((GATHER_GROUP,), jnp.int32)])
    cstart = jnp.cumsum(counts) - counts
    q = jnp.arange(nb_max, dtype=jnp.int32)
    q_eff = jnp.minimum(q, n_active - 1)
    blk_e = jnp.minimum(jnp.searchsorted(blk_end, q_eff, side="right"), ne - 1).astype(jnp.int32)
    row_in_e = (q_eff - blk_start[blk_e]) * moe_tm
    valid = jnp.clip(counts[blk_e] - row_in_e, 0, moe_tm)
    valid = jnp.where(q < n_active, valid, 0).astype(jnp.int32)
    src_start = (cstart[blk_e] + row_in_e).astype(jnp.int32)
    return pos.astype(jnp.int32), sorted_tok, src_start, blk_e, q_eff.astype(jnp.int32), valid


def _gather_kernel(qeff_ref, valid_ref, src_ref, tok_ref, h_hbm, o_ref, buf, sem):
    s = pl.program_id(0)
    nb = pl.num_programs(0) - 1
    per = GATHER_GROUP // SUBLANES
    groups = lambda nvalid: (nvalid + GATHER_GROUP - 1) // GATHER_GROUP

    @pl.when(s < nb)
    def _():
        slot = s % 2
        base = src_ref[s]

        def issue(g, c):
            for i in range(GATHER_GROUP):
                tok = tok_ref[base + g * GATHER_GROUP + i]
                pltpu.make_async_copy(h_hbm.at[pl.ds(tok, 1)],
                                      buf.at[slot, g * per + i // SUBLANES, pl.ds(i % SUBLANES, 1)],
                                      sem.at[slot]).start()
            return c

        lax.fori_loop(0, groups(valid_ref[s]), issue, 0)

    @pl.when(s > 0)
    def _():
        slot = (s - 1) % 2
        nvalid = valid_ref[s - 1]

        @pl.when(nvalid > 0)
        def _():
            def wait(g, c):
                pltpu.make_async_copy(h_hbm.at[pl.ds(0, SUBLANES)], buf.at[slot, g], sem.at[slot]).wait()
                return c

            lax.fori_loop(0, groups(nvalid) * per, wait, 0)
            rows = lax.broadcasted_iota(jnp.int32, o_ref.shape, 0)
            o_ref[...] = jnp.where(rows < nvalid, buf[slot].reshape(o_ref.shape), 0.0).astype(o_ref.dtype)


def _gather_rows(h, sorted_tok, src_start, q_eff, valid, moe_tm):
    n_tok, d = h.shape
    nb_max = q_eff.shape[0]
    grid_spec = pltpu.PrefetchScalarGridSpec(
        num_scalar_prefetch=4,
        grid=(nb_max + 1,),
        in_specs=[pl.BlockSpec(memory_space=pl.ANY)],
        out_specs=pl.BlockSpec((moe_tm, d), lambda s, qe, va, sr, tk: (qe[jnp.maximum(s - 1, 0)], 0)),
        scratch_shapes=[pltpu.VMEM((2, moe_tm // SUBLANES, SUBLANES, d), F32), pltpu.SemaphoreType.DMA((2,))],
    )
    return pl.pallas_call(
        _gather_kernel,
        grid_spec=grid_spec,
        out_shape=jax.ShapeDtypeStruct((nb_max * moe_tm, d), BF16),
        compiler_params=_params("arbitrary"),
        name="moe_gather",
    )(q_eff, valid, src_start, sorted_tok, h)


def _moe_kernel(be_ref, qeff_ref, valid_ref, x_ref, wg_ref, wl_ref, bg_ref, bl_ref, wd_ref, bd_ref, o_ref,
                wg_s, wl_s, wd_s, *, sub):
    q = pl.program_id(0)
    j = pl.program_id(1)
    nvalid = valid_ref[q]
    unit = sub // 2
    n_units = (nvalid + unit - 1) // unit
    n_all = o_ref.shape[0] // unit

    def fill(s, value):
        r0 = pl.multiple_of(s * unit, unit)
        o_ref[pl.ds(r0, unit), :] = jnp.broadcast_to(value, (unit, o_ref.shape[1]))

    def up_proj(r0, rows):
        x = x_ref[pl.ds(r0, rows), :]
        g = jnp.minimum(_dot(x, wg_s[...]) + bg_ref[0], SWIGLU_LIMIT)
        l = jnp.clip(_dot(x, wl_s[...]) + bl_ref[0], -SWIGLU_LIMIT, SWIGLU_LIMIT)
        return (g * jax.nn.sigmoid(SWIGLU_ALPHA * g) * (l + 1.0)).astype(BF16)

    def down_proj(r0, rows, act):
        o_ref[pl.ds(r0, rows), :] += _dot(act, wd_s[...])

    def rows_block(r0, rows):
        down_proj(r0, rows, up_proj(r0, rows))

    @pl.when(nvalid > 0)
    def _():
        @pl.when(j == 0)
        def _():
            lax.fori_loop(0, n_units, lambda s, c: (fill(s, bd_ref[0]), c)[1], 0)
            lax.fori_loop(n_units, n_all, lambda s, c: (fill(s, jnp.zeros((1, 1), F32)), c)[1], 0)

        def cast_weights():
            wg_s[...] = wg_ref[0].astype(BF16)
            wl_s[...] = wl_ref[0].astype(BF16)
            wd_s[...] = wd_ref[0].astype(BF16)

        def pair(i, c):
            r0 = pl.multiple_of(i * 2 * sub, unit)
            act_a = up_proj(r0, sub)
            act_b = up_proj(r0 + sub, sub)
            down_proj(r0, sub, act_a)
            down_proj(r0 + sub, sub, act_b)
            return c

        big = n_units >= 4

        @pl.when(big)
        def _():
            cast_weights()
            pair(0, 0)

        @pl.when(jnp.logical_not(big))
        def _():
            cast_weights()
            rows_block(0, unit)

        first = jnp.where(big, 4, 1)
        rest = n_units - first
        n_pairs = rest // 4
        tail = rest - n_pairs * 4
        lax.fori_loop(0, n_pairs, lambda i, c: pair(i + 1, c), 0)
        t0 = (first + n_pairs * 4) * unit
        for units in (2, 1):
            @pl.when((tail & units) != 0)
            def _():
                rows_block(pl.multiple_of(t0 + (tail & ~(2 * units - 1)) * unit, unit), units * unit)


def _moe_experts(x_sorted, blk_e, q_eff, valid, lw, moe_tm, sub, tf):
    n_rows, d = x_sorted.shape
    ne, _, f2 = lw["w_gu"].shape
    f = f2 // 2
    nf = f // tf
    nb_max = q_eff.shape[0]
    b_gu = lw["b_gu"].reshape(ne, 1, f2)
    b_down = lw["b_down"].reshape(ne, 1, d)

    def jf(q, j, va):
        return jnp.where(va[q] > 0, j, nf - 1)

    grid_spec = pltpu.PrefetchScalarGridSpec(
        num_scalar_prefetch=3,
        grid=(nb_max, nf),
        in_specs=[
            pl.BlockSpec((moe_tm, d), lambda q, j, be, qe, va: (qe[q], 0)),
            pl.BlockSpec((1, d, tf), lambda q, j, be, qe, va: (be[q], 0, jf(q, j, va))),
            pl.BlockSpec((1, d, tf), lambda q, j, be, qe, va: (be[q], 0, nf + jf(q, j, va))),
            pl.BlockSpec((1, 1, tf), lambda q, j, be, qe, va: (be[q], 0, jf(q, j, va))),
            pl.BlockSpec((1, 1, tf), lambda q, j, be, qe, va: (be[q], 0, nf + jf(q, j, va))),
            pl.BlockSpec((1, tf, d), lambda q, j, be, qe, va: (be[q], jf(q, j, va), 0)),
            pl.BlockSpec((1, 1, d), lambda q, j, be, qe, va: (be[q], 0, 0)),
        ],
        out_specs=pl.BlockSpec((moe_tm, d), lambda q, j, be, qe, va: (qe[q], 0)),
        scratch_shapes=[pltpu.VMEM((d, tf), BF16), pltpu.VMEM((d, tf), BF16), pltpu.VMEM((tf, d), BF16)],
    )
    return pl.pallas_call(
        functools.partial(_moe_kernel, sub=sub),
        grid_spec=grid_spec,
        out_shape=jax.ShapeDtypeStruct((n_rows, d), F32),
        compiler_params=_params("arbitrary", "arbitrary", vmem_limit_bytes=MOE_VMEM_LIMIT_BYTES),
        name="moe_experts",
    )(blk_e, q_eff, valid, x_sorted, lw["w_gu"], lw["w_gu"], b_gu, b_gu, lw["w_down"], b_down)


def _combine_kernel(pos_ref, h_ref, gates_ref, yrows_hbm, g_ref, b_ref, op_ref, os_ref, *scratch,
                    alpha, n_first, n_tiles):
    i = pl.program_id(0)
    tm = h_ref.shape[0]
    n_slots = COMBINE_LAG + 1
    slots = tuple(zip(scratch[:n_slots], scratch[n_slots:]))

    def fetch(tile, buf, sem):
        for r in range(tm):
            for k in range(TOP_K):
                p = pos_ref[(tile * tm + r) * TOP_K + k]
                pltpu.make_async_copy(yrows_hbm.at[pl.ds(p, 1)],
                                      buf.at[k, r // SUBLANES, pl.ds(r % SUBLANES, 1)], sem).start()

    def wait(buf, sem):
        for k in range(TOP_K):
            pltpu.make_async_copy(buf.at[k], buf.at[k], sem).wait()

    def finish(tile, buf):
        gates = gates_ref[...]
        slot = lambda k: buf[k].reshape(tm, buf.shape[-1])
        ffn = slot(0) * gates[:, 0:1]
        for k in range(1, TOP_K):
            ffn = ffn + slot(k) * gates[:, k:k + 1]
        y = _layer_norm(alpha * h_ref[...] + ffn, g_ref[...], b_ref[...])

        @pl.when(tile < n_first)
        def _():
            op_ref[...] = y

        @pl.when(tile >= n_first)
        def _():
            os_ref[...] = y

    for step in range(min(COMBINE_LAG, n_tiles)):
        @pl.when(i == step)
        def _():
            fetch(step, *slots[step % n_slots])

    for phase in range(n_slots):
        @pl.when((i >= COMBINE_LAG) & (i < n_tiles) & (i % n_slots == phase))
        def _():
            done = slots[(phase - COMBINE_LAG) % n_slots]
            wait(*done)
            fetch(i, *slots[phase])
            finish(i - COMBINE_LAG, done[0])

    for step in range(max(n_tiles, COMBINE_LAG), n_tiles + COMBINE_LAG):
        @pl.when(i == step)
        def _():
            done = slots[(step - COMBINE_LAG) % n_slots]
            wait(*done)
            finish(step - COMBINE_LAG, done[0])


def _combine(h, gates, pos, y_rows, lw, alpha, n_prompt):
    n_tok, d = h.shape
    tm = _row_tile(math.gcd(n_prompt, n_tok - n_prompt), 128)
    n_first = n_prompt // tm
    n_tiles = n_tok // tm
    done = lambda i: jnp.maximum(i - COMBINE_LAG, 0)
    n_slots = COMBINE_LAG + 1
    buf = pltpu.VMEM((TOP_K, tm // SUBLANES, SUBLANES, d), F32)
    grid_spec = pltpu.PrefetchScalarGridSpec(
        num_scalar_prefetch=1,
        grid=(n_tiles + COMBINE_LAG,),
        in_specs=[
            pl.BlockSpec((tm, d), lambda i, p: (done(i), 0)),
            pl.BlockSpec((tm, TOP_K), lambda i, p: (done(i), 0)),
            pl.BlockSpec(memory_space=pl.ANY),
            pl.BlockSpec((1, d), lambda i, p: (0, 0)),
            pl.BlockSpec((1, d), lambda i, p: (0, 0)),
        ],
        out_specs=[pl.BlockSpec((tm, d), lambda i, p: (jnp.minimum(done(i), n_first - 1), 0)),
                   pl.BlockSpec((tm, d), lambda i, p: (jnp.maximum(done(i) - n_first, 0), 0))],
        scratch_shapes=[buf] * n_slots + [pltpu.SemaphoreType.DMA(())] * n_slots,
    )
    return pl.pallas_call(
        functools.partial(_combine_kernel, alpha=alpha, n_first=n_first, n_tiles=n_tiles),
        grid_spec=grid_spec,
        out_shape=[jax.ShapeDtypeStruct((n_prompt, d), F32), jax.ShapeDtypeStruct((n_tok - n_prompt, d), F32)],
        compiler_params=_params("arbitrary"),
        name="moe_combine",
    )(pos, h, gates, y_rows, lw["ln2_g"], lw["ln2_b"])


def _moe_tiles(n_assign, ne, d, f):
    unit = MOE_SUB // 2
    mean = -(-n_assign // ne)
    tm = max(MOE_SUB, -(-(mean * 11 // 10) // unit) * unit)
    for tf in (512, 256):
        tf = min(tf, f)
        blocks = 2 * tm * d * 2 + 2 * tm * d * 4
        weights = 3 * d * tf * (2 * 4 + 2)
        temps = MOE_SUB * 2 * tf * 4
        if blocks + weights + temps <= MOE_VMEM_LIMIT_BYTES:
            return tm, tf
    return min(tm, 4 * MOE_SUB), min(256, f)


_VEC_PARAMS = ("mu_rkv", "w0", "a0", "k_k", "k_a", "r_k", "gn_g", "gn_b", "conv_b", "conv_ln_g", "conv_ln_b",
               "ln1_g", "ln1_b", "b_router", "ln2_g", "ln2_b")
_BF16_PARAMS = ("w_in", "w_A", "w_B", "a_A", "a_B", "g_A", "g_B", "w_out", "w_router")


def _diag_blocks(s):
    nb, ng = s.shape[:2]
    s6 = s.reshape(nb, ng, HEADS_PER_GROUP, RWKV_HEAD, HEADS_PER_GROUP, RWKV_HEAD)
    d = jnp.stack([s6[:, :, h, :, h, :] for h in range(HEADS_PER_GROUP)], axis=2)
    return d.reshape(nb, ng * HEADS_PER_GROUP, RWKV_HEAD, RWKV_HEAD)


def _layer(xp, xs, sx, srkv, swkv, sconv, lw, alpha):
    nbp, t, d = xp.shape
    nbs = xs.shape[0]
    xp2 = xp.reshape(nbp * t, d)
    xs2 = xs.reshape(nbs, d)

    up_last, hgp, dlp, gp, rp, kp, vp, ap, bp, bonp = _inproj(xp2, None, None, t, lw)
    us, hgs, dls, gs, rs, ks, vs, as_, bs, bons = _inproj(xs2, sx, srkv, 1, lw)
    dr = dlp.shape[1]

    seq = lambda z: z.reshape(nbp, t, dr)
    yp_raw, s_end = _scan(seq(rp), seq(dlp), seq(kp), seq(vp), seq(ap), seq(bp))
    s_new, ys_raw = _step(swkv, rs, dls, ks, vs, as_, bs)
    yrp = _post(yp_raw.reshape(nbp * t, dr), bonp, gp, lw)
    yrs = _post(ys_raw, bons, gs, lw)

    ycp = _conv_seq(hgp, t, lw)
    ycs, nbuf = _conv_step(sconv, hgs, lw)

    n_prompt = nbp * t
    n_tok = n_prompt + nbs
    dst = _outproj(xp2, yrp, ycp, lw, alpha, n_tok, 0)
    h_all, top_idx, gates = _outproj(xs2, yrs, ycs, lw, alpha, n_tok, n_prompt, dst=tuple(dst))

    ne = lw["w_router"].shape[1]
    moe_tm, moe_tf = _moe_tiles(n_tok * TOP_K, ne, d, lw["w_down"].shape[1])
    pos, sorted_tok, src_start, blk_e, q_eff, valid = _route(top_idx, ne, moe_tm)
    x_sorted = _gather_rows(h_all, sorted_tok, src_start, q_eff, valid, moe_tm)
    y_rows = _moe_experts(x_sorted, blk_e, q_eff, valid, lw, moe_tm, MOE_SUB, moe_tf)
    yp, ys = _combine(h_all, gates, pos, y_rows, lw, alpha, n_prompt)
    yp = yp.reshape(nbp, t, d)
    ys = ys.reshape(nbs, 1, d)
    p_state = (xp[:, -1], up_last.reshape(nbp, -1, SUBLANES, 3 * dr)[:, -1, -1], _diag_blocks(s_end),
               hgp.reshape(nbp, t, -1)[:, t - (CONV_WIDTH - 1):])
    s_state = (xs2, us, s_new, nbuf)
    return yp, ys, p_state, s_state


def kernel(x_prompt, x_sample, state_shift_x, state_shift_rkv, state_wkv, state_conv, w_in, mu_x, mu_rkv, w0, w_A,
           w_B, a0, a_A, a_B, g_A, g_B, k_k, k_a, r_k, gn_g, gn_b, conv_w, conv_b, conv_ln_g, conv_ln_b, w_out,
           ln1_g, ln1_b, w_router, b_router, w_gu, b_gu, w_down, b_down, ln2_g, ln2_b):
    params = dict(w_in=w_in, mu_x=mu_x, mu_rkv=mu_rkv, w0=w0, w_A=w_A, w_B=w_B, a0=a0, a_A=a_A, a_B=a_B, g_A=g_A,
                  g_B=g_B, k_k=k_k, k_a=k_a, r_k=r_k, gn_g=gn_g, gn_b=gn_b, conv_w=conv_w, conv_b=conv_b,
                  conv_ln_g=conv_ln_g, conv_ln_b=conv_ln_b, w_out=w_out, ln1_g=ln1_g, ln1_b=ln1_b,
                  w_router=w_router, b_router=b_router, w_gu=w_gu, b_gu=b_gu, w_down=w_down, b_down=b_down,
                  ln2_g=ln2_g, ln2_b=ln2_b)
    depth = w_in.shape[0]
    assert x_sample.shape[1] == 1, "the sample group advances one token per step"
    alpha = (2.0 * depth) ** 0.25
    xp, xs = x_prompt, x_sample
    p_states, s_states = [], []
    for l in range(depth):
        lw = {name: p[l] for name, p in params.items()}
        for name in _VEC_PARAMS:
            lw[name] = lw[name].reshape(1, -1)
        for name in _BF16_PARAMS:
            lw[name] = lw[name].astype(BF16)
        xp, xs, p_st, s_st = _layer(xp, xs, state_shift_x[l], state_shift_rkv[l], state_wkv[l], state_conv[l],
                                    lw, alpha)
        p_states.append(p_st)
        s_states.append(s_st)
    stack = lambda states, i: jnp.stack([st[i] for st in states])
    return (xp, xs,
            stack(p_states, 0), stack(p_states, 1), stack(p_states, 2), stack(p_states, 3),
            stack(s_states, 0), stack(s_states, 1), stack(s_states, 2), stack(s_states, 3))
```
